```python
import math
import jax, jax.numpy as jnp
from jax import lax
import numpy as np

D_MODEL = 1024
BATCH = 1
SEQ = 16384
DEPTH = 4
DEC_BATCH = 16
DEC_SEQ = 2048
PAST_LEN = 128

EPS = 1e-6
NEG_INF = -1e30
N_MOD = 6
N_BRANCH = 4
A_GROUPS = 4
A_GROUP_W = 192
A_W = A_GROUPS * A_GROUP_W
B_PAIRS = ((128, 1), (512, 4), (2048, 16))
B_N_GROUPS = 3
B_HEADS_PER_GROUP = 2
B_HEADS = B_N_GROUPS * B_HEADS_PER_GROUP
B_HEAD_DIM = 64
B_QKV_W = B_HEADS * B_HEAD_DIM
B_OUT_W = B_HEADS_PER_GROUP * B_HEAD_DIM
REL_BUCKETS = 32
REL_MAX_DIST = 1024
C_HEADS = 4
C_HEAD_W = 96
C_W = C_HEADS * C_HEAD_W
C_CHUNK = 128
D_HEADS = 4
D_Q_LORA = 384
D_KV_LORA = 320
D_NOPE = 64
D_ROPE = 32
D_V = 64
D_OUT_W = D_HEADS * D_V
D_QBLK = 128
ROPE_THETA = 10000.0
IN_SPLITS = (A_W, B_QKV_W, B_QKV_W, B_QKV_W, C_W, C_W, D_Q_LORA, D_KV_LORA, D_ROPE, N_BRANCH * D_MODEL)
MIX_W = A_W + 3 * B_QKV_W + 2 * C_W + D_Q_LORA + D_KV_LORA + D_ROPE
IN_W = MIX_W + N_BRANCH * D_MODEL
D_FF = 4 * D_MODEL

kernel_name = "hybrid_bidir_encoder_fnet_longnet_gmlp_mla"


def _rmsnorm(x, g):
    xf = x.astype(jnp.float32)
    y = xf * lax.rsqrt(jnp.mean(xf * xf, axis=-1, keepdims=True) + EPS)
    return (y * g.astype(jnp.float32)).astype(x.dtype)


def _layernorm(x, g, b):
    xf = x.astype(jnp.float32)
    mu = jnp.mean(xf, axis=-1, keepdims=True)
    xc = xf - mu
    y = xc * lax.rsqrt(jnp.mean(xc * xc, axis=-1, keepdims=True) + EPS)
    return (y * g.astype(jnp.float32) + b.astype(jnp.float32)).astype(x.dtype)


def _split_points():
    return np.cumsum(np.array(IN_SPLITS))[:-1].tolist()


def _t5_bucket(rel):
    half = REL_BUCKETS // 2
    max_exact = half // 2
    ret = jnp.where(rel > 0, half, 0)
    n = jnp.abs(rel)
    large = max_exact + (jnp.log(jnp.maximum(n, max_exact).astype(jnp.float32) / max_exact)
                         / math.log(REL_MAX_DIST / max_exact) * (half - max_exact)).astype(jnp.int32)
    large = jnp.minimum(large, half - 1)
    return ret + jnp.where(n < max_exact, n, large)


def _rope(x, pos):
    half = D_ROPE // 2
    inv = ROPE_THETA ** (-jnp.arange(half, dtype=jnp.float32) / half)
    ang = pos[:, None].astype(jnp.float32) * inv[None, :]
    cos = jnp.cos(ang)[:, None, :]
    sin = jnp.sin(ang)[:, None, :]
    xf = x.astype(jnp.float32)
    x1, x2 = xf[..., :half], xf[..., half:]
    return jnp.concatenate([x1 * cos - x2 * sin, x1 * sin + x2 * cos], axis=-1).astype(x.dtype)


def _fourier_mix(a):
    B, S, _ = a.shape
    ag = a.reshape(B, S, A_GROUPS, A_GROUP_W).astype(jnp.float32)
    f = jnp.fft.fft2(ag, axes=(1, 3), norm="ortho").real
    return f.reshape(B, S, A_W).astype(a.dtype)


def _dilated_group(q, k, v, bias_table, dil, half):
    B, S, H, hd = q.shape
    L = S // dil
    N = B * dil
    blk = half
    nb = -(-L // blk)
    Lp = nb * blk

    def to_sub(t):
        return t.reshape(B, L, dil, H, hd).transpose(0, 2, 1, 3, 4).reshape(N, L, H, hd)

    def windows(t):
        tp = jnp.pad(to_sub(t), ((0, 0), (blk, Lp - L + blk), (0, 0), (0, 0))).reshape(N, nb + 2, blk, H, hd)
        return jnp.concatenate([tp[:, :-2], tp[:, 1:-1], tp[:, 2:]], axis=2)

    qs = jnp.pad(to_sub(q), ((0, 0), (0, Lp - L), (0, 0), (0, 0))).reshape(N, nb, blk, H, hd)
    kw = windows(k)
    vw = windows(v)
    qi = jnp.arange(blk)
    ki = jnp.arange(3 * blk)
    rel = ki[None, :] - blk - qi[:, None]
    band = jnp.abs(rel) <= half
    kpos = jnp.arange(nb)[:, None] * blk - blk + ki[None, :]
    kvalid = (kpos >= 0) & (kpos < L)
    mask = band[None, :, :] & kvalid[:, None, :]
    bias = jnp.transpose(bias_table[_t5_bucket(rel * dil)], (2, 0, 1)).astype(jnp.float32)
    s = jnp.einsum('nbqhd,nbkhd->nbhqk', qs, kw, preferred_element_type=jnp.float32) * (hd ** -0.5) + bias
    s = jnp.where(mask[None, :, None], s, NEG_INF)
    lse = jax.nn.logsumexp(s, axis=-1)
    p = jnp.exp(s - lse[..., None])
    o = jnp.einsum('nbhqk,nbkhd->nbqhd', p.astype(v.dtype), vw)
    o = o.reshape(N, Lp, H, hd)[:, :L]
    lse = lse.transpose(0, 1, 3, 2).reshape(N, Lp, H)[:, :L]
    o = o.reshape(B, dil, L, H, hd).transpose(0, 2, 1, 3, 4).reshape(B, S, H, hd)
    lse = lse.reshape(B, dil, L, H).transpose(0, 2, 1, 3).reshape(B, S, H)
    return o, lse


def _dilated_attention(q, k, v, rel_bias):
    B, S, _ = q.shape
    shp = (B, S, B_N_GROUPS, B_HEADS_PER_GROUP, B_HEAD_DIM)
    q, k, v = q.reshape(shp), k.reshape(shp), v.reshape(shp)
    outs, lses = [], []
    for g, (window, dil) in enumerate(B_PAIRS):
        hs = slice(g * B_HEADS_PER_GROUP, (g + 1) * B_HEADS_PER_GROUP)
        o, l = _dilated_group(q[:, :, g], k[:, :, g], v[:, :, g], rel_bias[:, hs], dil, window // (2 * dil))
        outs.append(o)
        lses.append(l)
    o = jnp.stack(outs).astype(jnp.float32)
    alpha = jax.nn.softmax(jnp.stack(lses), axis=0)
    out = jnp.sum(alpha[..., None] * o, axis=0)
    return out.reshape(B, S, B_OUT_W).astype(q.dtype)


def _spatial_gating(u, v, ln_g, ln_b, w_s, b_s):
    B, S, _ = u.shape
    vn = _layernorm(v, ln_g, ln_b)
    vc = vn.reshape(B, S // C_CHUNK, C_CHUNK, C_HEADS, C_HEAD_W)
    mixed = jnp.einsum('hpq,bnqhc->bnphc', w_s, vc) + b_s.T[:, :, None]
    return u * mixed.reshape(B, S, C_W).astype(u.dtype)


def _mla(cq, ckv, kr, q_norm_g, kv_norm_g, w_uq, w_ukv):
    B, S, _ = cq.shape
    pos = jnp.arange(S)
    q = (_rmsnorm(cq, q_norm_g) @ w_uq).reshape(B, S, D_HEADS, D_NOPE + D_ROPE)
    kv = (_rmsnorm(ckv, kv_norm_g) @ w_ukv).reshape(B, S, D_HEADS, D_NOPE + D_V)
    q_nope = q[..., :D_NOPE]
    q_pe = _rope(q[..., D_NOPE:], pos)
    k_nope = kv[..., :D_NOPE]
    v = kv[..., D_NOPE:]
    k_pe = _rope(kr[:, :, None, :], pos)[:, :, 0]
    scale = (D_NOPE + D_ROPE) ** -0.5
    nqb = S // D_QBLK

    def blocks(t):
        return t.reshape(B, nqb, D_QBLK, t.shape[2], t.shape[3]).swapaxes(0, 1)

    def attend(qb):
        qn, qp = qb
        s = (jnp.einsum('bqhd,bkhd->bhqk', qn, k_nope, preferred_element_type=jnp.float32)
             + jnp.einsum('bqhr,bkr->bhqk', qp, k_pe, preferred_element_type=jnp.float32)) * scale
        p = jax.nn.softmax(s, axis=-1)
        return jnp.einsum('bhqk,bkhd->bqhd', p.astype(v.dtype), v)

    o = lax.map(attend, (blocks(q_nope), blocks(q_pe)))
    return o.swapaxes(0, 1).reshape(B, S, D_OUT_W)


def _layer(x, c, rel_bias, ada_w, ada_b, norm1_g, w_in, q_norm_g, kv_norm_g, w_uq, w_ukv,
           sgu_ln_g, sgu_ln_b, sgu_w, sgu_b, p_a, p_b, p_c, p_d, w_o, norm2_g, w1, w2):
    B, S, D = x.shape
    mod = (jax.nn.silu(c) @ ada_w + ada_b)[:, None, :]
    sh1, sc1, gt1, sh2, sc2, gt2 = jnp.split(mod, N_MOD, axis=-1)
    h = _rmsnorm(x, norm1_g) * (1 + sc1) + sh1
    z = h @ w_in
    za, bq, bk, bv, cu, cv, dcq, dckv, dkr, zg = jnp.split(z, _split_points(), axis=-1)
    ya = _fourier_mix(za) @ p_a
    yb = _dilated_attention(bq, bk, bv, rel_bias) @ p_b
    yc = _spatial_gating(cu, cv, sgu_ln_g, sgu_ln_b, sgu_w, sgu_b) @ p_c
    yd = _mla(dcq, dckv, dkr, q_norm_g, kv_norm_g, w_uq, w_ukv) @ p_d
    g = jax.nn.sigmoid(zg).reshape(B, S, N_BRANCH, D)
    merged = g[:, :, 0] * ya + g[:, :, 1] * yb + g[:, :, 2] * yc + g[:, :, 3] * yd
    x = x + gt1 * (merged @ w_o)
    h = _rmsnorm(x, norm2_g) * (1 + sc2) + sh2
    x = x + gt2 * (jnp.square(jax.nn.relu(h @ w1)) @ w2)
    return x


def setup_inputs(seed: int = 0) -> dict:
    key = jax.random.key(seed)
    ks = jax.random.split(key, 32)
    f32 = jnp.float32

    def nrm(k, shape, fan_in, scale=1.0):
        return jax.random.normal(k, shape, f32) * (scale * fan_in ** -0.5)

    def gain(k, shape):
        return 1.0 + 0.05 * jax.random.normal(k, shape, f32)

    D = D_MODEL
    return {
        "x_prompt": jax.random.normal(ks[0], (BATCH, SEQ, D), f32),
        "x_sample": jax.random.normal(ks[1], (DEC_BATCH, DEC_SEQ, D), f32),
        "c_prompt": jax.random.normal(ks[2], (BATCH, D), f32),
        "c_sample": jax.random.normal(ks[3], (DEC_BATCH, D), f32),
        "rel_bias": 0.1 * jax.random.normal(ks[4], (REL_BUCKETS, B_HEADS), f32),
        "ada_w": nrm(ks[5], (DEPTH, D, N_MOD * D), D, 0.5),
        "ada_b": 0.02 * jax.random.normal(ks[6], (DEPTH, N_MOD * D), f32),
        "norm1_g": gain(ks[7], (DEPTH, D)),
        "w_in": nrm(ks[8], (DEPTH, D, IN_W), D),
        "mla_q_norm_g": gain(ks[9], (DEPTH, D_Q_LORA)),
        "mla_kv_norm_g": gain(ks[10], (DEPTH, D_KV_LORA)),
        "mla_w_uq": nrm(ks[11], (DEPTH, D_Q_LORA, D_HEADS * (D_NOPE + D_ROPE)), D_Q_LORA),
        "mla_w_ukv": nrm(ks[12], (DEPTH, D_KV_LORA, D_HEADS * (D_NOPE + D_V)), D_KV_LORA),
        "sgu_ln_g": gain(ks[13], (DEPTH, C_W)),
        "sgu_ln_b": 0.02 * jax.random.normal(ks[14], (DEPTH, C_W), f32),
        "sgu_w": nrm(ks[15], (DEPTH, C_HEADS, C_CHUNK, C_CHUNK), C_CHUNK),
        "sgu_b": gain(ks[16], (DEPTH, C_HEADS, C_CHUNK)),
        "p_a": nrm(ks[17], (DEPTH, A_W, D), A_W),
        "p_b": nrm(ks[18], (DEPTH, B_OUT_W, D), B_OUT_W),
        "p_c": nrm(ks[19], (DEPTH, C_W, D), C_W),
        "p_d": nrm(ks[20], (DEPTH, D_OUT_W, D), D_OUT_W),
        "w_o": nrm(ks[21], (DEPTH, D, D), D),
        "norm2_g": gain(ks[22], (DEPTH, D)),
        "mlp_w1": nrm(ks[23], (DEPTH, D, D_FF), D),
        "mlp_w2": nrm(ks[24], (DEPTH, D_FF, D), D_FF),
        "final_g": gain(ks[25], (D,)),
    }


def reference(x_prompt, x_sample, c_prompt, c_sample, rel_bias, ada_w, ada_b, norm1_g, w_in,
              mla_q_norm_g, mla_kv_norm_g, mla_w_uq, mla_w_ukv, sgu_ln_g, sgu_ln_b, sgu_w, sgu_b,
              p_a, p_b, p_c, p_d, w_o, norm2_g, mlp_w1, mlp_w2, final_g):
    def run(x, c):
        for l in range(DEPTH):
            x = _layer(x, c, rel_bias, ada_w[l], ada_b[l], norm1_g[l], w_in[l],
                       mla_q_norm_g[l], mla_kv_norm_g[l], mla_w_uq[l], mla_w_ukv[l],
                       sgu_ln_g[l], sgu_ln_b[l], sgu_w[l], sgu_b[l],
                       p_a[l], p_b[l], p_c[l], p_d[l], w_o[l], norm2_g[l], mlp_w1[l], mlp_w2[l])
        return _rmsnorm(x, final_g)

    y_prompt = run(x_prompt, c_prompt)
    y_sample = run(x_sample, c_sample)
    return (y_prompt, y_sample)
```

```python
import functools
import math

import jax
import jax.numpy as jnp
import numpy as np
from jax import lax
from jax.experimental import pallas as pl
from jax.experimental.pallas import tpu as pltpu

F32 = jnp.float32
BF16 = jnp.bfloat16
HIGHEST = lax.Precision.HIGHEST

D_MODEL = 1024
DEPTH = 4
EPS = 1e-6
NEG_INF = -1e30
N_MOD = 6
A_GROUPS = 4
A_GROUP_W = 192
A_W = A_GROUPS * A_GROUP_W
B_PAIRS = ((128, 1), (512, 4), (2048, 16))
B_HEAD_DIM = 64
B_QKV_W = 384
B_GROUP_W = 128
B_HALF = 64
B_QBLK = 128
REL_BUCKETS = 32
REL_MAX_DIST = 1024
C_HEADS = 4
C_HEAD_W = 96
C_W = 384
C_CHUNK = 128
D_HEADS = 4
D_Q_LORA = 384
D_KV_LORA = 320
D_KV_PAD = 384
D_NOPE = 64
D_ROPE = 32
D_V = 64
D_HEAD_PAD = 128
D_PAD_W = D_HEADS * D_HEAD_PAD
ROPE_THETA = 10000.0
D_FF = 4 * D_MODEL
MIX_W = A_W + 3 * B_QKV_W + 2 * C_W + D_Q_LORA + D_KV_LORA + D_ROPE

VMEM_LIMIT_BYTES = 56 * 1024 * 1024
LANE = 128
ROW_TILE = 512
DFT_DIRECT_MAX = 2048
DFT_N2 = 128
ATT_TILE = 1024
BAND_ROWS = 2048


def _cparams(*sem):
    return pltpu.CompilerParams(dimension_semantics=sem, vmem_limit_bytes=VMEM_LIMIT_BYTES)


def _resident(shape):
    nd = len(shape)
    return pl.BlockSpec(shape, lambda *_: (0,) * nd, pipeline_mode=pl.Buffered(1))


def _dot(a, b):
    return jnp.dot(a, b, preferred_element_type=F32)


def _dot_nt(a, b):
    return lax.dot_general(a, b, (((1,), (1,)), ((), ())), preferred_element_type=F32)


def _ada_kernel(c_ref, w_ref, b_ref, o_ref):
    c = c_ref[...]
    a = c * jax.nn.sigmoid(c)
    o_ref[0] = jnp.dot(a, w_ref[0], preferred_element_type=F32, precision=HIGHEST) + b_ref[0]


def _ada_mod(c_all, ada_w, ada_b):
    rows = c_all.shape[0]
    tn = 1536
    return pl.pallas_call(
        _ada_kernel,
        grid=(DEPTH, N_MOD * D_MODEL // tn),
        in_specs=[
            pl.BlockSpec((rows, D_MODEL), lambda l, j: (0, 0)),
            pl.BlockSpec((1, D_MODEL, tn), lambda l, j: (l, 0, j)),
            pl.BlockSpec((1, 1, tn), lambda l, j: (l, 0, j)),
        ],
        out_specs=pl.BlockSpec((1, rows, tn), lambda l, j: (l, 0, j)),
        out_shape=jax.ShapeDtypeStruct((DEPTH, rows, N_MOD * D_MODEL), F32),
        compiler_params=_cparams("parallel", "parallel"),
        name="ada_mod",
    )(c_all, ada_w, ada_b.reshape(DEPTH, 1, N_MOD * D_MODEL))


def _fold_kernel(w_ref, m_ref, o_ref):
    o_ref[0] = jnp.dot(w_ref[0], m_ref[...], preferred_element_type=F32, precision=HIGHEST).astype(BF16)


def _fold_channel_dft(w_a):
    n = np.arange(A_GROUP_W)
    idx = (n[:, None] * n[None, :]) % A_GROUP_W
    ang = 2.0 * np.pi * idx / A_GROUP_W
    eye = np.eye(A_GROUPS)
    m = np.concatenate([np.kron(eye, np.cos(ang)), np.kron(eye, np.sin(ang))], axis=1).astype(np.float32)
    return pl.pallas_call(
        _fold_kernel,
        grid=(DEPTH,),
        in_specs=[
            pl.BlockSpec((1, D_MODEL, A_W), lambda l: (l, 0, 0)),
            pl.BlockSpec((A_W, 2 * A_W), lambda l: (0, 0)),
        ],
        out_specs=pl.BlockSpec((1, D_MODEL, 2 * A_W), lambda l: (l, 0, 0)),
        out_shape=jax.ShapeDtypeStruct((DEPTH, D_MODEL, 2 * A_W), BF16),
        compiler_params=_cparams("parallel"),
        name="fold_channel_dft",
    )(w_a, jnp.asarray(m))


def _modulated_norm(x, g, shift, scale):
    ms = jnp.mean(x * x, axis=-1, keepdims=True)
    return (x * lax.rsqrt(ms + EPS) * g) * (1.0 + scale) + shift


def _inproj_kernel(x_ref, mod_ref, g1_ref, cos_ref, sin_ref,
                   wfft_ref, wb_ref, wc_ref, wdq_ref, wdkv_ref, wkr_ref,
                   qg_ref, kvg_ref, wq2_ref, wkv2_ref, lng_ref, lnb_ref, ones_ref,
                   y1_ref, y2_ref, bqkv_ref, cu_ref, cvn_ref, mq_ref, mk_ref, mv_ref):
    x = x_ref[0]
    h = _modulated_norm(x, g1_ref[...], mod_ref[0, 0:1, :], mod_ref[0, 1:2, :])
    hb = h.astype(BF16)

    y = _dot(hb, wfft_ref[...])
    y1_ref[0] = y[:, :A_W].astype(BF16)
    y2_ref[0] = y[:, A_W:].astype(BF16)

    zb = _dot(hb, wb_ref[...])
    bqkv_ref[0, :, :B_QKV_W] = (zb[:, :B_QKV_W] * (B_HEAD_DIM ** -0.5)).astype(BF16)
    bqkv_ref[0, :, B_QKV_W:] = zb[:, B_QKV_W:].astype(BF16)

    zc = _dot(hb, wc_ref[...])
    cu_ref[0] = zc[:, :C_W].astype(BF16)
    cv = zc[:, C_W:]
    mu = jnp.mean(cv, axis=-1, keepdims=True)
    cc = cv - mu
    var = jnp.mean(cc * cc, axis=-1, keepdims=True)
    cvn_ref[0] = (cc * lax.rsqrt(var + EPS) * lng_ref[...] + lnb_ref[...]).astype(BF16)

    cos = cos_ref[...]
    sin = sin_ref[...]
    cos4 = jnp.concatenate([cos] * D_HEADS, axis=1)
    sin4 = jnp.concatenate([sin] * D_HEADS, axis=1)

    dcq = _dot(hb, wdq_ref[...])
    msq = jnp.mean(dcq * dcq, axis=-1, keepdims=True)
    qn = (dcq * lax.rsqrt(msq + EPS) * qg_ref[...]).astype(BF16)
    qq = _dot(qn, wq2_ref[...])
    q = (qq[:, :D_PAD_W] * cos4 + qq[:, D_PAD_W:] * sin4) * ((D_NOPE + D_ROPE) ** -0.5)
    mq_ref[0] = q.astype(BF16)

    dckv = _dot(hb, wdkv_ref[...])
    mskv = jnp.sum(dckv * dckv, axis=-1, keepdims=True) * (1.0 / D_KV_LORA)
    kvn = (dckv * lax.rsqrt(mskv + EPS) * kvg_ref[...]).astype(BF16)
    kv = _dot(kvn, wkv2_ref[...])
    kr = _dot(hb, wkr_ref[...])
    kpe = kr[:, :D_HEAD_PAD] * cos + kr[:, D_HEAD_PAD:] * sin
    mk_ref[0] = (kv[:, :D_PAD_W] + jnp.concatenate([kpe] * D_HEADS, axis=1)).astype(BF16)
    mv_ref[0] = (kv[:, D_PAD_W:] + ones_ref[...]).astype(BF16)


def _inproj(x, mod, lw, cos_t, sin_t, ones_row):
    B, S, _ = x.shape
    tm = ROW_TILE
    row = lambda w: pl.BlockSpec((1, tm, w), lambda b, i: (b, i, 0))
    tab = pl.BlockSpec((tm, D_HEAD_PAD), lambda b, i: (i, 0))
    weights = [lw["wfft"], lw["wb"], lw["wc"], lw["wdq"], lw["wdkv"], lw["wkr"],
               lw["qg"], lw["kvg"], lw["wq2"], lw["wkv2"], lw["lng"], lw["lnb"], ones_row]
    out_w = [(A_W, BF16), (A_W, BF16), (3 * B_QKV_W, BF16), (C_W, BF16), (C_W, BF16),
             (D_PAD_W, BF16), (D_PAD_W, BF16), (D_PAD_W, BF16)]
    return pl.pallas_call(
        _inproj_kernel,
        grid=(B, S // tm),
        in_specs=[row(D_MODEL),
                  pl.BlockSpec((1, N_MOD, D_MODEL), lambda b, i: (b, 0, 0)),
                  _resident(lw["g1"].shape), tab, tab] + [_resident(w.shape) for w in weights],
        out_specs=[row(w) for w, _ in out_w],
        out_shape=[jax.ShapeDtypeStruct((B, S, w), dt) for w, dt in out_w],
        compiler_params=_cparams("parallel", "parallel"),
        name="inproj",
    )(x, mod, lw["g1"], cos_t, sin_t, *weights)


def _dft_direct_kernel(m_ref, y1_ref, y2_ref, o_ref, *, seq, tr):
    r0 = pl.multiple_of(pl.program_id(1) * tr, tr)
    mc = m_ref[pl.ds(r0, tr), :seq]
    ms = m_ref[pl.ds(r0, tr), seq:]
    o_ref[0] = (_dot(mc, y1_ref[0]) + _dot(ms, y2_ref[0])).astype(BF16)


def _dft_direct(m, y1, y2):
    B, S, _ = y1.shape
    tr = ROW_TILE
    yspec = pl.BlockSpec((1, S, A_W), lambda b, i: (b, 0, 0))
    return pl.pallas_call(
        functools.partial(_dft_direct_kernel, seq=S, tr=tr),
        grid=(B, S // tr),
        in_specs=[_resident(m.shape), yspec, yspec],
        out_specs=pl.BlockSpec((1, tr, A_W), lambda b, i: (b, i, 0)),
        out_shape=jax.ShapeDtypeStruct((B, S, A_W), BF16),
        compiler_params=_cparams("parallel", "arbitrary"),
        name="dft_direct",
    )(m, y1, y2)


def _fft_stage1_kernel(m_ref, y1_ref, y2_ref, ar_ref, ai_ref, *, n1):
    rhs = jnp.concatenate([y1_ref[0], y2_ref[0]], axis=0)
    out = _dot(m_ref[...], rhs)
    ar_ref[0] = out[:n1].astype(BF16)
    ai_ref[0] = out[n1:].astype(BF16)


def _fft_stage2_kernel(h_ref, ar_ref, ai_ref, o_ref, *, t):
    for j in range(t):
        rhs = jnp.concatenate([ar_ref[0, j], ai_ref[0, j]], axis=0)
        o_ref[0, j] = _dot(h_ref[j], rhs).astype(BF16)


def _fft_two_stage(m1, h2, y1, y2):
    B, S, _ = y1.shape
    n2 = DFT_N2
    n1 = S // n2
    width = n2 * A_W
    lb = 8 * A_W
    y1v = y1.reshape(B, n1, width)
    y2v = y2.reshape(B, n1, width)
    blk = pl.BlockSpec((1, n1, lb), lambda b, j: (b, 0, j))
    ar, ai = pl.pallas_call(
        functools.partial(_fft_stage1_kernel, n1=n1),
        grid=(B, width // lb),
        in_specs=[_resident(m1.shape), blk, blk],
        out_specs=[blk, blk],
        out_shape=[jax.ShapeDtypeStruct((B, n1, width), BF16)] * 2,
        compiler_params=_cparams("parallel", "parallel"),
        name="fft_stage1",
    )(m1, y1v, y2v)
    t = 8
    slab = pl.BlockSpec((1, t, n2, A_W), lambda b, j: (b, j, 0, 0))
    out = pl.pallas_call(
        functools.partial(_fft_stage2_kernel, t=t),
        grid=(B, n1 // t),
        in_specs=[pl.BlockSpec((t, n2, 2 * n2), lambda b, j: (j, 0, 0)), slab, slab],
        out_specs=slab,
        out_shape=jax.ShapeDtypeStruct((B, n1, n2, A_W), BF16),
        compiler_params=_cparams("parallel", "parallel"),
        name="fft_stage2",
    )(h2, ar.reshape(B, n1, n2, A_W), ai.reshape(B, n1, n2, A_W))
    return out.transpose(0, 2, 1, 3).reshape(B, S, A_W)


def _dft_tables(S):
    scale = 1.0 / math.sqrt(S * A_GROUP_W)

    def cs(idx, period):
        ang = idx.astype(F32) * (2.0 * math.pi / period)
        return jnp.cos(ang), jnp.sin(ang)

    if S <= DFT_DIRECT_MAX:
        n = jnp.arange(S, dtype=jnp.int32)
        c, s = cs((n[:, None] * n[None, :]) % S, S)
        return {"direct": jnp.concatenate([c * scale, -s * scale], axis=1).astype(BF16)}
    n2 = DFT_N2
    n1 = S // n2
    a = jnp.arange(n1, dtype=jnp.int32)
    c1, s1 = cs((a[:, None] * a[None, :]) % n1, n1)
    m1 = jnp.concatenate([jnp.concatenate([c1, -s1], axis=1), jnp.concatenate([s1, c1], axis=1)], axis=0)
    k1 = a[:, None, None]
    k2 = jnp.arange(n2, dtype=jnp.int32)[None, :, None]
    nn = jnp.arange(n2, dtype=jnp.int32)[None, None, :]
    hc, hs = cs((nn * (k1 + n1 * k2)) % S, S)
    h2 = jnp.concatenate([hc * scale, -hs * scale], axis=2)
    return {"m1": m1.astype(BF16), "h2": h2.astype(BF16)}


def _fourier_seq(tables, y1, y2):
    if "direct" in tables:
        return _dft_direct(tables["direct"], y1, y2)
    return _fft_two_stage(tables["m1"], tables["h2"], y1, y2)


def _band_kernel(bias_ref, q_ref, k_ref, v_ref, o_ref, lse_ref, *, seq, rows, nseq, kwin):
    nsub = rows // B_QBLK
    last_blk = seq // B_QBLK - 1
    j = pl.program_id(1)
    head0 = lax.broadcasted_iota(jnp.int32, (B_QBLK, B_GROUP_W), 1) < B_HEAD_DIM

    def body(t, carry):
        si = t // nsub
        b = t % nsub
        r0 = pl.multiple_of(b * B_QBLK, B_QBLK)
        gb = j * nsub + b
        ws = pl.multiple_of(jnp.clip(gb * B_QBLK - B_HALF, 0, seq - kwin), B_HALF)
        tix = jnp.where(gb == 0, 0, jnp.where(gb == last_blk, 2, 1))
        q = q_ref[si, pl.ds(r0, B_QBLK), :]
        kw = k_ref[si, pl.ds(ws, kwin), :]
        vw = v_ref[si, pl.ds(ws, kwin), :]
        zero = jnp.zeros_like(q)
        outs, lses = [], []
        for h in range(2):
            qh = jnp.where(head0 if h == 0 else jnp.logical_not(head0), q, zero)
            s = _dot_nt(qh, kw) + bias_ref[tix, h]
            m = jnp.max(s, axis=1, keepdims=True)
            p = jnp.exp(s - m)
            l = jnp.sum(p, axis=1, keepdims=True)
            outs.append(_dot(p.astype(BF16), vw) * (1.0 / l))
            lses.append(jnp.broadcast_to(m + jnp.log(l), (B_QBLK, B_GROUP_W)))
        o_ref[si, pl.ds(r0, B_QBLK), :] = jnp.where(head0, outs[0], outs[1]).astype(BF16)
        lse_ref[si, pl.ds(r0, B_QBLK), :] = jnp.where(head0, lses[0], lses[1])
        return carry

    lax.fori_loop(0, nseq * nsub, body, 0)


def _band_attention(bias, arr, cols):
    N, L, _ = arr.shape
    kwin = bias.shape[-1]
    if L >= BAND_ROWS:
        nseq, rows = 1, BAND_ROWS
    else:
        nseq, rows = BAND_ROWS // L, L
    cq, ck, cv = cols
    out_spec = pl.BlockSpec((nseq, rows, B_GROUP_W), lambda n, j: (n, j, 0))
    return pl.pallas_call(
        functools.partial(_band_kernel, seq=L, rows=rows, nseq=nseq, kwin=kwin),
        grid=(N // nseq, L // rows),
        in_specs=[
            _resident(bias.shape),
            pl.BlockSpec((nseq, rows, B_GROUP_W), lambda n, j: (n, j, cq)),
            pl.BlockSpec((nseq, L, B_GROUP_W), lambda n, j: (n, 0, ck)),
            pl.BlockSpec((nseq, L, B_GROUP_W), lambda n, j: (n, 0, cv)),
        ],
        out_specs=[out_spec, out_spec],
        out_shape=[jax.ShapeDtypeStruct((N, L, B_GROUP_W), BF16), jax.ShapeDtypeStruct((N, L, B_GROUP_W), F32)],
        compiler_params=_cparams("parallel", "arbitrary"),
        name="band_attention",
    )(bias, arr, arr, arr)


def _t5_bucket(rel):
    half = REL_BUCKETS // 2
    max_exact = half // 2
    ret = jnp.where(rel > 0, half, 0)
    n = jnp.abs(rel)
    large = max_exact + (jnp.log(jnp.maximum(n, max_exact).astype(F32) / max_exact)
                         / math.log(REL_MAX_DIST / max_exact) * (half - max_exact)).astype(jnp.int32)
    large = jnp.minimum(large, half - 1)
    return ret + jnp.where(n < max_exact, n, large)


def _band_bias_tiles(rel_bias_g, dil, L):
    kwin = min(2 * B_QBLK, L)
    i = np.arange(B_QBLK)[:, None]
    jj = np.arange(kwin)[None, :]
    tiles = []
    for off in (0, -B_HALF, B_QBLK - kwin):
        rel = jj + off - i
        band = np.abs(rel) <= B_HALF
        vals = rel_bias_g[_t5_bucket(jnp.asarray(rel * dil, dtype=jnp.int32))]
        vals = jnp.transpose(vals, (2, 0, 1)).astype(F32)
        tiles.append(jnp.where(jnp.asarray(band)[None], vals, NEG_INF))
    return jnp.stack(tiles)


def _dilated_branch(bias_tiles, bqkv):
    B, S, _ = bqkv.shape
    outs = []
    for g, (_, dil) in enumerate(B_PAIRS):
        if dil == 1:
            o, lse = _band_attention(bias_tiles[g], bqkv, (g, 3 + g, 6 + g))
        else:
            L = S // dil
            sub = bqkv.reshape(B, L, dil, 9, B_GROUP_W)[:, :, :, g::3, :]
            sub = sub.transpose(0, 2, 1, 3, 4).reshape(B * dil, L, 3 * B_GROUP_W)
            o, lse = _band_attention(bias_tiles[g], sub, (0, 1, 2))
            o = o.reshape(B, dil, L, B_GROUP_W).transpose(0, 2, 1, 3).reshape(B, S, B_GROUP_W)
            lse = lse.reshape(B, dil, L, B_GROUP_W).transpose(0, 2, 1, 3).reshape(B, S, B_GROUP_W)
        outs.append((o, lse))
    return outs


def _mla_kernel(q_ref, k_ref, v_ref, o_ref, m_sc, acc_sc, *, tk):
    kv = pl.program_id(2)

    @pl.when(kv == 0)
    def _():
        m_sc[...] = jnp.full(m_sc.shape, -jnp.inf, F32)
        acc_sc[...] = jnp.zeros(acc_sc.shape, F32)

    for h in range(D_HEADS):
        sl = slice(h * D_HEAD_PAD, (h + 1) * D_HEAD_PAD)
        s = _dot_nt(q_ref[0, :, sl], k_ref[0, :, sl])
        m_prev = m_sc[h]
        m_new = jnp.maximum(m_prev, jnp.max(s, axis=1, keepdims=True))
        alpha = jnp.exp(m_prev - m_new)
        p = jnp.exp(s - jnp.concatenate([m_new] * (tk // LANE), axis=1))
        acc_sc[h] = alpha * acc_sc[h] + _dot(p.astype(BF16), v_ref[0, :, sl])
        m_sc[h] = m_new

    @pl.when(kv == pl.num_programs(2) - 1)
    def _():
        for h in range(D_HEADS):
            acc = acc_sc[h]
            o_ref[0, :, h * D_HEAD_PAD:(h + 1) * D_HEAD_PAD] = (acc * (1.0 / acc[:, D_V:D_V + 1])).astype(BF16)


def _mla_attention(q, k, v):
    B, S, _ = q.shape
    tq = tk = min(ATT_TILE, S)
    return pl.pallas_call(
        functools.partial(_mla_kernel, tk=tk),
        grid=(B, S // tq, S // tk),
        in_specs=[
            pl.BlockSpec((1, tq, D_PAD_W), lambda b, i, j: (b, i, 0)),
            pl.BlockSpec((1, tk, D_PAD_W), lambda b, i, j: (b, j, 0)),
            pl.BlockSpec((1, tk, D_PAD_W), lambda b, i, j: (b, j, 0)),
        ],
        out_specs=pl.BlockSpec((1, tq, D_PAD_W), lambda b, i, j: (b, i, 0)),
        out_shape=jax.ShapeDtypeStruct((B, S, D_PAD_W), BF16),
        scratch_shapes=[pltpu.VMEM((D_HEADS, tq, LANE), F32), pltpu.VMEM((D_HEADS, tq, D_HEAD_PAD), F32)],
        compiler_params=_cparams("parallel", "parallel", "arbitrary"),
        name="mla_attention",
    )(q, k, v)


def _merge_kernel(x_ref, mod_ref, g1_ref, ya_ref, ob0_ref, ob1_ref, ob2_ref, l0_ref, l1_ref, l2_ref,
                  cu_ref, cvn_ref, yd_ref, wg_ref, pa_ref, pb_ref, pc_ref, pd_ref, wo_ref,
                  ws_ref, bs_ref, o_ref, *, tm):
    x = x_ref[0]
    h = _modulated_norm(x, g1_ref[...], mod_ref[0, 0:1, :], mod_ref[0, 1:2, :])
    hb = h.astype(BF16)

    def gate(i):
        return jax.nn.sigmoid(_dot(hb, wg_ref[:, i * D_MODEL:(i + 1) * D_MODEL]))

    merged = gate(0) * _dot(ya_ref[0], pa_ref[...])

    l0, l1, l2 = l0_ref[0], l1_ref[0], l2_ref[0]
    mx = jnp.maximum(jnp.maximum(l0, l1), l2)
    e0, e1, e2 = jnp.exp(l0 - mx), jnp.exp(l1 - mx), jnp.exp(l2 - mx)
    yb = (e0 * ob0_ref[0].astype(F32) + e1 * ob1_ref[0].astype(F32) + e2 * ob2_ref[0].astype(F32)) / (e0 + e1 + e2)
    merged = merged + gate(1) * _dot(yb.astype(BF16), pb_ref[...])

    head = lax.broadcasted_iota(jnp.int32, (C_CHUNK, C_W), 1) // C_HEAD_W
    chunks = []
    for c in range(tm // C_CHUNK):
        rs = slice(c * C_CHUNK, (c + 1) * C_CHUNK)
        mm = _dot(ws_ref[...], cvn_ref[0, rs, :])
        mixed = bs_ref[...]
        for hh in range(C_HEADS):
            mixed = mixed + jnp.where(head == hh, mm[hh * C_CHUNK:(hh + 1) * C_CHUNK], 0.0)
        chunks.append((cu_ref[0, rs, :].astype(F32) * mixed).astype(BF16))
    yc = jnp.concatenate(chunks, axis=0)
    merged = merged + gate(2) * _dot(yc, pc_ref[...])

    merged = merged + gate(3) * _dot(yd_ref[0], pd_ref[...])
    o_ref[0] = x + mod_ref[0, 2:3, :] * _dot(merged.astype(BF16), wo_ref[...])


def _merge(x, mod, lw, ya, band, cu, cvn, yd):
    B, S, _ = x.shape
    tm = ROW_TILE
    row = lambda w: pl.BlockSpec((1, tm, w), lambda b, i: (b, i, 0))
    weights = [lw["wg"], lw["pa"], lw["pb"], lw["pc"], lw["pd"], lw["wo"], lw["ws"], lw["bs"]]
    (ob0, l0), (ob1, l1), (ob2, l2) = band
    return pl.pallas_call(
        functools.partial(_merge_kernel, tm=tm),
        grid=(B, S // tm),
        in_specs=[row(D_MODEL), pl.BlockSpec((1, N_MOD, D_MODEL), lambda b, i: (b, 0, 0)), _resident(lw["g1"].shape),
                  row(A_W)] + [row(B_GROUP_W)] * 6 + [row(C_W), row(C_W), row(D_PAD_W)]
                 + [_resident(w.shape) for w in weights],
        out_specs=row(D_MODEL),
        out_shape=jax.ShapeDtypeStruct((B, S, D_MODEL), F32),
        compiler_params=_cparams("parallel", "parallel"),
        name="merge",
    )(x, mod, lw["g1"], ya, ob0, ob1, ob2, l0, l1, l2, cu, cvn, yd, *weights)


def _mlp_kernel(x_ref, mod_ref, g2_ref, w1_ref, w2_ref, gf_ref, o_ref, *, final):
    x = x_ref[0]
    h = _modulated_norm(x, g2_ref[...], mod_ref[0, 3:4, :], mod_ref[0, 4:5, :])
    a = jnp.maximum(_dot(h.astype(BF16), w1_ref[...]), 0.0)
    y = x + mod_ref[0, 5:6, :] * _dot((a * a).astype(BF16), w2_ref[...])
    if final:
        ms = jnp.mean(y * y, axis=-1, keepdims=True)
        y = y * lax.rsqrt(ms + EPS) * gf_ref[...]
    o_ref[0] = y


def _mlp(x, mod, lw, final_g, final):
    B, S, _ = x.shape
    tm = ROW_TILE
    row = pl.BlockSpec((1, tm, D_MODEL), lambda b, i: (b, i, 0))
    return pl.pallas_call(
        functools.partial(_mlp_kernel, final=final),
        grid=(B, S // tm),
        in_specs=[row, pl.BlockSpec((1, N_MOD, D_MODEL), lambda b, i: (b, 0, 0)), _resident(lw["g2"].shape),
                  _resident(lw["w1"].shape), _resident(lw["w2"].shape), _resident(final_g.shape)],
        out_specs=row,
        out_shape=jax.ShapeDtypeStruct((B, S, D_MODEL), F32),
        compiler_params=_cparams("parallel", "parallel"),
        name="mlp",
    )(x, mod, lw["g2"], lw["w1"], lw["w2"], final_g)


def _head_slots(w, per_head, take):
    k = w.shape[0]
    wh = w.reshape(k, D_HEADS, per_head)[:, :, take]
    pad = D_HEAD_PAD - wh.shape[-1]
    return jnp.pad(wh, ((0, 0), (0, 0), (0, pad))).reshape(k, D_PAD_W)


def _rope_rot_cols(w):
    half = D_ROPE // 2
    return jnp.concatenate([-w[..., half:], w[..., :half]], axis=-1)


def _prepare_layer(l, w_in, wfft, norm1_g, q_norm_g, kv_norm_g, w_uq, w_ukv, sgu_ln_g, sgu_ln_b, sgu_w, sgu_b,
                   p_a, p_b, p_c, p_d, w_o, norm2_g, w1, w2):
    wi = w_in[l]
    o = A_W
    wb = wi[:, o:o + 3 * B_QKV_W]; o += 3 * B_QKV_W
    wc = wi[:, o:o + 2 * C_W]; o += 2 * C_W
    wdq = wi[:, o:o + D_Q_LORA]; o += D_Q_LORA
    wdkv = wi[:, o:o + D_KV_LORA]; o += D_KV_LORA
    wkr = wi[:, o:o + D_ROPE]; o += D_ROPE
    wg = wi[:, o:]

    def rope_slot(w):
        return jnp.pad(w, ((0, 0), (D_NOPE, D_HEAD_PAD - D_NOPE - D_ROPE)))

    wkr2 = jnp.concatenate([rope_slot(wkr), rope_slot(_rope_rot_cols(wkr))], axis=1)

    uq = w_uq[l].reshape(D_Q_LORA, D_HEADS, D_NOPE + D_ROPE)
    uq_rot = jnp.concatenate([jnp.zeros_like(uq[..., :D_NOPE]), _rope_rot_cols(uq[..., D_NOPE:])], axis=-1)
    padq = ((0, 0), (0, 0), (0, D_HEAD_PAD - D_NOPE - D_ROPE))
    wq2 = jnp.concatenate([jnp.pad(uq, padq).reshape(D_Q_LORA, D_PAD_W),
                           jnp.pad(uq_rot, padq).reshape(D_Q_LORA, D_PAD_W)], axis=1)

    ukv = jnp.pad(w_ukv[l], ((0, D_KV_PAD - D_KV_LORA), (0, 0)))
    wkv2 = jnp.concatenate([_head_slots(ukv, D_NOPE + D_V, slice(0, D_NOPE)),
                            _head_slots(ukv, D_NOPE + D_V, slice(D_NOPE, D_NOPE + D_V))], axis=1)

    pd_pad = jnp.pad(p_d[l].reshape(D_HEADS, D_V, D_MODEL), ((0, 0), (0, D_HEAD_PAD - D_V), (0, 0)))
    bias_sgu = jnp.repeat(sgu_b[l].T, C_HEAD_W, axis=1)
    row = lambda v: v.reshape(1, -1).astype(F32)
    return {
        "g1": row(norm1_g[l]), "g2": row(norm2_g[l]),
        "wfft": wfft[l], "wb": wb.astype(BF16), "wc": wc.astype(BF16), "wdq": wdq.astype(BF16),
        "wdkv": jnp.pad(wdkv, ((0, 0), (0, D_KV_PAD - D_KV_LORA))).astype(BF16),
        "wkr": wkr2.astype(BF16), "wg": wg.astype(BF16),
        "qg": row(q_norm_g[l]), "kvg": row(jnp.pad(kv_norm_g[l], (0, D_KV_PAD - D_KV_LORA))),
        "wq2": wq2.astype(BF16), "wkv2": wkv2.astype(BF16),
        "lng": row(sgu_ln_g[l]), "lnb": row(sgu_ln_b[l]),
        "ws": sgu_w[l].reshape(C_HEADS * C_CHUNK, C_CHUNK).astype(BF16), "bs": bias_sgu.astype(F32),
        "pa": p_a[l].astype(BF16), "pb": p_b[l].astype(BF16), "pc": p_c[l].astype(BF16),
        "pd": pd_pad.reshape(D_PAD_W, D_MODEL).astype(BF16), "wo": w_o[l].astype(BF16),
        "w1": w1[l].astype(BF16), "w2": w2[l].astype(BF16),
    }


def _rope_tables(S):
    half = D_ROPE // 2
    inv = ROPE_THETA ** (-jnp.arange(half, dtype=F32) / half)
    ang = jnp.arange(S, dtype=F32)[:, None] * inv[None, :]
    cos = jnp.cos(ang)
    sin = jnp.sin(ang)
    pad = jnp.zeros((S, D_HEAD_PAD - D_NOPE - D_ROPE), F32)
    cos_t = jnp.concatenate([jnp.ones((S, D_NOPE), F32), cos, cos, pad], axis=1)
    sin_t = jnp.concatenate([jnp.zeros((S, D_NOPE), F32), sin, sin, pad], axis=1)
    return cos_t, sin_t


def _run_group(x, mods, layers, bias_tiles_fn, final_g):
    B, S, _ = x.shape
    cos_t, sin_t = _rope_tables(S)
    tables = _dft_tables(S)
    bias_tiles = bias_tiles_fn(S)
    ones_row = jnp.zeros((D_HEADS, D_HEAD_PAD), F32).at[:, D_V].set(1.0).reshape(1, D_PAD_W)
    gf = final_g.reshape(1, D_MODEL)
    for l, lw in enumerate(layers):
        mod = mods[l]
        y1, y2, bqkv, cu, cvn, mq, mk, mv = _inproj(x, mod, lw, cos_t, sin_t, ones_row)
        ya = _fourier_seq(tables, y1, y2)
        band = _dilated_branch(bias_tiles, bqkv)
        yd = _mla_attention(mq, mk, mv)
        x = _merge(x, mod, lw, ya, band, cu, cvn, yd)
        x = _mlp(x, mod, lw, gf, final=(l == len(layers) - 1))
    return x


def kernel(x_prompt, x_sample, c_prompt, c_sample, rel_bias, ada_w, ada_b, norm1_g, w_in, mla_q_norm_g, mla_kv_norm_g, mla_w_uq, mla_w_ukv, sgu_ln_g, sgu_ln_b, sgu_w, sgu_b, p_a, p_b, p_c, p_d, w_o, norm2_g, mlp_w1, mlp_w2, final_g):
    nb_p = c_prompt.shape[0]
    nb_s = c_sample.shape[0]
    rows = -(-(nb_p + nb_s) // 8) * 8
    c_all = jnp.concatenate([c_prompt, c_sample, jnp.zeros((rows - nb_p - nb_s, D_MODEL), F32)], axis=0)
    mod_all = _ada_mod(c_all, ada_w, ada_b).reshape(DEPTH, rows, N_MOD, D_MODEL)
    mods_p = mod_all[:, :nb_p]
    mods_s = mod_all[:, nb_p:nb_p + nb_s]

    wfft = _fold_channel_dft(w_in[:, :, :A_W])
    layers = [_prepare_layer(l, w_in, wfft, norm1_g, mla_q_norm_g, mla_kv_norm_g, mla_w_uq, mla_w_ukv,
                             sgu_ln_g, sgu_ln_b, sgu_w, sgu_b, p_a, p_b, p_c, p_d, w_o, norm2_g, mlp_w1, mlp_w2)
              for l in range(DEPTH)]

    def bias_tiles_fn(S):
        return [_band_bias_tiles(rel_bias[:, 2 * g:2 * g + 2], dil, S // dil) for g, (_, dil) in enumerate(B_PAIRS)]

    y_prompt = _run_group(x_prompt, mods_p, layers, bias_tiles_fn, final_g)
    y_sample = _run_group(x_sample, mods_s, layers, bias_tiles_fn, final_g)
    return (y_prompt, y_sample)
```

```python
import functools
import math

import jax
import jax.numpy as jnp
import numpy as np
from jax import lax
from jax.experimental import pallas as pl
from jax.experimental.pallas import tpu as pltpu

F32 = jnp.float32
BF16 = jnp.bfloat16
HIGHEST = lax.Precision.HIGHEST

D_MODEL = 1024
DEPTH = 4
EPS = 1e-6
NEG_INF = -1e30
N_MOD = 6
A_GROUPS = 4
A_GROUP_W = 192
A_W = A_GROUPS * A_GROUP_W
B_PAIRS = ((128, 1), (512, 4), (2048, 16))
B_HEAD_DIM = 64
B_QKV_W = 384
B_GROUP_W = 128
B_HALF = 64
B_QBLK = 128
REL_BUCKETS = 32
REL_MAX_DIST = 1024
C_HEADS = 4
C_HEAD_W = 96
C_W = 384
C_CHUNK = 128
D_HEADS = 4
D_Q_LORA = 384
D_KV_LORA = 320
D_KV_PAD = 384
D_NOPE = 64
D_ROPE = 32
D_V = 64
D_HEAD_PAD = 128
D_PAD_W = D_HEADS * D_HEAD_PAD
ROPE_THETA = 10000.0
D_FF = 4 * D_MODEL
MIX_W = A_W + 3 * B_QKV_W + 2 * C_W + D_Q_LORA + D_KV_LORA + D_ROPE

VMEM_LIMIT_BYTES = 56 * 1024 * 1024
LANE = 128
ROW_TILE = 512
DFT_DIRECT_MAX = 2048
DFT_N2 = 128
DFT_SUB = 16
ATT_TILE = 1024
BAND_ROWS = 2048
BAND_UNROLL = 4


def _cparams(*sem):
    return pltpu.CompilerParams(dimension_semantics=sem, vmem_limit_bytes=VMEM_LIMIT_BYTES)


def _resident(shape):
    nd = len(shape)
    return pl.BlockSpec(shape, lambda *_: (0,) * nd, pipeline_mode=pl.Buffered(1))


def _dot(a, b):
    return jnp.dot(a, b, preferred_element_type=F32)


def _dot_nt(a, b):
    return lax.dot_general(a, b, (((1,), (1,)), ((), ())), preferred_element_type=F32)


def _ada_kernel(c_ref, w_ref, b_ref, o_ref):
    c = c_ref[...]
    a = c * jax.nn.sigmoid(c)
    o_ref[0] = jnp.dot(a, w_ref[0], preferred_element_type=F32, precision=HIGHEST) + b_ref[0]


def _ada_mod(c_all, ada_w, ada_b):
    rows = c_all.shape[0]
    tn = 1536
    return pl.pallas_call(
        _ada_kernel,
        grid=(DEPTH, N_MOD * D_MODEL // tn),
        in_specs=[
            pl.BlockSpec((rows, D_MODEL), lambda l, j: (0, 0)),
            pl.BlockSpec((1, D_MODEL, tn), lambda l, j: (l, 0, j)),
            pl.BlockSpec((1, 1, tn), lambda l, j: (l, 0, j)),
        ],
        out_specs=pl.BlockSpec((1, rows, tn), lambda l, j: (l, 0, j)),
        out_shape=jax.ShapeDtypeStruct((DEPTH, rows, N_MOD * D_MODEL), F32),
        compiler_params=_cparams("parallel", "parallel"),
        name="ada_mod",
    )(c_all, ada_w, ada_b.reshape(DEPTH, 1, N_MOD * D_MODEL))


def _fold_kernel(w_ref, m_ref, o_ref):
    o_ref[0] = jnp.dot(w_ref[0], m_ref[...], preferred_element_type=F32, precision=HIGHEST).astype(BF16)


def _fold_channel_dft(w_a):
    n = np.arange(A_GROUP_W)
    idx = (n[:, None] * n[None, :]) % A_GROUP_W
    ang = 2.0 * np.pi * idx / A_GROUP_W
    eye = np.eye(A_GROUPS)
    m = np.concatenate([np.kron(eye, np.cos(ang)), np.kron(eye, np.sin(ang))], axis=1).astype(np.float32)
    return pl.pallas_call(
        _fold_kernel,
        grid=(DEPTH,),
        in_specs=[
            pl.BlockSpec((1, D_MODEL, A_W), lambda l: (l, 0, 0)),
            pl.BlockSpec((A_W, 2 * A_W), lambda l: (0, 0)),
        ],
        out_specs=pl.BlockSpec((1, D_MODEL, 2 * A_W), lambda l: (l, 0, 0)),
        out_shape=jax.ShapeDtypeStruct((DEPTH, D_MODEL, 2 * A_W), BF16),
        compiler_params=_cparams("parallel"),
        name="fold_channel_dft",
    )(w_a, jnp.asarray(m))


def _modulated_norm(x, g, shift, scale):
    ms = jnp.mean(x * x, axis=-1, keepdims=True)
    return (x * lax.rsqrt(ms + EPS) * g) * (1.0 + scale) + shift


def _inproj_kernel(x_ref, mod_ref, g1_ref, cos_ref, sin_ref,
                   wfft_ref, wb_ref, wc_ref, wdq_ref, wdkv_ref, wkr_ref,
                   qg_ref, kvg_ref, wq2_ref, wkv2_ref, lng_ref, lnb_ref, ones_ref,
                   y1_ref, y2_ref, b0_ref, b1_ref, b2_ref, cu_ref, cvn_ref, mq_ref, mk_ref, mv_ref, res_sc, *, tm):
    x = x_ref[0]
    h = _modulated_norm(x, g1_ref[...], mod_ref[0, 0:1, :], mod_ref[0, 1:2, :])
    hb = h.astype(BF16)

    y = _dot(hb, wfft_ref[...])
    y1_ref[0] = y[:, :A_W].astype(BF16)
    y2_ref[0] = y[:, A_W:].astype(BF16)

    zb = _dot(hb, wb_ref[...])
    b0_ref[0] = zb[:, :B_QKV_W].astype(BF16)
    nres = 2 * B_QKV_W // LANE
    for c in range(nres):
        res_sc[c] = zb[:, B_QKV_W + c * LANE:B_QKV_W + (c + 1) * LANE]
    for g, out_ref in ((1, b1_ref), (2, b2_ref)):
        dil = B_PAIRS[g][1]
        for r in range(dil):
            for c in range(3):
                out_ref[0, r, :, c * LANE:(c + 1) * LANE] = (
                    res_sc.at[3 * (g - 1) + c][pl.ds(r, tm // dil, stride=dil), :].astype(BF16))

    zc = _dot(hb, wc_ref[...])
    cu_ref[0] = zc[:, :C_W].astype(BF16)
    cv = zc[:, C_W:]
    mu = jnp.mean(cv, axis=-1, keepdims=True)
    cc = cv - mu
    var = jnp.mean(cc * cc, axis=-1, keepdims=True)
    cvn_ref[0] = (cc * lax.rsqrt(var + EPS) * lng_ref[...] + lnb_ref[...]).astype(BF16)

    cos = cos_ref[...]
    sin = sin_ref[...]
    cos4 = jnp.concatenate([cos] * D_HEADS, axis=1)
    sin4 = jnp.concatenate([sin] * D_HEADS, axis=1)

    dcq = _dot(hb, wdq_ref[...])
    msq = jnp.mean(dcq * dcq, axis=-1, keepdims=True)
    qn = (dcq * lax.rsqrt(msq + EPS) * qg_ref[...]).astype(BF16)
    qq = _dot(qn, wq2_ref[...])
    q = (qq[:, :D_PAD_W] * cos4 + qq[:, D_PAD_W:] * sin4) * ((D_NOPE + D_ROPE) ** -0.5)
    mq_ref[0] = q.astype(BF16)

    dckv = _dot(hb, wdkv_ref[...])
    mskv = jnp.sum(dckv * dckv, axis=-1, keepdims=True) * (1.0 / D_KV_LORA)
    kvn = (dckv * lax.rsqrt(mskv + EPS) * kvg_ref[...]).astype(BF16)
    kv = _dot(kvn, wkv2_ref[...])
    kr = _dot(hb, wkr_ref[...])
    kpe = kr[:, :D_HEAD_PAD] * cos + kr[:, D_HEAD_PAD:] * sin
    mk_ref[0] = (kv[:, :D_PAD_W] + jnp.concatenate([kpe] * D_HEADS, axis=1)).astype(BF16)
    mv_ref[0] = (kv[:, D_PAD_W:] + ones_ref[...]).astype(BF16)


def _inproj(x, mod, lw, cos_t, sin_t, ones_row):
    B, S, _ = x.shape
    tm = ROW_TILE
    row = lambda w: pl.BlockSpec((1, tm, w), lambda b, i: (b, i, 0))
    tab = pl.BlockSpec((tm, D_HEAD_PAD), lambda b, i: (i, 0))
    weights = [lw["wfft"], lw["wb"], lw["wc"], lw["wdq"], lw["wdkv"], lw["wkr"],
               lw["qg"], lw["kvg"], lw["wq2"], lw["wkv2"], lw["lng"], lw["lnb"], ones_row]
    def nat(w):
        return row(w), jax.ShapeDtypeStruct((B, S, w), BF16)

    def residue(dil):
        return (pl.BlockSpec((1, dil, tm // dil, B_QKV_W), lambda b, i: (b, 0, i, 0)),
                jax.ShapeDtypeStruct((B, dil, S // dil, B_QKV_W), BF16))

    outs = [nat(A_W), nat(A_W), nat(B_QKV_W), residue(B_PAIRS[1][1]), residue(B_PAIRS[2][1]),
            nat(C_W), nat(C_W), nat(D_PAD_W), nat(D_PAD_W), nat(D_PAD_W)]
    return pl.pallas_call(
        functools.partial(_inproj_kernel, tm=tm),
        grid=(B, S // tm),
        in_specs=[row(D_MODEL),
                  pl.BlockSpec((1, N_MOD, D_MODEL), lambda b, i: (b, 0, 0)),
                  _resident(lw["g1"].shape), tab, tab] + [_resident(w.shape) for w in weights],
        out_specs=[o[0] for o in outs],
        out_shape=[o[1] for o in outs],
        scratch_shapes=[pltpu.VMEM((2 * B_QKV_W // LANE, tm, LANE), F32)],
        compiler_params=_cparams("parallel", "parallel"),
        name="inproj",
    )(x, mod, lw["g1"], cos_t, sin_t, *weights)


def _dft_direct_kernel(m_ref, y1_ref, y2_ref, o_ref, *, seq, tr):
    r0 = pl.multiple_of(pl.program_id(1) * tr, tr)
    mc = m_ref[pl.ds(r0, tr), :seq]
    ms = m_ref[pl.ds(r0, tr), seq:]
    o_ref[0] = (_dot(mc, y1_ref[0]) + _dot(ms, y2_ref[0])).astype(BF16)


def _dft_direct(m, y1, y2):
    B, S, _ = y1.shape
    tr = ROW_TILE
    yspec = pl.BlockSpec((1, S, A_W), lambda b, i: (b, 0, 0))
    return pl.pallas_call(
        functools.partial(_dft_direct_kernel, seq=S, tr=tr),
        grid=(B, S // tr),
        in_specs=[_resident(m.shape), yspec, yspec],
        out_specs=pl.BlockSpec((1, tr, A_W), lambda b, i: (b, i, 0)),
        out_shape=jax.ShapeDtypeStruct((B, S, A_W), BF16),
        compiler_params=_cparams("parallel", "arbitrary"),
        name="dft_direct",
    )(m, y1, y2)


def _fft_stage1_kernel(m_ref, y1_ref, y2_ref, ar_ref, ai_ref, in_sc, out_sc, *, n1):
    nl = A_W // LANE
    for c in range(nl):
        sl = slice(c * LANE, (c + 1) * LANE)
        in_sc[c] = y1_ref[0, :, :, sl].reshape(n1 * DFT_SUB, LANE).astype(F32)
        in_sc[nl + c] = y2_ref[0, :, :, sl].reshape(n1 * DFT_SUB, LANE).astype(F32)

    def body(j, carry):
        rows = pl.ds(j, n1, stride=DFT_SUB)
        top = jnp.concatenate([in_sc.at[c][rows, :] for c in range(nl)], axis=1)
        bot = jnp.concatenate([in_sc.at[nl + c][rows, :] for c in range(nl)], axis=1)
        out = _dot(m_ref[...], jnp.concatenate([top, bot], axis=0).astype(BF16))
        for c in range(nl):
            out_sc.at[c][rows, :] = out[:n1, c * LANE:(c + 1) * LANE]
            out_sc.at[nl + c][rows, :] = out[n1:, c * LANE:(c + 1) * LANE]
        return carry

    lax.fori_loop(0, DFT_SUB, body, 0)
    for c in range(nl):
        sl = slice(c * LANE, (c + 1) * LANE)
        ar_ref[0, :, :, sl] = out_sc[c].reshape(n1, DFT_SUB, LANE).astype(BF16)
        ai_ref[0, :, :, sl] = out_sc[nl + c].reshape(n1, DFT_SUB, LANE).astype(BF16)


def _fft_stage2_kernel(h_ref, ar_ref, ai_ref, o_ref, out_sc, *, n2):
    nl = A_W // LANE

    def body(j, carry):
        rhs = jnp.concatenate([ar_ref[0, j], ai_ref[0, j]], axis=0)
        out = _dot(h_ref[j], rhs)
        for c in range(nl):
            out_sc.at[c][pl.ds(j, n2, stride=DFT_SUB), :] = out[:, c * LANE:(c + 1) * LANE]
        return carry

    lax.fori_loop(0, DFT_SUB, body, 0)
    for c in range(nl):
        o_ref[0, :, :, c * LANE:(c + 1) * LANE] = out_sc[c].reshape(n2, DFT_SUB, LANE).astype(BF16)


def _fft_two_stage(m1, h2, y1, y2):
    B, S, _ = y1.shape
    n2 = DFT_N2
    n1 = S // n2
    nl = A_W // LANE
    blk = pl.BlockSpec((1, n1, DFT_SUB, A_W), lambda b, j: (b, 0, j, 0))
    ar, ai = pl.pallas_call(
        functools.partial(_fft_stage1_kernel, n1=n1),
        grid=(B, n2 // DFT_SUB),
        in_specs=[_resident(m1.shape), blk, blk],
        out_specs=[blk, blk],
        out_shape=[jax.ShapeDtypeStruct((B, n1, n2, A_W), BF16)] * 2,
        scratch_shapes=[pltpu.VMEM((2 * nl, n1 * DFT_SUB, LANE), F32)] * 2,
        compiler_params=_cparams("parallel", "parallel"),
        name="fft_stage1",
    )(m1, y1.reshape(B, n1, n2, A_W), y2.reshape(B, n1, n2, A_W))
    slab = pl.BlockSpec((1, DFT_SUB, n2, A_W), lambda b, j: (b, j, 0, 0))
    out = pl.pallas_call(
        functools.partial(_fft_stage2_kernel, n2=n2),
        grid=(B, n1 // DFT_SUB),
        in_specs=[pl.BlockSpec((DFT_SUB, n2, 2 * n2), lambda b, j: (j, 0, 0)), slab, slab],
        out_specs=pl.BlockSpec((1, n2, DFT_SUB, A_W), lambda b, j: (b, 0, j, 0)),
        out_shape=jax.ShapeDtypeStruct((B, n2, n1, A_W), BF16),
        scratch_shapes=[pltpu.VMEM((nl, n2 * DFT_SUB, LANE), F32)],
        compiler_params=_cparams("parallel", "parallel"),
        name="fft_stage2",
    )(h2, ar, ai)
    return out.reshape(B, S, A_W)


def _dft_tables(S):
    scale = 1.0 / math.sqrt(S * A_GROUP_W)

    def cs(idx, period):
        ang = idx.astype(F32) * (2.0 * math.pi / period)
        return jnp.cos(ang), jnp.sin(ang)

    if S <= DFT_DIRECT_MAX:
        n = jnp.arange(S, dtype=jnp.int32)
        c, s = cs((n[:, None] * n[None, :]) % S, S)
        return {"direct": jnp.concatenate([c * scale, -s * scale], axis=1).astype(BF16)}
    n2 = DFT_N2
    n1 = S // n2
    a = jnp.arange(n1, dtype=jnp.int32)
    c1, s1 = cs((a[:, None] * a[None, :]) % n1, n1)
    m1 = jnp.concatenate([jnp.concatenate([c1, -s1], axis=1), jnp.concatenate([s1, c1], axis=1)], axis=0)
    k1 = a[:, None, None]
    k2 = jnp.arange(n2, dtype=jnp.int32)[None, :, None]
    nn = jnp.arange(n2, dtype=jnp.int32)[None, None, :]
    hc, hs = cs((nn * (k1 + n1 * k2)) % S, S)
    h2 = jnp.concatenate([hc * scale, -hs * scale], axis=2)
    return {"m1": m1.astype(BF16), "h2": h2.astype(BF16)}


def _fourier_seq(tables, y1, y2):
    if "direct" in tables:
        return _dft_direct(tables["direct"], y1, y2)
    return _fft_two_stage(tables["m1"], tables["h2"], y1, y2)


def _band_kernel(bias_ref, q_ref, k_ref, v_ref, o_ref, lse_ref, *, seq, rows, nseq, kwin):
    nsub = rows // B_QBLK
    last_blk = seq // B_QBLK - 1
    j = pl.program_id(1)
    head0 = lax.broadcasted_iota(jnp.int32, (B_QBLK, B_GROUP_W), 1) < B_HEAD_DIM

    def body(t, carry):
        si = t // nsub
        b = t % nsub
        r0 = pl.multiple_of(b * B_QBLK, B_QBLK)
        gb = j * nsub + b
        ws = pl.multiple_of(jnp.clip(gb * B_QBLK - B_HALF, 0, seq - kwin), B_HALF)
        tix = jnp.where(gb == 0, 0, jnp.where(gb == last_blk, 2, 1))
        q = q_ref[si, pl.ds(r0, B_QBLK), :]
        kw = k_ref[si, pl.ds(ws, kwin), :]
        vw = v_ref[si, pl.ds(ws, kwin), :]
        zero = jnp.zeros_like(q)
        outs, lses = [], []
        for h in range(2):
            qh = jnp.where(head0 if h == 0 else jnp.logical_not(head0), q, zero)
            s = _dot_nt(qh, kw) + bias_ref[tix, h]
            m = jnp.max(s, axis=1, keepdims=True)
            p = jnp.exp(s - m)
            l = jnp.sum(p, axis=1, keepdims=True)
            outs.append(_dot(p.astype(BF16), vw) * (1.0 / l))
            lses.append(jnp.broadcast_to(m + jnp.log(l), (B_QBLK, B_GROUP_W)))
        o_ref[si, pl.ds(r0, B_QBLK), :] = jnp.where(head0, outs[0], outs[1]).astype(BF16)
        lse_ref[si, pl.ds(r0, B_QBLK), :] = jnp.where(head0, lses[0], lses[1])
        return carry

    lax.fori_loop(0, nseq * nsub, body, 0, unroll=BAND_UNROLL)


def _band_attention(bias, arr, cols):
    N, L, _ = arr.shape
    kwin = bias.shape[-1]
    if L >= BAND_ROWS:
        nseq, rows = 1, BAND_ROWS
    else:
        nseq, rows = BAND_ROWS // L, L
    cq, ck, cv = cols
    out_spec = pl.BlockSpec((nseq, rows, B_GROUP_W), lambda n, j: (n, j, 0))
    return pl.pallas_call(
        functools.partial(_band_kernel, seq=L, rows=rows, nseq=nseq, kwin=kwin),
        grid=(N // nseq, L // rows),
        in_specs=[
            _resident(bias.shape),
            pl.BlockSpec((nseq, rows, B_GROUP_W), lambda n, j: (n, j, cq)),
            pl.BlockSpec((nseq, L, B_GROUP_W), lambda n, j: (n, 0, ck)),
            pl.BlockSpec((nseq, L, B_GROUP_W), lambda n, j: (n, 0, cv)),
        ],
        out_specs=[out_spec, out_spec],
        out_shape=[jax.ShapeDtypeStruct((N, L, B_GROUP_W), BF16), jax.ShapeDtypeStruct((N, L, B_GROUP_W), F32)],
        compiler_params=_cparams("parallel", "arbitrary"),
        name="band_attention",
    )(bias, arr, arr, arr)


def _t5_bucket(rel):
    half = REL_BUCKETS // 2
    max_exact = half // 2
    ret = jnp.where(rel > 0, half, 0)
    n = jnp.abs(rel)
    large = max_exact + (jnp.log(jnp.maximum(n, max_exact).astype(F32) / max_exact)
                         / math.log(REL_MAX_DIST / max_exact) * (half - max_exact)).astype(jnp.int32)
    large = jnp.minimum(large, half - 1)
    return ret + jnp.where(n < max_exact, n, large)


def _band_bias_tiles(rel_bias_g, dil, L):
    kwin = min(2 * B_QBLK, L)
    i = np.arange(B_QBLK)[:, None]
    jj = np.arange(kwin)[None, :]
    tiles = []
    for off in (0, -B_HALF, B_QBLK - kwin):
        rel = jj + off - i
        band = np.abs(rel) <= B_HALF
        bucket = _t5_bucket(jnp.asarray(rel * dil, dtype=jnp.int32))
        onehot = (bucket[..., None] == jnp.arange(REL_BUCKETS, dtype=jnp.int32)).astype(F32)
        vals = jnp.einsum("ijb,bh->hij", onehot, rel_bias_g.astype(F32), precision=HIGHEST)
        tiles.append(jnp.where(jnp.asarray(band)[None], vals, NEG_INF))
    return jnp.stack(tiles)


def _dilated_branch(bias_tiles, b0, b1, b2):
    outs = []
    for g, arr in enumerate((b0, b1, b2)):
        lead = arr.shape[:-2]
        L = arr.shape[-2]
        o, lse = _band_attention(bias_tiles[g], arr.reshape(-1, L, B_QKV_W), (0, 1, 2))
        outs.append((o.reshape(*lead, L, B_GROUP_W), lse.reshape(*lead, L, B_GROUP_W)))
    return outs


def _mla_kernel(q_ref, k_ref, v_ref, o_ref, m_sc, acc_sc, *, tk):
    kv = pl.program_id(2)

    @pl.when(kv == 0)
    def _():
        m_sc[...] = jnp.full(m_sc.shape, -jnp.inf, F32)
        acc_sc[...] = jnp.zeros(acc_sc.shape, F32)

    for h in range(D_HEADS):
        sl = slice(h * D_HEAD_PAD, (h + 1) * D_HEAD_PAD)
        s = _dot_nt(q_ref[0, :, sl], k_ref[0, :, sl])
        m_prev = m_sc[h]
        m_new = jnp.maximum(m_prev, jnp.max(s, axis=1, keepdims=True))
        alpha = jnp.exp(m_prev - m_new)
        p = jnp.exp(s - jnp.concatenate([m_new] * (tk // LANE), axis=1))
        acc_sc[h] = alpha * acc_sc[h] + _dot(p.astype(BF16), v_ref[0, :, sl])
        m_sc[h] = m_new

    @pl.when(kv == pl.num_programs(2) - 1)
    def _():
        for h in range(D_HEADS):
            acc = acc_sc[h]
            o_ref[0, :, h * D_HEAD_PAD:(h + 1) * D_HEAD_PAD] = (acc * (1.0 / acc[:, D_V:D_V + 1])).astype(BF16)


def _mla_attention(q, k, v):
    B, S, _ = q.shape
    tq = tk = min(ATT_TILE, S)
    return pl.pallas_call(
        functools.partial(_mla_kernel, tk=tk),
        grid=(B, S // tq, S // tk),
        in_specs=[
            pl.BlockSpec((1, tq, D_PAD_W), lambda b, i, j: (b, i, 0)),
            pl.BlockSpec((1, tk, D_PAD_W), lambda b, i, j: (b, j, 0)),
            pl.BlockSpec((1, tk, D_PAD_W), lambda b, i, j: (b, j, 0)),
        ],
        out_specs=pl.BlockSpec((1, tq, D_PAD_W), lambda b, i, j: (b, i, 0)),
        out_shape=jax.ShapeDtypeStruct((B, S, D_PAD_W), BF16),
        scratch_shapes=[pltpu.VMEM((D_HEADS, tq, LANE), F32), pltpu.VMEM((D_HEADS, tq, D_HEAD_PAD), F32)],
        compiler_params=_cparams("parallel", "parallel", "arbitrary"),
        name="mla_attention",
    )(q, k, v)


def _merge_kernel(x_ref, mod_ref, g1_ref, ya_ref, ob0_ref, ob1_ref, ob2_ref, l0_ref, l1_ref, l2_ref,
                  cu_ref, cvn_ref, yd_ref, wg_ref, pa_ref, pb_ref, pc_ref, pd_ref, wo_ref,
                  ws_ref, bs_ref, o_ref, nat_sc, *, tm):
    x = x_ref[0]
    h = _modulated_norm(x, g1_ref[...], mod_ref[0, 0:1, :], mod_ref[0, 1:2, :])
    hb = h.astype(BF16)

    def gate(i):
        return jax.nn.sigmoid(_dot(hb, wg_ref[:, i * D_MODEL:(i + 1) * D_MODEL]))

    merged = gate(0) * _dot(ya_ref[0], pa_ref[...])

    for k, (o_ref_g, l_ref_g, dil) in enumerate(((ob1_ref, l1_ref, B_PAIRS[1][1]), (ob2_ref, l2_ref, B_PAIRS[2][1]))):
        for r in range(dil):
            rows = pl.ds(r, tm // dil, stride=dil)
            nat_sc.at[2 * k][rows, :] = o_ref_g[0, r].astype(F32)
            nat_sc.at[2 * k + 1][rows, :] = l_ref_g[0, r]
    l0, l1, l2 = l0_ref[0], nat_sc[1], nat_sc[3]
    mx = jnp.maximum(jnp.maximum(l0, l1), l2)
    e0, e1, e2 = jnp.exp(l0 - mx), jnp.exp(l1 - mx), jnp.exp(l2 - mx)
    yb = (e0 * ob0_ref[0].astype(F32) + e1 * nat_sc[0] + e2 * nat_sc[2]) / (e0 + e1 + e2)
    merged = merged + gate(1) * _dot(yb.astype(BF16), pb_ref[...])

    head = lax.broadcasted_iota(jnp.int32, (C_CHUNK, C_W), 1) // C_HEAD_W
    chunks = []
    for c in range(tm // C_CHUNK):
        rs = slice(c * C_CHUNK, (c + 1) * C_CHUNK)
        mm = _dot(ws_ref[...], cvn_ref[0, rs, :])
        mixed = bs_ref[...]
        for hh in range(C_HEADS):
            mixed = mixed + jnp.where(head == hh, mm[hh * C_CHUNK:(hh + 1) * C_CHUNK], 0.0)
        chunks.append((cu_ref[0, rs, :].astype(F32) * mixed).astype(BF16))
    yc = jnp.concatenate(chunks, axis=0)
    merged = merged + gate(2) * _dot(yc, pc_ref[...])

    merged = merged + gate(3) * _dot(yd_ref[0], pd_ref[...])
    o_ref[0] = x + mod_ref[0, 2:3, :] * _dot(merged.astype(BF16), wo_ref[...])


def _merge(x, mod, lw, ya, band, cu, cvn, yd):
    B, S, _ = x.shape
    tm = ROW_TILE
    row = lambda w: pl.BlockSpec((1, tm, w), lambda b, i: (b, i, 0))
    weights = [lw["wg"], lw["pa"], lw["pb"], lw["pc"], lw["pd"], lw["wo"], lw["ws"], lw["bs"]]
    (ob0, l0), (ob1, l1), (ob2, l2) = band
    res = lambda dil: pl.BlockSpec((1, dil, tm // dil, B_GROUP_W), lambda b, i: (b, 0, i, 0))
    r1, r2 = res(B_PAIRS[1][1]), res(B_PAIRS[2][1])
    return pl.pallas_call(
        functools.partial(_merge_kernel, tm=tm),
        grid=(B, S // tm),
        in_specs=[row(D_MODEL), pl.BlockSpec((1, N_MOD, D_MODEL), lambda b, i: (b, 0, 0)), _resident(lw["g1"].shape),
                  row(A_W), row(B_GROUP_W), r1, r2, row(B_GROUP_W), r1, r2, row(C_W), row(C_W), row(D_PAD_W)]
                 + [_resident(w.shape) for w in weights],
        out_specs=row(D_MODEL),
        out_shape=jax.ShapeDtypeStruct((B, S, D_MODEL), F32),
        scratch_shapes=[pltpu.VMEM((4, tm, B_GROUP_W), F32)],
        compiler_params=_cparams("parallel", "parallel"),
        name="merge",
    )(x, mod, lw["g1"], ya, ob0, ob1, ob2, l0, l1, l2, cu, cvn, yd, *weights)


def _mlp_kernel(x_ref, mod_ref, g2_ref, w1_ref, w2_ref, gf_ref, o_ref, *, final):
    x = x_ref[0]
    h = _modulated_norm(x, g2_ref[...], mod_ref[0, 3:4, :], mod_ref[0, 4:5, :])
    a = jnp.maximum(_dot(h.astype(BF16), w1_ref[...]), 0.0)
    y = x + mod_ref[0, 5:6, :] * _dot((a * a).astype(BF16), w2_ref[...])
    if final:
        ms = jnp.mean(y * y, axis=-1, keepdims=True)
        y = y * lax.rsqrt(ms + EPS) * gf_ref[...]
    o_ref[0] = y


def _mlp(x, mod, lw, final_g, final):
    B, S, _ = x.shape
    tm = ROW_TILE
    row = pl.BlockSpec((1, tm, D_MODEL), lambda b, i: (b, i, 0))
    return pl.pallas_call(
        functools.partial(_mlp_kernel, final=final),
        grid=(B, S // tm),
        in_specs=[row, pl.BlockSpec((1, N_MOD, D_MODEL), lambda b, i: (b, 0, 0)), _resident(lw["g2"].shape),
                  _resident(lw["w1"].shape), _resident(lw["w2"].shape), _resident(final_g.shape)],
        out_specs=row,
        out_shape=jax.ShapeDtypeStruct((B, S, D_MODEL), F32),
        compiler_params=_cparams("parallel", "parallel"),
        name="mlp",
    )(x, mod, lw["g2"], lw["w1"], lw["w2"], final_g)


def _head_slots(w, per_head, take):
    k = w.shape[0]
    wh = w.reshape(k, D_HEADS, per_head)[:, :, take]
    pad = D_HEAD_PAD - wh.shape[-1]
    return jnp.pad(wh, ((0, 0), (0, 0), (0, pad))).reshape(k, D_PAD_W)


def _rope_rot_cols(w):
    half = D_ROPE // 2
    return jnp.concatenate([-w[..., half:], w[..., :half]], axis=-1)


def _prepare_layer(l, w_in, wfft, norm1_g, q_norm_g, kv_norm_g, w_uq, w_ukv, sgu_ln_g, sgu_ln_b, sgu_w, sgu_b,
                   p_a, p_b, p_c, p_d, w_o, norm2_g, w1, w2):
    wi = w_in[l]
    o = A_W
    wb = wi[:, o:o + 3 * B_QKV_W]; o += 3 * B_QKV_W
    qkv_scale = jnp.asarray([B_HEAD_DIM ** -0.5, 1.0, 1.0], F32)[None, None, :, None]
    wb = (wb.reshape(D_MODEL, 3, len(B_PAIRS), B_GROUP_W).transpose(0, 2, 1, 3) * qkv_scale).reshape(D_MODEL, 3 * B_QKV_W)
    wc = wi[:, o:o + 2 * C_W]; o += 2 * C_W
    wdq = wi[:, o:o + D_Q_LORA]; o += D_Q_LORA
    wdkv = wi[:, o:o + D_KV_LORA]; o += D_KV_LORA
    wkr = wi[:, o:o + D_ROPE]; o += D_ROPE
    wg = wi[:, o:]

    def rope_slot(w):
        return jnp.pad(w, ((0, 0), (D_NOPE, D_HEAD_PAD - D_NOPE - D_ROPE)))

    wkr2 = jnp.concatenate([rope_slot(wkr), rope_slot(_rope_rot_cols(wkr))], axis=1)

    uq = w_uq[l].reshape(D_Q_LORA, D_HEADS, D_NOPE + D_ROPE)
    uq_rot = jnp.concatenate([jnp.zeros_like(uq[..., :D_NOPE]), _rope_rot_cols(uq[..., D_NOPE:])], axis=-1)
    padq = ((0, 0), (0, 0), (0, D_HEAD_PAD - D_NOPE - D_ROPE))
    wq2 = jnp.concatenate([jnp.pad(uq, padq).reshape(D_Q_LORA, D_PAD_W),
                           jnp.pad(uq_rot, padq).reshape(D_Q_LORA, D_PAD_W)], axis=1)

    ukv = jnp.pad(w_ukv[l], ((0, D_KV_PAD - D_KV_LORA), (0, 0)))
    wkv2 = jnp.concatenate([_head_slots(ukv, D_NOPE + D_V, slice(0, D_NOPE)),
                            _head_slots(ukv, D_NOPE + D_V, slice(D_NOPE, D_NOPE + D_V))], axis=1)

    pd_pad = jnp.pad(p_d[l].reshape(D_HEADS, D_V, D_MODEL), ((0, 0), (0, D_HEAD_PAD - D_V), (0, 0)))
    bias_sgu = jnp.repeat(sgu_b[l].T, C_HEAD_W, axis=1)
    row = lambda v: v.reshape(1, -1).astype(F32)
    return {
        "g1": row(norm1_g[l]), "g2": row(norm2_g[l]),
        "wfft": wfft[l], "wb": wb.astype(BF16), "wc": wc.astype(BF16), "wdq": wdq.astype(BF16),
        "wdkv": jnp.pad(wdkv, ((0, 0), (0, D_KV_PAD - D_KV_LORA))).astype(BF16),
        "wkr": wkr2.astype(BF16), "wg": wg.astype(BF16),
        "qg": row(q_norm_g[l]), "kvg": row(jnp.pad(kv_norm_g[l], (0, D_KV_PAD - D_KV_LORA))),
        "wq2": wq2.astype(BF16), "wkv2": wkv2.astype(BF16),
        "lng": row(sgu_ln_g[l]), "lnb": row(sgu_ln_b[l]),
        "ws": sgu_w[l].reshape(C_HEADS * C_CHUNK, C_CHUNK).astype(BF16), "bs": bias_sgu.astype(F32),
        "pa": p_a[l].astype(BF16), "pb": p_b[l].astype(BF16), "pc": p_c[l].astype(BF16),
        "pd": pd_pad.reshape(D_PAD_W, D_MODEL).astype(BF16), "wo": w_o[l].astype(BF16),
        "w1": w1[l].astype(BF16), "w2": w2[l].astype(BF16),
    }


def _rope_tables(S):
    half = D_ROPE // 2
    inv = ROPE_THETA ** (-jnp.arange(half, dtype=F32) / half)
    ang = jnp.arange(S, dtype=F32)[:, None] * inv[None, :]
    cos = jnp.cos(ang)
    sin = jnp.sin(ang)
    pad = jnp.zeros((S, D_HEAD_PAD - D_NOPE - D_ROPE), F32)
    cos_t = jnp.concatenate([jnp.ones((S, D_NOPE), F32), cos, cos, pad], axis=1)
    sin_t = jnp.concatenate([jnp.zeros((S, D_NOPE), F32), sin, sin, pad], axis=1)
    return cos_t, sin_t


def _run_group(x, mods, layers, bias_tiles_fn, final_g):
    B, S, _ = x.shape
    cos_t, sin_t = _rope_tables(S)
    tables = _dft_tables(S)
    bias_tiles = bias_tiles_fn(S)
    ones_np = np.zeros((D_HEADS, D_HEAD_PAD), np.float32)
    ones_np[:, D_V] = 1.0
    ones_row = jnp.asarray(ones_np.reshape(1, D_PAD_W))
    gf = final_g.reshape(1, D_MODEL)
    for l, lw in enumerate(layers):
        mod = mods[l]
        y1, y2, b0, b1, b2, cu, cvn, mq, mk, mv = _inproj(x, mod, lw, cos_t, sin_t, ones_row)
        ya = _fourier_seq(tables, y1, y2)
        band = _dilated_branch(bias_tiles, b0, b1, b2)
        yd = _mla_attention(mq, mk, mv)
        x = _merge(x, mod, lw, ya, band, cu, cvn, yd)
        x = _mlp(x, mod, lw, gf, final=(l == len(layers) - 1))
    return x


def kernel(x_prompt, x_sample, c_prompt, c_sample, rel_bias, ada_w, ada_b, norm1_g, w_in, mla_q_norm_g, mla_kv_norm_g, mla_w_uq, mla_w_ukv, sgu_ln_g, sgu_ln_b, sgu_w, sgu_b, p_a, p_b, p_c, p_d, w_o, norm2_g, mlp_w1, mlp_w2, final_g):
    nb_p = c_prompt.shape[0]
    nb_s = c_sample.shape[0]
    rows = -(-(nb_p + nb_s) // 8) * 8
    c_all = jnp.concatenate([c_prompt, c_sample, jnp.zeros((rows - nb_p - nb_s, D_MODEL), F32)], axis=0)
    mod_all = _ada_mod(c_all, ada_w, ada_b).reshape(DEPTH, rows, N_MOD, D_MODEL)
    mods_p = mod_all[:, :nb_p]
    mods_s = mod_all[:, nb_p:nb_p + nb_s]

    wfft = _fold_channel_dft(w_in[:, :, :A_W])
    layers = [_prepare_layer(l, w_in, wfft, norm1_g, mla_q_norm_g, mla_kv_norm_g, mla_w_uq, mla_w_ukv,
                             sgu_ln_g, sgu_ln_b, sgu_w, sgu_b, p_a, p_b, p_c, p_d, w_o, norm2_g, mlp_w1, mlp_w2)
              for l in range(DEPTH)]

    tile_cache = {}

    def bias_tiles_fn(S):
        out = []
        for g, (_, dil) in enumerate(B_PAIRS):
            key = (g, min(2 * B_QBLK, S // dil))
            if key not in tile_cache:
                tile_cache[key] = _band_bias_tiles(rel_bias[:, 2 * g:2 * g + 2], dil, S // dil)
            out.append(tile_cache[key])
        return out

    y_prompt = _run_group(x_prompt, mods_p, layers, bias_tiles_fn, final_g)
    y_sample = _run_group(x_sample, mods_s, layers, bias_tiles_fn, final_g)
    return (y_prompt, y_sample)
```

```python
import functools
import math

import jax
import jax.numpy as jnp
import numpy as np
from jax import lax
from jax.experimental import pallas as pl
from jax.experimental.pallas import tpu as pltpu

F32 = jnp.float32
BF16 = jnp.bfloat16
HIGHEST = lax.Precision.HIGHEST

D_MODEL = 1024
DEPTH = 4
EPS = 1e-6
NEG_INF = -1e30
N_MOD = 6
A_GROUPS = 4
A_GROUP_W = 192
A_W = A_GROUPS * A_GROUP_W
B_PAIRS = ((128, 1), (512, 4), (2048, 16))
B_HEAD_DIM = 64
B_QKV_W = 384
B_GROUP_W = 128
B_HALF = 64
B_QBLK = 128
REL_BUCKETS = 32
REL_MAX_DIST = 1024
C_HEADS = 4
C_HEAD_W = 96
C_W = 384
C_CHUNK = 128
D_HEADS = 4
D_Q_LORA = 384
D_KV_LORA = 320
D_KV_PAD = 384
D_NOPE = 64
D_ROPE = 32
D_V = 64
D_HEAD_PAD = 128
D_PAD_W = D_HEADS * D_HEAD_PAD
ROPE_THETA = 10000.0
D_FF = 4 * D_MODEL
MIX_W = A_W + 3 * B_QKV_W + 2 * C_W + D_Q_LORA + D_KV_LORA + D_ROPE

VMEM_LIMIT_BYTES = 56 * 1024 * 1024
LANE = 128
ROW_TILE = 512
DFT_N2 = 128
DFT_KRON_COLS = 64
DFT_SUB = 16
ATT_TQ = 2048
ATT_TK = 1024
BAND_ROWS = 4096
BAND_UNROLL = 16


def _cparams(*sem):
    return pltpu.CompilerParams(dimension_semantics=sem, vmem_limit_bytes=VMEM_LIMIT_BYTES)


def _resident(shape):
    nd = len(shape)
    return pl.BlockSpec(shape, lambda *_: (0,) * nd, pipeline_mode=pl.Buffered(1))


def _dot(a, b):
    return jnp.dot(a, b, preferred_element_type=F32)


def _dot_nt(a, b):
    return lax.dot_general(a, b, (((1,), (1,)), ((), ())), preferred_element_type=F32)


def _ada_kernel(c_ref, w_ref, b_ref, o_ref):
    c = c_ref[...]
    a = c * jax.nn.sigmoid(c)
    o_ref[0] = jnp.dot(a, w_ref[0], preferred_element_type=F32, precision=HIGHEST) + b_ref[0]


def _ada_mod(c_all, ada_w, ada_b):
    rows = c_all.shape[0]
    tn = 1536
    return pl.pallas_call(
        _ada_kernel,
        grid=(DEPTH, N_MOD * D_MODEL // tn),
        in_specs=[
            pl.BlockSpec((rows, D_MODEL), lambda l, j: (0, 0)),
            pl.BlockSpec((1, D_MODEL, tn), lambda l, j: (l, 0, j)),
            pl.BlockSpec((1, 1, tn), lambda l, j: (l, 0, j)),
        ],
        out_specs=pl.BlockSpec((1, rows, tn), lambda l, j: (l, 0, j)),
        out_shape=jax.ShapeDtypeStruct((DEPTH, rows, N_MOD * D_MODEL), F32),
        compiler_params=_cparams("parallel", "parallel"),
        name="ada_mod",
    )(c_all, ada_w, ada_b.reshape(DEPTH, 1, N_MOD * D_MODEL))


def _fold_kernel(w_ref, m_ref, o_ref):
    o_ref[0] = jnp.dot(w_ref[0], m_ref[...], preferred_element_type=F32, precision=HIGHEST).astype(BF16)


def _fold_channel_dft(w_a):
    n = np.arange(A_GROUP_W)
    idx = (n[:, None] * n[None, :]) % A_GROUP_W
    ang = 2.0 * np.pi * idx / A_GROUP_W
    eye = np.eye(A_GROUPS)
    m = np.concatenate([np.kron(eye, np.cos(ang)), np.kron(eye, np.sin(ang))], axis=1).astype(np.float32)
    return pl.pallas_call(
        _fold_kernel,
        grid=(DEPTH,),
        in_specs=[
            pl.BlockSpec((1, D_MODEL, A_W), lambda l: (l, 0, 0)),
            pl.BlockSpec((A_W, 2 * A_W), lambda l: (0, 0)),
        ],
        out_specs=pl.BlockSpec((1, D_MODEL, 2 * A_W), lambda l: (l, 0, 0)),
        out_shape=jax.ShapeDtypeStruct((DEPTH, D_MODEL, 2 * A_W), BF16),
        compiler_params=_cparams("parallel"),
        name="fold_channel_dft",
    )(w_a, jnp.asarray(m))


def _modulated_norm(x, g, shift, scale):
    ms = jnp.mean(x * x, axis=-1, keepdims=True)
    return (x * lax.rsqrt(ms + EPS) * g) * (1.0 + scale) + shift


def _inproj_kernel(x_ref, mod_ref, g1_ref, cos_ref, sin_ref,
                   wfft_ref, wb_ref, wc_ref, wdq_ref, wdkv_ref, wkr_ref,
                   qg_ref, kvg_ref, wq2_ref, wkv2_ref, lng_ref, lnb_ref, ones_ref,
                   y1_ref, y2_ref, b0_ref, b1_ref, b2_ref, cu_ref, cvn_ref, mq_ref, mk_ref, mv_ref, res_sc, *, tm):
    x = x_ref[0]
    h = _modulated_norm(x, g1_ref[...], mod_ref[0, 0:1, :], mod_ref[0, 1:2, :])
    hb = h.astype(BF16)

    y = _dot(hb, wfft_ref[...])
    y1_ref[0] = y[:, :A_W].astype(BF16)
    y2_ref[0] = y[:, A_W:].astype(BF16)

    zb = _dot(hb, wb_ref[...])
    b0_ref[0] = zb[:, :B_QKV_W].astype(BF16)
    nres = 2 * B_QKV_W // LANE
    for c in range(nres):
        res_sc[c] = zb[:, B_QKV_W + c * LANE:B_QKV_W + (c + 1) * LANE]
    for g, out_ref in ((1, b1_ref), (2, b2_ref)):
        dil = B_PAIRS[g][1]
        for r in range(dil):
            for c in range(3):
                out_ref[0, r, :, c * LANE:(c + 1) * LANE] = (
                    res_sc.at[3 * (g - 1) + c][pl.ds(r, tm // dil, stride=dil), :].astype(BF16))

    zc = _dot(hb, wc_ref[...])
    cu_ref[0] = zc[:, :C_W].astype(BF16)
    cv = zc[:, C_W:]
    mu = jnp.mean(cv, axis=-1, keepdims=True)
    cc = cv - mu
    var = jnp.mean(cc * cc, axis=-1, keepdims=True)
    cvn_ref[0] = (cc * lax.rsqrt(var + EPS) * lng_ref[...] + lnb_ref[...]).astype(BF16)

    cos = cos_ref[...]
    sin = sin_ref[...]
    cos4 = jnp.concatenate([cos] * D_HEADS, axis=1)
    sin4 = jnp.concatenate([sin] * D_HEADS, axis=1)

    dcq = _dot(hb, wdq_ref[...])
    msq = jnp.mean(dcq * dcq, axis=-1, keepdims=True)
    qn = (dcq * lax.rsqrt(msq + EPS) * qg_ref[...]).astype(BF16)
    qq = _dot(qn, wq2_ref[...])
    q = (qq[:, :D_PAD_W] * cos4 + qq[:, D_PAD_W:] * sin4) * ((D_NOPE + D_ROPE) ** -0.5)
    mq_ref[0] = q.astype(BF16)

    dckv = _dot(hb, wdkv_ref[...])
    mskv = jnp.sum(dckv * dckv, axis=-1, keepdims=True) * (1.0 / D_KV_LORA)
    kvn = (dckv * lax.rsqrt(mskv + EPS) * kvg_ref[...]).astype(BF16)
    kv = _dot(kvn, wkv2_ref[...])
    kr = _dot(hb, wkr_ref[...])
    kpe = kr[:, :D_HEAD_PAD] * cos + kr[:, D_HEAD_PAD:] * sin
    mk_ref[0] = (kv[:, :D_PAD_W] + jnp.concatenate([kpe] * D_HEADS, axis=1)).astype(BF16)
    mv_ref[0] = (kv[:, D_PAD_W:] + ones_ref[...]).astype(BF16)


def _inproj(x, mod, lw, cos_t, sin_t, ones_row):
    B, S, _ = x.shape
    tm = ROW_TILE
    row = lambda w: pl.BlockSpec((1, tm, w), lambda b, i: (b, i, 0))
    tab = pl.BlockSpec((tm, D_HEAD_PAD), lambda b, i: (i, 0))
    weights = [lw["wfft"], lw["wb"], lw["wc"], lw["wdq"], lw["wdkv"], lw["wkr"],
               lw["qg"], lw["kvg"], lw["wq2"], lw["wkv2"], lw["lng"], lw["lnb"], ones_row]

    def nat(w):
        return row(w), jax.ShapeDtypeStruct((B, S, w), BF16)

    def residue(dil):
        return (pl.BlockSpec((1, dil, tm // dil, B_QKV_W), lambda b, i: (b, 0, i, 0)),
                jax.ShapeDtypeStruct((B, dil, S // dil, B_QKV_W), BF16))

    outs = [nat(A_W), nat(A_W), nat(B_QKV_W), residue(B_PAIRS[1][1]), residue(B_PAIRS[2][1]),
            nat(C_W), nat(C_W), nat(D_PAD_W), nat(D_PAD_W), nat(D_PAD_W)]
    return pl.pallas_call(
        functools.partial(_inproj_kernel, tm=tm),
        grid=(B, S // tm),
        in_specs=[row(D_MODEL),
                  pl.BlockSpec((1, N_MOD, D_MODEL), lambda b, i: (b, 0, 0)),
                  _resident(lw["g1"].shape), tab, tab] + [_resident(w.shape) for w in weights],
        out_specs=[o[0] for o in outs],
        out_shape=[o[1] for o in outs],
        scratch_shapes=[pltpu.VMEM((2 * B_QKV_W // LANE, tm, LANE), F32)],
        compiler_params=_cparams("parallel", "parallel"),
        name="inproj",
    )(x, mod, lw["g1"], cos_t, sin_t, *weights)


def _fft_stage1_kron_kernel(m_ref, y1_ref, y2_ref, ar_ref, ai_ref, *, n1, nsub):
    rows = n1 * DFT_SUB
    for s in range(nsub):
        js = slice(s * DFT_SUB, (s + 1) * DFT_SUB)
        rhs = jnp.concatenate([y1_ref[0, :, js, :].reshape(rows, A_W), y2_ref[0, :, js, :].reshape(rows, A_W)], axis=0)
        out = _dot(m_ref[...], rhs).astype(BF16)
        ar_ref[0, :, js, :] = out[:rows].reshape(n1, DFT_SUB, A_W)
        ai_ref[0, :, js, :] = out[rows:].reshape(n1, DFT_SUB, A_W)


def _fft_stage1_kernel(m_ref, y1_ref, y2_ref, ar_ref, ai_ref, in_sc, out_sc, *, n1):
    nl = A_W // LANE
    for c in range(nl):
        sl = slice(c * LANE, (c + 1) * LANE)
        in_sc[c] = y1_ref[0, :, :, sl].reshape(n1 * DFT_SUB, LANE).astype(F32)
        in_sc[nl + c] = y2_ref[0, :, :, sl].reshape(n1 * DFT_SUB, LANE).astype(F32)

    def body(j, carry):
        rows = pl.ds(j, n1, stride=DFT_SUB)
        top = jnp.concatenate([in_sc.at[c][rows, :] for c in range(nl)], axis=1)
        bot = jnp.concatenate([in_sc.at[nl + c][rows, :] for c in range(nl)], axis=1)
        out = _dot(m_ref[...], jnp.concatenate([top, bot], axis=0).astype(BF16))
        for c in range(nl):
            out_sc.at[c][rows, :] = out[:n1, c * LANE:(c + 1) * LANE]
            out_sc.at[nl + c][rows, :] = out[n1:, c * LANE:(c + 1) * LANE]
        return carry

    lax.fori_loop(0, DFT_SUB, body, 0)
    for c in range(nl):
        sl = slice(c * LANE, (c + 1) * LANE)
        ar_ref[0, :, :, sl] = out_sc[c].reshape(n1, DFT_SUB, LANE).astype(BF16)
        ai_ref[0, :, :, sl] = out_sc[nl + c].reshape(n1, DFT_SUB, LANE).astype(BF16)


def _fft_stage2_kernel(h_ref, ar_ref, ai_ref, o_ref, out_sc, *, n2):
    nl = A_W // LANE

    def body(j, carry):
        rhs = jnp.concatenate([ar_ref[0, j], ai_ref[0, j]], axis=0)
        out = _dot(h_ref[j], rhs)
        for c in range(nl):
            out_sc.at[c][pl.ds(j, n2, stride=DFT_SUB), :] = out[:, c * LANE:(c + 1) * LANE]
        return carry

    lax.fori_loop(0, DFT_SUB, body, 0)
    for c in range(nl):
        o_ref[0, :, :, c * LANE:(c + 1) * LANE] = out_sc[c].reshape(n2, DFT_SUB, LANE).astype(BF16)


def _fft_two_stage(m1, h2, y1, y2):
    B, S, _ = y1.shape
    n2 = DFT_N2
    n1 = S // n2
    nl = A_W // LANE
    if n1 == DFT_SUB:
        jb = min(n2, DFT_KRON_COLS)
        stage1 = functools.partial(_fft_stage1_kron_kernel, n1=n1, nsub=jb // DFT_SUB)
        scratch = []
    else:
        jb = DFT_SUB
        stage1 = functools.partial(_fft_stage1_kernel, n1=n1)
        scratch = [pltpu.VMEM((2 * nl, n1 * DFT_SUB, LANE), F32)] * 2
    blk = pl.BlockSpec((1, n1, jb, A_W), lambda b, j: (b, 0, j, 0))
    ar, ai = pl.pallas_call(
        stage1,
        grid=(B, n2 // jb),
        in_specs=[_resident(m1.shape), blk, blk],
        out_specs=[blk, blk],
        out_shape=[jax.ShapeDtypeStruct((B, n1, n2, A_W), BF16)] * 2,
        scratch_shapes=scratch,
        compiler_params=_cparams("parallel", "parallel"),
        name="fft_stage1",
    )(m1, y1.reshape(B, n1, n2, A_W), y2.reshape(B, n1, n2, A_W))
    slab = pl.BlockSpec((1, DFT_SUB, n2, A_W), lambda b, j: (b, j, 0, 0))
    out = pl.pallas_call(
        functools.partial(_fft_stage2_kernel, n2=n2),
        grid=(B, n1 // DFT_SUB),
        in_specs=[pl.BlockSpec((DFT_SUB, n2, 2 * n2), lambda b, j: (j, 0, 0)), slab, slab],
        out_specs=pl.BlockSpec((1, n2, DFT_SUB, A_W), lambda b, j: (b, 0, j, 0)),
        out_shape=jax.ShapeDtypeStruct((B, n2, n1, A_W), BF16),
        scratch_shapes=[pltpu.VMEM((nl, n2 * DFT_SUB, LANE), F32)],
        compiler_params=_cparams("parallel", "parallel"),
        name="fft_stage2",
    )(h2, ar, ai)
    return out.reshape(B, S, A_W)


def _dft_tables(S):
    scale = 1.0 / math.sqrt(S * A_GROUP_W)

    def cs(idx, period):
        ang = idx.astype(F32) * (2.0 * math.pi / period)
        return jnp.cos(ang), jnp.sin(ang)

    n2 = DFT_N2
    n1 = S // n2
    a = jnp.arange(n1, dtype=jnp.int32)
    c1, s1 = cs((a[:, None] * a[None, :]) % n1, n1)
    if n1 == DFT_SUB:
        eye = jnp.eye(DFT_SUB, dtype=F32)
        c1, s1 = jnp.kron(c1, eye), jnp.kron(s1, eye)
    m1 = jnp.concatenate([jnp.concatenate([c1, -s1], axis=1), jnp.concatenate([s1, c1], axis=1)], axis=0)
    k1 = a[:, None, None]
    k2 = jnp.arange(n2, dtype=jnp.int32)[None, :, None]
    nn = jnp.arange(n2, dtype=jnp.int32)[None, None, :]
    hc, hs = cs((nn * (k1 + n1 * k2)) % S, S)
    h2 = jnp.concatenate([hc * scale, -hs * scale], axis=2)
    return {"m1": m1.astype(BF16), "h2": h2.astype(BF16)}


def _fourier_seq(tables, y1, y2):
    return _fft_two_stage(tables["m1"], tables["h2"], y1, y2)


def _band_kernel(bias_ref, q_ref, k_ref, v_ref, o_ref, lse_ref, *, seq, rows, nseq, kwin):
    nsub = rows // B_QBLK
    last_blk = seq // B_QBLK - 1
    j = pl.program_id(1)
    head0 = lax.broadcasted_iota(jnp.int32, (B_QBLK, B_GROUP_W), 1) < B_HEAD_DIM

    def body(t, carry):
        si = t // nsub
        b = t % nsub
        r0 = pl.multiple_of(b * B_QBLK, B_QBLK)
        gb = j * nsub + b
        ws = pl.multiple_of(jnp.clip(gb * B_QBLK - B_HALF, 0, seq - kwin), B_HALF)
        tix = jnp.where(gb == 0, 0, jnp.where(gb == last_blk, 2, 1))
        q = q_ref[si, pl.ds(r0, B_QBLK), :]
        kw = k_ref[si, pl.ds(ws, kwin), :]
        vw = v_ref[si, pl.ds(ws, kwin), :]
        zero = jnp.zeros_like(q)
        q2 = jnp.concatenate([jnp.where(head0, q, zero), jnp.where(head0, zero, q)], axis=0)
        s = _dot_nt(q2, kw) + bias_ref[tix]
        m = jnp.max(s, axis=1, keepdims=True)
        p = jnp.exp(s - m)
        l = jnp.sum(p, axis=1, keepdims=True)
        o2 = _dot(p.astype(BF16), vw) * (1.0 / l)
        lse2 = jnp.broadcast_to(m + jnp.log(l), (2 * B_QBLK, B_GROUP_W))
        o_ref[si, pl.ds(r0, B_QBLK), :] = jnp.where(head0, o2[:B_QBLK], o2[B_QBLK:]).astype(BF16)
        lse_ref[si, pl.ds(r0, B_QBLK), :] = jnp.where(head0, lse2[:B_QBLK], lse2[B_QBLK:])
        return carry

    lax.fori_loop(0, nseq * nsub, body, 0, unroll=BAND_UNROLL)


def _band_attention(bias, arr, cols):
    N, L, _ = arr.shape
    kwin = bias.shape[-1]
    if L >= BAND_ROWS:
        nseq, rows = 1, BAND_ROWS
    else:
        nseq, rows = min(BAND_ROWS // L, N), L
    assert N % nseq == 0 and (nseq * rows // B_QBLK) % BAND_UNROLL == 0
    cq, ck, cv = cols
    out_spec = pl.BlockSpec((nseq, rows, B_GROUP_W), lambda n, j: (n, j, 0))
    return pl.pallas_call(
        functools.partial(_band_kernel, seq=L, rows=rows, nseq=nseq, kwin=kwin),
        grid=(N // nseq, L // rows),
        in_specs=[
            _resident(bias.shape),
            pl.BlockSpec((nseq, rows, B_GROUP_W), lambda n, j: (n, j, cq)),
            pl.BlockSpec((nseq, L, B_GROUP_W), lambda n, j: (n, 0, ck)),
            pl.BlockSpec((nseq, L, B_GROUP_W), lambda n, j: (n, 0, cv)),
        ],
        out_specs=[out_spec, out_spec],
        out_shape=[jax.ShapeDtypeStruct((N, L, B_GROUP_W), BF16), jax.ShapeDtypeStruct((N, L, B_GROUP_W), F32)],
        compiler_params=_cparams("parallel", "arbitrary"),
        name="band_attention",
    )(bias, arr, arr, arr)


def _t5_bucket(rel):
    half = REL_BUCKETS // 2
    max_exact = half // 2
    ret = jnp.where(rel > 0, half, 0)
    n = jnp.abs(rel)
    large = max_exact + (jnp.log(jnp.maximum(n, max_exact).astype(F32) / max_exact)
                         / math.log(REL_MAX_DIST / max_exact) * (half - max_exact)).astype(jnp.int32)
    large = jnp.minimum(large, half - 1)
    return ret + jnp.where(n < max_exact, n, large)


def _band_bias_tiles(rel_bias_g, dil, L):
    kwin = min(2 * B_QBLK, L)
    i = np.arange(B_QBLK)[:, None]
    jj = np.arange(kwin)[None, :]
    tiles = []
    for off in (0, -B_HALF, B_QBLK - kwin):
        rel = jj + off - i
        band = np.abs(rel) <= B_HALF
        bucket = _t5_bucket(jnp.asarray(rel * dil, dtype=jnp.int32))
        onehot = (bucket[..., None] == jnp.arange(REL_BUCKETS, dtype=jnp.int32)).astype(F32)
        vals = jnp.einsum("ijb,bh->hij", onehot, rel_bias_g.astype(F32), precision=HIGHEST)
        tiles.append(jnp.where(jnp.asarray(band)[None], vals, NEG_INF))
    return jnp.stack(tiles).reshape(3, 2 * B_QBLK, kwin)


def _dilated_branch(bias_tiles, b0, b1, b2):
    outs = []
    for g, arr in enumerate((b0, b1, b2)):
        lead = arr.shape[:-2]
        L = arr.shape[-2]
        o, lse = _band_attention(bias_tiles[g], arr.reshape(-1, L, B_QKV_W), (0, 1, 2))
        outs.append((o.reshape(*lead, L, B_GROUP_W), lse.reshape(*lead, L, B_GROUP_W)))
    return outs


def _mla_kernel(q_ref, k_ref, v_ref, o_ref, m_sc, acc_sc, *, tk):
    kv = pl.program_id(2)

    @pl.when(kv == 0)
    def _():
        m_sc[...] = jnp.full(m_sc.shape, -jnp.inf, F32)
        acc_sc[...] = jnp.zeros(acc_sc.shape, F32)

    for h in range(D_HEADS):
        sl = slice(h * D_HEAD_PAD, (h + 1) * D_HEAD_PAD)
        s = _dot_nt(q_ref[0, :, sl], k_ref[0, :, sl])
        m_prev = m_sc[h]
        m_new = jnp.maximum(m_prev, jnp.max(s, axis=1, keepdims=True))
        alpha = jnp.exp(m_prev - m_new)
        p = jnp.exp(s - jnp.concatenate([m_new] * (tk // LANE), axis=1))
        acc_sc[h] = alpha * acc_sc[h] + _dot(p.astype(BF16), v_ref[0, :, sl])
        m_sc[h] = m_new

    @pl.when(kv == pl.num_programs(2) - 1)
    def _():
        for h in range(D_HEADS):
            acc = acc_sc[h]
            o_ref[0, :, h * D_HEAD_PAD:(h + 1) * D_HEAD_PAD] = (acc * (1.0 / acc[:, D_V:D_V + 1])).astype(BF16)


def _mla_attention(q, k, v):
    B, S, _ = q.shape
    tq = min(ATT_TQ, S)
    tk = min(ATT_TK, S)
    return pl.pallas_call(
        functools.partial(_mla_kernel, tk=tk),
        grid=(B, S // tq, S // tk),
        in_specs=[
            pl.BlockSpec((1, tq, D_PAD_W), lambda b, i, j: (b, i, 0)),
            pl.BlockSpec((1, tk, D_PAD_W), lambda b, i, j: (b, j, 0)),
            pl.BlockSpec((1, tk, D_PAD_W), lambda b, i, j: (b, j, 0)),
        ],
        out_specs=pl.BlockSpec((1, tq, D_PAD_W), lambda b, i, j: (b, i, 0)),
        out_shape=jax.ShapeDtypeStruct((B, S, D_PAD_W), BF16),
        scratch_shapes=[pltpu.VMEM((D_HEADS, tq, LANE), F32), pltpu.VMEM((D_HEADS, tq, D_HEAD_PAD), F32)],
        compiler_params=_cparams("parallel", "parallel", "arbitrary"),
        name="mla_attention",
    )(q, k, v)


def _merge_kernel(x_ref, mod_ref, g1_ref, ya_ref, ob0_ref, ob1_ref, ob2_ref, l0_ref, l1_ref, l2_ref,
                  cu_ref, cvn_ref, yd_ref, wg_ref, pa_ref, pb_ref, pc_ref, pd_ref, wo_ref,
                  ws_ref, bs_ref, o_ref, nat_sc, *, tm):
    x = x_ref[0]
    h = _modulated_norm(x, g1_ref[...], mod_ref[0, 0:1, :], mod_ref[0, 1:2, :])
    hb = h.astype(BF16)

    def gate(i):
        return jax.nn.sigmoid(_dot(hb, wg_ref[:, i * D_MODEL:(i + 1) * D_MODEL]))

    merged = gate(0) * _dot(ya_ref[0], pa_ref[...])

    for k, (o_ref_g, l_ref_g, dil) in enumerate(((ob1_ref, l1_ref, B_PAIRS[1][1]), (ob2_ref, l2_ref, B_PAIRS[2][1]))):
        for r in range(dil):
            rows = pl.ds(r, tm // dil, stride=dil)
            nat_sc.at[2 * k][rows, :] = o_ref_g[0, r].astype(F32)
            nat_sc.at[2 * k + 1][rows, :] = l_ref_g[0, r]
    l0, l1, l2 = l0_ref[0], nat_sc[1], nat_sc[3]
    mx = jnp.maximum(jnp.maximum(l0, l1), l2)
    e0, e1, e2 = jnp.exp(l0 - mx), jnp.exp(l1 - mx), jnp.exp(l2 - mx)
    yb = (e0 * ob0_ref[0].astype(F32) + e1 * nat_sc[0] + e2 * nat_sc[2]) / (e0 + e1 + e2)
    merged = merged + gate(1) * _dot(yb.astype(BF16), pb_ref[...])

    head = lax.broadcasted_iota(jnp.int32, (C_CHUNK, C_W), 1) // C_HEAD_W
    chunks = []
    for c in range(tm // C_CHUNK):
        rs = slice(c * C_CHUNK, (c + 1) * C_CHUNK)
        mm = _dot(ws_ref[...], cvn_ref[0, rs, :])
        mixed = bs_ref[...]
        for hh in range(C_HEADS):
            mixed = mixed + jnp.where(head == hh, mm[hh * C_CHUNK:(hh + 1) * C_CHUNK], 0.0)
        chunks.append((cu_ref[0, rs, :].astype(F32) * mixed).astype(BF16))
    yc = jnp.concatenate(chunks, axis=0)
    merged = merged + gate(2) * _dot(yc, pc_ref[...])

    merged = merged + gate(3) * _dot(yd_ref[0], pd_ref[...])
    o_ref[0] = x + mod_ref[0, 2:3, :] * _dot(merged.astype(BF16), wo_ref[...])


def _merge(x, mod, lw, ya, band, cu, cvn, yd):
    B, S, _ = x.shape
    tm = ROW_TILE
    row = lambda w: pl.BlockSpec((1, tm, w), lambda b, i: (b, i, 0))
    weights = [lw["wg"], lw["pa"], lw["pb"], lw["pc"], lw["pd"], lw["wo"], lw["ws"], lw["bs"]]
    (ob0, l0), (ob1, l1), (ob2, l2) = band
    res = lambda dil: pl.BlockSpec((1, dil, tm // dil, B_GROUP_W), lambda b, i: (b, 0, i, 0))
    r1, r2 = res(B_PAIRS[1][1]), res(B_PAIRS[2][1])
    return pl.pallas_call(
        functools.partial(_merge_kernel, tm=tm),
        grid=(B, S // tm),
        in_specs=[row(D_MODEL), pl.BlockSpec((1, N_MOD, D_MODEL), lambda b, i: (b, 0, 0)), _resident(lw["g1"].shape),
                  row(A_W), row(B_GROUP_W), r1, r2, row(B_GROUP_W), r1, r2, row(C_W), row(C_W), row(D_PAD_W)]
                 + [_resident(w.shape) for w in weights],
        out_specs=row(D_MODEL),
        out_shape=jax.ShapeDtypeStruct((B, S, D_MODEL), F32),
        scratch_shapes=[pltpu.VMEM((4, tm, B_GROUP_W), F32)],
        compiler_params=_cparams("parallel", "parallel"),
        name="merge",
    )(x, mod, lw["g1"], ya, ob0, ob1, ob2, l0, l1, l2, cu, cvn, yd, *weights)


def _mlp_kernel(x_ref, mod_ref, g2_ref, w1_ref, w2_ref, gf_ref, o_ref, *, final):
    x = x_ref[0]
    h = _modulated_norm(x, g2_ref[...], mod_ref[0, 3:4, :], mod_ref[0, 4:5, :])
    a = jnp.maximum(_dot(h.astype(BF16), w1_ref[...]), 0.0)
    y = x + mod_ref[0, 5:6, :] * _dot((a * a).astype(BF16), w2_ref[...])
    if final:
        ms = jnp.mean(y * y, axis=-1, keepdims=True)
        y = y * lax.rsqrt(ms + EPS) * gf_ref[...]
    o_ref[0] = y


def _mlp(x, mod, lw, final_g, final):
    B, S, _ = x.shape
    tm = ROW_TILE
    row = pl.BlockSpec((1, tm, D_MODEL), lambda b, i: (b, i, 0))
    return pl.pallas_call(
        functools.partial(_mlp_kernel, final=final),
        grid=(B, S // tm),
        in_specs=[row, pl.BlockSpec((1, N_MOD, D_MODEL), lambda b, i: (b, 0, 0)), _resident(lw["g2"].shape),
                  _resident(lw["w1"].shape), _resident(lw["w2"].shape), _resident(final_g.shape)],
        out_specs=row,
        out_shape=jax.ShapeDtypeStruct((B, S, D_MODEL), F32),
        compiler_params=_cparams("parallel", "parallel"),
        name="mlp",
    )(x, mod, lw["g2"], lw["w1"], lw["w2"], final_g)


def _head_slots(w, per_head, take):
    k = w.shape[0]
    wh = w.reshape(k, D_HEADS, per_head)[:, :, take]
    pad = D_HEAD_PAD - wh.shape[-1]
    return jnp.pad(wh, ((0, 0), (0, 0), (0, pad))).reshape(k, D_PAD_W)


def _rope_rot_cols(w):
    half = D_ROPE // 2
    return jnp.concatenate([-w[..., half:], w[..., :half]], axis=-1)


def _prepare_layer(l, w_in, wfft, norm1_g, q_norm_g, kv_norm_g, w_uq, w_ukv, sgu_ln_g, sgu_ln_b, sgu_w, sgu_b,
                   p_a, p_b, p_c, p_d, w_o, norm2_g, w1, w2):
    wi = w_in[l]
    o = A_W
    wb = wi[:, o:o + 3 * B_QKV_W]; o += 3 * B_QKV_W
    qkv_scale = jnp.asarray([B_HEAD_DIM ** -0.5, 1.0, 1.0], F32)[None, None, :, None]
    wb = (wb.reshape(D_MODEL, 3, len(B_PAIRS), B_GROUP_W).transpose(0, 2, 1, 3) * qkv_scale).reshape(D_MODEL, 3 * B_QKV_W)
    wc = wi[:, o:o + 2 * C_W]; o += 2 * C_W
    wdq = wi[:, o:o + D_Q_LORA]; o += D_Q_LORA
    wdkv = wi[:, o:o + D_KV_LORA]; o += D_KV_LORA
    wkr = wi[:, o:o + D_ROPE]; o += D_ROPE
    wg = wi[:, o:]

    def rope_slot(w):
        return jnp.pad(w, ((0, 0), (D_NOPE, D_HEAD_PAD - D_NOPE - D_ROPE)))

    wkr2 = jnp.concatenate([rope_slot(wkr), rope_slot(_rope_rot_cols(wkr))], axis=1)

    uq = w_uq[l].reshape(D_Q_LORA, D_HEADS, D_NOPE + D_ROPE)
    uq_rot = jnp.concatenate([jnp.zeros_like(uq[..., :D_NOPE]), _rope_rot_cols(uq[..., D_NOPE:])], axis=-1)
    padq = ((0, 0), (0, 0), (0, D_HEAD_PAD - D_NOPE - D_ROPE))
    wq2 = jnp.concatenate([jnp.pad(uq, padq).reshape(D_Q_LORA, D_PAD_W),
                           jnp.pad(uq_rot, padq).reshape(D_Q_LORA, D_PAD_W)], axis=1)

    ukv = jnp.pad(w_ukv[l], ((0, D_KV_PAD - D_KV_LORA), (0, 0)))
    wkv2 = jnp.concatenate([_head_slots(ukv, D_NOPE + D_V, slice(0, D_NOPE)),
                            _head_slots(ukv, D_NOPE + D_V, slice(D_NOPE, D_NOPE + D_V))], axis=1)

    pd_pad = jnp.pad(p_d[l].reshape(D_HEADS, D_V, D_MODEL), ((0, 0), (0, D_HEAD_PAD - D_V), (0, 0)))
    bias_sgu = jnp.repeat(sgu_b[l].T, C_HEAD_W, axis=1)
    row = lambda v: v.reshape(1, -1).astype(F32)
    return {
        "g1": row(norm1_g[l]), "g2": row(norm2_g[l]),
        "wfft": wfft[l], "wb": wb.astype(BF16), "wc": wc.astype(BF16), "wdq": wdq.astype(BF16),
        "wdkv": jnp.pad(wdkv, ((0, 0), (0, D_KV_PAD - D_KV_LORA))).astype(BF16),
        "wkr": wkr2.astype(BF16), "wg": wg.astype(BF16),
        "qg": row(q_norm_g[l]), "kvg": row(jnp.pad(kv_norm_g[l], (0, D_KV_PAD - D_KV_LORA))),
        "wq2": wq2.astype(BF16), "wkv2": wkv2.astype(BF16),
        "lng": row(sgu_ln_g[l]), "lnb": row(sgu_ln_b[l]),
        "ws": sgu_w[l].reshape(C_HEADS * C_CHUNK, C_CHUNK).astype(BF16), "bs": bias_sgu.astype(F32),
        "pa": p_a[l].astype(BF16), "pb": p_b[l].astype(BF16), "pc": p_c[l].astype(BF16),
        "pd": pd_pad.reshape(D_PAD_W, D_MODEL).astype(BF16), "wo": w_o[l].astype(BF16),
        "w1": w1[l].astype(BF16), "w2": w2[l].astype(BF16),
    }


def _rope_tables(S):
    half = D_ROPE // 2
    inv = ROPE_THETA ** (-jnp.arange(half, dtype=F32) / half)
    ang = jnp.arange(S, dtype=F32)[:, None] * inv[None, :]
    cos = jnp.cos(ang)
    sin = jnp.sin(ang)
    pad = jnp.zeros((S, D_HEAD_PAD - D_NOPE - D_ROPE), F32)
    cos_t = jnp.concatenate([jnp.ones((S, D_NOPE), F32), cos, cos, pad], axis=1)
    sin_t = jnp.concatenate([jnp.zeros((S, D_NOPE), F32), sin, sin, pad], axis=1)
    return cos_t, sin_t


def _run_group(x, mods, layers, bias_tiles_fn, final_g):
    B, S, _ = x.shape
    cos_t, sin_t = _rope_tables(S)
    tables = _dft_tables(S)
    bias_tiles = bias_tiles_fn(S)
    ones_np = np.zeros((D_HEADS, D_HEAD_PAD), np.float32)
    ones_np[:, D_V] = 1.0
    ones_row = jnp.asarray(ones_np.reshape(1, D_PAD_W))
    gf = final_g.reshape(1, D_MODEL)
    for l, lw in enumerate(layers):
        mod = mods[l]
        y1, y2, b0, b1, b2, cu, cvn, mq, mk, mv = _inproj(x, mod, lw, cos_t, sin_t, ones_row)
        ya = _fourier_seq(tables, y1, y2)
        band = _dilated_branch(bias_tiles, b0, b1, b2)
        yd = _mla_attention(mq, mk, mv)
        x = _merge(x, mod, lw, ya, band, cu, cvn, yd)
        x = _mlp(x, mod, lw, gf, final=(l == len(layers) - 1))
    return x


def kernel(x_prompt, x_sample, c_prompt, c_sample, rel_bias, ada_w, ada_b, norm1_g, w_in, mla_q_norm_g, mla_kv_norm_g, mla_w_uq, mla_w_ukv, sgu_ln_g, sgu_ln_b, sgu_w, sgu_b, p_a, p_b, p_c, p_d, w_o, norm2_g, mlp_w1, mlp_w2, final_g):
    nb_p = c_prompt.shape[0]
    nb_s = c_sample.shape[0]
    rows = -(-(nb_p + nb_s) // 8) * 8
    c_all = jnp.concatenate([c_prompt, c_sample, jnp.zeros((rows - nb_p - nb_s, D_MODEL), F32)], axis=0)
    mod_all = _ada_mod(c_all, ada_w, ada_b).reshape(DEPTH, rows, N_MOD, D_MODEL)
    mods_p = mod_all[:, :nb_p]
    mods_s = mod_all[:, nb_p:nb_p + nb_s]

    wfft = _fold_channel_dft(w_in[:, :, :A_W])
    layers = [_prepare_layer(l, w_in, wfft, norm1_g, mla_q_norm_g, mla_kv_norm_g, mla_w_uq, mla_w_ukv,
                             sgu_ln_g, sgu_ln_b, sgu_w, sgu_b, p_a, p_b, p_c, p_d, w_o, norm2_g, mlp_w1, mlp_w2)
              for l in range(DEPTH)]

    tile_cache = {}

    def bias_tiles_fn(S):
        out = []
        for g, (_, dil) in enumerate(B_PAIRS):
            key = (g, min(2 * B_QBLK, S // dil))
            if key not in tile_cache:
                tile_cache[key] = _band_bias_tiles(rel_bias[:, 2 * g:2 * g + 2], dil, S // dil)
            out.append(tile_cache[key])
        return out

    y_prompt = _run_group(x_prompt, mods_p, layers, bias_tiles_fn, final_g)
    y_sample = _run_group(x_sample, mods_s, layers, bias_tiles_fn, final_g)
    return (y_prompt, y_sample)
```

```python
import functools
import math

import jax
import jax.numpy as jnp
import numpy as np
from jax import lax
from jax.experimental import pallas as pl
from jax.experimental.pallas import tpu as pltpu

F32 = jnp.float32
BF16 = jnp.bfloat16
HIGHEST = lax.Precision.HIGHEST

D_MODEL = 1024
DEPTH = 4
EPS = 1e-6
NEG_INF = -1e30
N_MOD = 6
A_GROUPS = 4
A_GROUP_W = 192
A_W = A_GROUPS * A_GROUP_W
B_PAIRS = ((128, 1), (512, 4), (2048, 16))
B_HEAD_DIM = 64
B_QKV_W = 384
B_GROUP_W = 128
B_HALF = 64
B_QBLK = 128
REL_BUCKETS = 32
REL_MAX_DIST = 1024
C_HEADS = 4
C_HEAD_W = 96
C_W = 384
C_CHUNK = 128
D_HEADS = 4
D_Q_LORA = 384
D_KV_LORA = 320
D_KV_PAD = 384
D_NOPE = 64
D_ROPE = 32
D_V = 64
D_HEAD_PAD = 128
D_PAD_W = D_HEADS * D_HEAD_PAD
ROPE_THETA = 10000.0
LOG2_E = math.log2(math.e)
D_FF = 4 * D_MODEL
MIX_W = A_W + 3 * B_QKV_W + 2 * C_W + D_Q_LORA + D_KV_LORA + D_ROPE

VMEM_LIMIT_BYTES = 56 * 1024 * 1024
LANE = 128
ROW_TILE = 512
DFT_N2 = 128
DFT_KRON_COLS = 64
DFT_PITCH_PAD = 8
DFT_COPY_UNROLL = 8
DFT_MATMUL_UNROLL = 4
DFT_SUB = 16
ATT_TQ = 2048
ATT_TK = 1024
BAND_ROWS = 4096
BAND_UNROLL = 16


def _cparams(*sem):
    return pltpu.CompilerParams(dimension_semantics=sem, vmem_limit_bytes=VMEM_LIMIT_BYTES)


def _resident(shape):
    nd = len(shape)
    return pl.BlockSpec(shape, lambda *_: (0,) * nd, pipeline_mode=pl.Buffered(1))


def _dot(a, b):
    return jnp.dot(a, b, preferred_element_type=F32)


def _dot_nt(a, b):
    return lax.dot_general(a, b, (((1,), (1,)), ((), ())), preferred_element_type=F32)


def _ada_kernel(c_ref, w_ref, b_ref, o_ref):
    c = c_ref[...]
    a = c * jax.nn.sigmoid(c)
    o_ref[0] = jnp.dot(a, w_ref[0], preferred_element_type=F32, precision=HIGHEST) + b_ref[0]


def _ada_mod(c_all, ada_w, ada_b):
    rows = c_all.shape[0]
    tn = 1536
    return pl.pallas_call(
        _ada_kernel,
        grid=(DEPTH, N_MOD * D_MODEL // tn),
        in_specs=[
            pl.BlockSpec((rows, D_MODEL), lambda l, j: (0, 0)),
            pl.BlockSpec((1, D_MODEL, tn), lambda l, j: (l, 0, j)),
            pl.BlockSpec((1, 1, tn), lambda l, j: (l, 0, j)),
        ],
        out_specs=pl.BlockSpec((1, rows, tn), lambda l, j: (l, 0, j)),
        out_shape=jax.ShapeDtypeStruct((DEPTH, rows, N_MOD * D_MODEL), F32),
        compiler_params=_cparams("parallel", "parallel"),
        name="ada_mod",
    )(c_all, ada_w, ada_b.reshape(DEPTH, 1, N_MOD * D_MODEL))


def _fold_kernel(w_ref, m_ref, o_ref):
    o_ref[0] = jnp.dot(w_ref[0], m_ref[...], preferred_element_type=F32, precision=HIGHEST).astype(BF16)


def _fold_channel_dft(w_a):
    n = np.arange(A_GROUP_W)
    idx = (n[:, None] * n[None, :]) % A_GROUP_W
    ang = 2.0 * np.pi * idx / A_GROUP_W
    eye = np.eye(A_GROUPS)
    m = np.concatenate([np.kron(eye, np.cos(ang)), np.kron(eye, np.sin(ang))], axis=1).astype(np.float32)
    return pl.pallas_call(
        _fold_kernel,
        grid=(DEPTH,),
        in_specs=[
            pl.BlockSpec((1, D_MODEL, A_W), lambda l: (l, 0, 0)),
            pl.BlockSpec((A_W, 2 * A_W), lambda l: (0, 0)),
        ],
        out_specs=pl.BlockSpec((1, D_MODEL, 2 * A_W), lambda l: (l, 0, 0)),
        out_shape=jax.ShapeDtypeStruct((DEPTH, D_MODEL, 2 * A_W), BF16),
        compiler_params=_cparams("parallel"),
        name="fold_channel_dft",
    )(w_a, jnp.asarray(m))


def _modulated_norm(x, g, shift, scale):
    ms = jnp.mean(x * x, axis=-1, keepdims=True)
    return (x * lax.rsqrt(ms + EPS) * g) * (1.0 + scale) + shift


def _inproj_kernel(x_ref, mod_ref, g1_ref, cos_ref, sin_ref,
                   wfft_ref, wb_ref, wc_ref, wdq_ref, wdkv_ref, wkr_ref,
                   qg_ref, kvg_ref, wq2_ref, wkv2_ref, lng_ref, lnb_ref, ones_ref,
                   y1_ref, y2_ref, b0_ref, b1_ref, b2_ref, cu_ref, cvn_ref, mq_ref, mk_ref, mv_ref, res_sc, *, tm):
    x = x_ref[0]
    h = _modulated_norm(x, g1_ref[...], mod_ref[0, 0:1, :], mod_ref[0, 1:2, :])
    hb = h.astype(BF16)

    y = _dot(hb, wfft_ref[...])
    y1_ref[0] = y[:, :A_W].astype(BF16)
    y2_ref[0] = y[:, A_W:].astype(BF16)

    zb = _dot(hb, wb_ref[...])
    b0_ref[0] = zb[:, :B_QKV_W].astype(BF16)
    nres = 2 * B_QKV_W // LANE
    for c in range(nres):
        res_sc[c] = zb[:, B_QKV_W + c * LANE:B_QKV_W + (c + 1) * LANE]
    for g, out_ref in ((1, b1_ref), (2, b2_ref)):
        dil = B_PAIRS[g][1]
        for r in range(dil):
            for c in range(3):
                out_ref[0, r, :, c * LANE:(c + 1) * LANE] = (
                    res_sc.at[3 * (g - 1) + c][pl.ds(r, tm // dil, stride=dil), :].astype(BF16))

    zc = _dot(hb, wc_ref[...])
    cu_ref[0] = zc[:, :C_W].astype(BF16)
    cv = zc[:, C_W:]
    mu = jnp.mean(cv, axis=-1, keepdims=True)
    cc = cv - mu
    var = jnp.mean(cc * cc, axis=-1, keepdims=True)
    cvn_ref[0] = (cc * lax.rsqrt(var + EPS) * lng_ref[...] + lnb_ref[...]).astype(BF16)

    cos = cos_ref[...]
    sin = sin_ref[...]
    cos4 = jnp.concatenate([cos] * D_HEADS, axis=1)
    sin4 = jnp.concatenate([sin] * D_HEADS, axis=1)

    dcq = _dot(hb, wdq_ref[...])
    msq = jnp.mean(dcq * dcq, axis=-1, keepdims=True)
    qn = (dcq * lax.rsqrt(msq + EPS) * qg_ref[...]).astype(BF16)
    qq = _dot(qn, wq2_ref[...])
    q = (qq[:, :D_PAD_W] * cos4 + qq[:, D_PAD_W:] * sin4) * ((D_NOPE + D_ROPE) ** -0.5 * LOG2_E)
    mq_ref[0] = q.astype(BF16)

    dckv = _dot(hb, wdkv_ref[...])
    mskv = jnp.sum(dckv * dckv, axis=-1, keepdims=True) * (1.0 / D_KV_LORA)
    kvn = (dckv * lax.rsqrt(mskv + EPS) * kvg_ref[...]).astype(BF16)
    kv = _dot(kvn, wkv2_ref[...])
    kr = _dot(hb, wkr_ref[...])
    kpe = kr[:, :D_HEAD_PAD] * cos + kr[:, D_HEAD_PAD:] * sin
    mk_ref[0] = (kv[:, :D_PAD_W] + jnp.concatenate([kpe] * D_HEADS, axis=1)).astype(BF16)
    mv_ref[0] = (kv[:, D_PAD_W:] + ones_ref[...]).astype(BF16)


def _inproj(x, mod, lw, cos_t, sin_t, ones_row):
    B, S, _ = x.shape
    tm = ROW_TILE
    row = lambda w: pl.BlockSpec((1, tm, w), lambda b, i: (b, i, 0))
    tab = pl.BlockSpec((tm, D_HEAD_PAD), lambda b, i: (i, 0))
    weights = [lw["wfft"], lw["wb"], lw["wc"], lw["wdq"], lw["wdkv"], lw["wkr"],
               lw["qg"], lw["kvg"], lw["wq2"], lw["wkv2"], lw["lng"], lw["lnb"], ones_row]

    def nat(w):
        return row(w), jax.ShapeDtypeStruct((B, S, w), BF16)

    def residue(dil):
        return (pl.BlockSpec((1, dil, tm // dil, B_QKV_W), lambda b, i: (b, 0, i, 0)),
                jax.ShapeDtypeStruct((B, dil, S // dil, B_QKV_W), BF16))

    outs = [nat(A_W), nat(A_W), nat(B_QKV_W), residue(B_PAIRS[1][1]), residue(B_PAIRS[2][1]),
            nat(C_W), nat(C_W), nat(D_PAD_W), nat(D_PAD_W), nat(D_PAD_W)]
    return pl.pallas_call(
        functools.partial(_inproj_kernel, tm=tm),
        grid=(B, S // tm),
        in_specs=[row(D_MODEL),
                  pl.BlockSpec((1, N_MOD, D_MODEL), lambda b, i: (b, 0, 0)),
                  _resident(lw["g1"].shape), tab, tab] + [_resident(w.shape) for w in weights],
        out_specs=[o[0] for o in outs],
        out_shape=[o[1] for o in outs],
        scratch_shapes=[pltpu.VMEM((2 * B_QKV_W // LANE, tm, LANE), F32)],
        compiler_params=_cparams("parallel", "parallel"),
        name="inproj",
    )(x, mod, lw["g1"], cos_t, sin_t, *weights)


def _fft_stage1_kron_kernel(m_ref, y1_ref, y2_ref, ar_ref, ai_ref, *, n1, nsub):
    rows = n1 * DFT_SUB
    for s in range(nsub):
        js = slice(s * DFT_SUB, (s + 1) * DFT_SUB)
        rhs = jnp.concatenate([y1_ref[0, :, js, :].reshape(rows, A_W), y2_ref[0, :, js, :].reshape(rows, A_W)], axis=0)
        out = _dot(m_ref[...], rhs).astype(BF16)
        ar_ref[0, :, js, :] = out[:rows].reshape(n1, DFT_SUB, A_W)
        ai_ref[0, :, js, :] = out[rows:].reshape(n1, DFT_SUB, A_W)


def _fft_stage1_kernel(m_ref, y1_ref, y2_ref, ar_ref, ai_ref, sc, *, n1):
    nl = A_W // LANE
    pitch = n1 + DFT_PITCH_PAD

    def fill(i, carry):
        rows = pl.ds(i, DFT_SUB, stride=pitch)
        for c in range(nl):
            sl = slice(c * LANE, (c + 1) * LANE)
            sc.at[c][rows, :] = y1_ref[0, i, :, sl].astype(F32)
            sc.at[nl + c][rows, :] = y2_ref[0, i, :, sl].astype(F32)
        return carry

    lax.fori_loop(0, n1, fill, 0, unroll=DFT_COPY_UNROLL)

    def body(j, carry):
        rows = pl.ds(pl.multiple_of(j * pitch, 8), n1)
        top = jnp.concatenate([sc.at[c][rows, :] for c in range(nl)], axis=1)
        bot = jnp.concatenate([sc.at[nl + c][rows, :] for c in range(nl)], axis=1)
        out = _dot(m_ref[...], jnp.concatenate([top, bot], axis=0).astype(BF16))
        for c in range(nl):
            sc.at[c][rows, :] = out[:n1, c * LANE:(c + 1) * LANE]
            sc.at[nl + c][rows, :] = out[n1:, c * LANE:(c + 1) * LANE]
        return carry

    lax.fori_loop(0, DFT_SUB, body, 0, unroll=DFT_MATMUL_UNROLL)

    def drain(i, carry):
        rows = pl.ds(i, DFT_SUB, stride=pitch)
        for c in range(nl):
            sl = slice(c * LANE, (c + 1) * LANE)
            ar_ref[0, i, :, sl] = sc.at[c][rows, :].astype(BF16)
            ai_ref[0, i, :, sl] = sc.at[nl + c][rows, :].astype(BF16)
        return carry

    lax.fori_loop(0, n1, drain, 0, unroll=DFT_COPY_UNROLL)


def _fft_stage2_kernel(h_ref, ar_ref, ai_ref, o_ref, sc, *, n2):
    nl = A_W // LANE
    pitch = n2 + DFT_PITCH_PAD

    def body(j, carry):
        rhs = jnp.concatenate([ar_ref[0, j], ai_ref[0, j]], axis=0)
        out = _dot(h_ref[j], rhs)
        rows = pl.ds(pl.multiple_of(j * pitch, 8), n2)
        for c in range(nl):
            sc.at[c][rows, :] = out[:, c * LANE:(c + 1) * LANE]
        return carry

    lax.fori_loop(0, DFT_SUB, body, 0, unroll=DFT_MATMUL_UNROLL)

    def drain(k2, carry):
        rows = pl.ds(k2, DFT_SUB, stride=pitch)
        for c in range(nl):
            o_ref[0, k2, :, c * LANE:(c + 1) * LANE] = sc.at[c][rows, :].astype(BF16)
        return carry

    lax.fori_loop(0, n2, drain, 0, unroll=DFT_COPY_UNROLL)


def _fft_two_stage(m1, h2, y1, y2):
    B, S, _ = y1.shape
    n2 = DFT_N2
    n1 = S // n2
    nl = A_W // LANE
    if n1 == DFT_SUB:
        jb = min(n2, DFT_KRON_COLS)
        stage1 = functools.partial(_fft_stage1_kron_kernel, n1=n1, nsub=jb // DFT_SUB)
        scratch = []
    else:
        jb = DFT_SUB
        stage1 = functools.partial(_fft_stage1_kernel, n1=n1)
        scratch = [pltpu.VMEM((2 * nl, DFT_SUB * (n1 + DFT_PITCH_PAD), LANE), F32)]
    blk = pl.BlockSpec((1, n1, jb, A_W), lambda b, j: (b, 0, j, 0))
    ar, ai = pl.pallas_call(
        stage1,
        grid=(B, n2 // jb),
        in_specs=[_resident(m1.shape), blk, blk],
        out_specs=[blk, blk],
        out_shape=[jax.ShapeDtypeStruct((B, n1, n2, A_W), BF16)] * 2,
        scratch_shapes=scratch,
        compiler_params=_cparams("parallel", "parallel"),
        name="fft_stage1",
    )(m1, y1.reshape(B, n1, n2, A_W), y2.reshape(B, n1, n2, A_W))
    slab = pl.BlockSpec((1, DFT_SUB, n2, A_W), lambda b, j: (b, j, 0, 0))
    out = pl.pallas_call(
        functools.partial(_fft_stage2_kernel, n2=n2),
        grid=(B, n1 // DFT_SUB),
        in_specs=[pl.BlockSpec((DFT_SUB, n2, 2 * n2), lambda b, j: (j, 0, 0)), slab, slab],
        out_specs=pl.BlockSpec((1, n2, DFT_SUB, A_W), lambda b, j: (b, 0, j, 0)),
        out_shape=jax.ShapeDtypeStruct((B, n2, n1, A_W), BF16),
        scratch_shapes=[pltpu.VMEM((nl, DFT_SUB * (n2 + DFT_PITCH_PAD), LANE), F32)],
        compiler_params=_cparams("parallel", "parallel"),
        name="fft_stage2",
    )(h2, ar, ai)
    return out.reshape(B, S, A_W)


def _dft_tables(S):
    scale = 1.0 / math.sqrt(S * A_GROUP_W)

    def cs(idx, period):
        ang = idx.astype(F32) * (2.0 * math.pi / period)
        return jnp.cos(ang), jnp.sin(ang)

    n2 = DFT_N2
    n1 = S // n2
    a = jnp.arange(n1, dtype=jnp.int32)
    c1, s1 = cs((a[:, None] * a[None, :]) % n1, n1)
    if n1 == DFT_SUB:
        eye = jnp.eye(DFT_SUB, dtype=F32)
        c1, s1 = jnp.kron(c1, eye), jnp.kron(s1, eye)
    m1 = jnp.concatenate([jnp.concatenate([c1, -s1], axis=1), jnp.concatenate([s1, c1], axis=1)], axis=0)
    k1 = a[:, None, None]
    k2 = jnp.arange(n2, dtype=jnp.int32)[None, :, None]
    nn = jnp.arange(n2, dtype=jnp.int32)[None, None, :]
    hc, hs = cs((nn * (k1 + n1 * k2)) % S, S)
    h2 = jnp.concatenate([hc * scale, -hs * scale], axis=2)
    return {"m1": m1.astype(BF16), "h2": h2.astype(BF16)}


def _fourier_seq(tables, y1, y2):
    return _fft_two_stage(tables["m1"], tables["h2"], y1, y2)


def _band_kernel(bias_ref, q_ref, k_ref, v_ref, o_ref, lse_ref, *, seq, rows, nseq, kwin):
    nsub = rows // B_QBLK
    last_blk = seq // B_QBLK - 1
    j = pl.program_id(1)
    head0 = lax.broadcasted_iota(jnp.int32, (B_QBLK, B_GROUP_W), 1) < B_HEAD_DIM

    def body(t, carry):
        si = t // nsub
        b = t % nsub
        r0 = pl.multiple_of(b * B_QBLK, B_QBLK)
        gb = j * nsub + b
        ws = pl.multiple_of(jnp.clip(gb * B_QBLK - B_HALF, 0, seq - kwin), B_HALF)
        tix = jnp.where(gb == 0, 0, jnp.where(gb == last_blk, 2, 1))
        q = q_ref[si, pl.ds(r0, B_QBLK), :]
        kw = k_ref[si, pl.ds(ws, kwin), :]
        vw = v_ref[si, pl.ds(ws, kwin), :]
        zero = jnp.zeros_like(q)
        q2 = jnp.concatenate([jnp.where(head0, q, zero), jnp.where(head0, zero, q)], axis=0)
        s = _dot_nt(q2, kw) + bias_ref[tix]
        m = jnp.max(s, axis=1, keepdims=True)
        p = jnp.exp(s - m)
        l = jnp.sum(p, axis=1, keepdims=True)
        o2 = _dot(p.astype(BF16), vw) * (1.0 / l)
        lse2 = jnp.broadcast_to(m + jnp.log(l), (2 * B_QBLK, B_GROUP_W))
        o_ref[si, pl.ds(r0, B_QBLK), :] = jnp.where(head0, o2[:B_QBLK], o2[B_QBLK:]).astype(BF16)
        lse_ref[si, pl.ds(r0, B_QBLK), :] = jnp.where(head0, lse2[:B_QBLK], lse2[B_QBLK:])
        return carry

    lax.fori_loop(0, nseq * nsub, body, 0, unroll=BAND_UNROLL)


def _band_attention(bias, arr, cols):
    N, L, _ = arr.shape
    kwin = bias.shape[-1]
    if L >= BAND_ROWS:
        nseq, rows = 1, BAND_ROWS
    else:
        nseq, rows = min(BAND_ROWS // L, N), L
    assert N % nseq == 0 and (nseq * rows // B_QBLK) % BAND_UNROLL == 0
    cq, ck, cv = cols
    out_spec = pl.BlockSpec((nseq, rows, B_GROUP_W), lambda n, j: (n, j, 0))
    return pl.pallas_call(
        functools.partial(_band_kernel, seq=L, rows=rows, nseq=nseq, kwin=kwin),
        grid=(N // nseq, L // rows),
        in_specs=[
            _resident(bias.shape),
            pl.BlockSpec((nseq, rows, B_GROUP_W), lambda n, j: (n, j, cq)),
            pl.BlockSpec((nseq, L, B_GROUP_W), lambda n, j: (n, 0, ck)),
            pl.BlockSpec((nseq, L, B_GROUP_W), lambda n, j: (n, 0, cv)),
        ],
        out_specs=[out_spec, out_spec],
        out_shape=[jax.ShapeDtypeStruct((N, L, B_GROUP_W), BF16), jax.ShapeDtypeStruct((N, L, B_GROUP_W), F32)],
        compiler_params=_cparams("parallel", "arbitrary"),
        name="band_attention",
    )(bias, arr, arr, arr)


def _t5_bucket(rel):
    half = REL_BUCKETS // 2
    max_exact = half // 2
    ret = jnp.where(rel > 0, half, 0)
    n = jnp.abs(rel)
    large = max_exact + (jnp.log(jnp.maximum(n, max_exact).astype(F32) / max_exact)
                         / math.log(REL_MAX_DIST / max_exact) * (half - max_exact)).astype(jnp.int32)
    large = jnp.minimum(large, half - 1)
    return ret + jnp.where(n < max_exact, n, large)


def _band_bias_tiles(rel_bias_g, dil, L):
    kwin = min(2 * B_QBLK, L)
    i = np.arange(B_QBLK)[:, None]
    jj = np.arange(kwin)[None, :]
    tiles = []
    for off in (0, -B_HALF, B_QBLK - kwin):
        rel = jj + off - i
        band = np.abs(rel) <= B_HALF
        bucket = _t5_bucket(jnp.asarray(rel * dil, dtype=jnp.int32))
        onehot = (bucket[..., None] == jnp.arange(REL_BUCKETS, dtype=jnp.int32)).astype(F32)
        vals = jnp.einsum("ijb,bh->hij", onehot, rel_bias_g.astype(F32), precision=HIGHEST)
        tiles.append(jnp.where(jnp.asarray(band)[None], vals, NEG_INF))
    return jnp.stack(tiles).reshape(3, 2 * B_QBLK, kwin)


def _dilated_branch(bias_tiles, b0, b1, b2):
    outs = []
    for g, arr in enumerate((b0, b1, b2)):
        lead = arr.shape[:-2]
        L = arr.shape[-2]
        o, lse = _band_attention(bias_tiles[g], arr.reshape(-1, L, B_QKV_W), (0, 1, 2))
        outs.append((o.reshape(*lead, L, B_GROUP_W), lse.reshape(*lead, L, B_GROUP_W)))
    return outs


def _mla_kernel(q_ref, k_ref, v_ref, o_ref, m_sc, acc_sc, *, tk):
    kv = pl.program_id(2)

    @pl.when(kv == 0)
    def _():
        m_sc[...] = jnp.full(m_sc.shape, -jnp.inf, F32)
        acc_sc[...] = jnp.zeros(acc_sc.shape, F32)

    for h in range(D_HEADS):
        sl = slice(h * D_HEAD_PAD, (h + 1) * D_HEAD_PAD)
        s = _dot_nt(q_ref[0, :, sl], k_ref[0, :, sl])
        m_prev = m_sc[h]
        m_new = jnp.maximum(m_prev, jnp.max(s, axis=1, keepdims=True))
        alpha = jnp.exp2(m_prev - m_new)
        p = jnp.exp2(s - jnp.concatenate([m_new] * (tk // LANE), axis=1))
        acc_sc[h] = alpha * acc_sc[h] + _dot(p.astype(BF16), v_ref[0, :, sl])
        m_sc[h] = m_new

    @pl.when(kv == pl.num_programs(2) - 1)
    def _():
        for h in range(D_HEADS):
            acc = acc_sc[h]
            o_ref[0, :, h * D_HEAD_PAD:(h + 1) * D_HEAD_PAD] = (acc * (1.0 / acc[:, D_V:D_V + 1])).astype(BF16)


def _mla_attention(q, k, v):
    B, S, _ = q.shape
    tq = min(ATT_TQ, S)
    tk = min(ATT_TK, S)
    return pl.pallas_call(
        functools.partial(_mla_kernel, tk=tk),
        grid=(B, S // tq, S // tk),
        in_specs=[
            pl.BlockSpec((1, tq, D_PAD_W), lambda b, i, j: (b, i, 0)),
            pl.BlockSpec((1, tk, D_PAD_W), lambda b, i, j: (b, j, 0)),
            pl.BlockSpec((1, tk, D_PAD_W), lambda b, i, j: (b, j, 0)),
        ],
        out_specs=pl.BlockSpec((1, tq, D_PAD_W), lambda b, i, j: (b, i, 0)),
        out_shape=jax.ShapeDtypeStruct((B, S, D_PAD_W), BF16),
        scratch_shapes=[pltpu.VMEM((D_HEADS, tq, LANE), F32), pltpu.VMEM((D_HEADS, tq, D_HEAD_PAD), F32)],
        compiler_params=_cparams("parallel", "parallel", "arbitrary"),
        name="mla_attention",
    )(q, k, v)


def _merge_kernel(x_ref, mod_ref, g1_ref, ya_ref, ob0_ref, ob1_ref, ob2_ref, l0_ref, l1_ref, l2_ref,
                  cu_ref, cvn_ref, yd_ref, wg_ref, pa_ref, pb_ref, pc_ref, pd_ref, wo_ref,
                  ws_ref, bs_ref, o_ref, nat_sc, *, tm):
    x = x_ref[0]
    h = _modulated_norm(x, g1_ref[...], mod_ref[0, 0:1, :], mod_ref[0, 1:2, :])
    hb = h.astype(BF16)

    def gate(i):
        return jax.nn.sigmoid(_dot(hb, wg_ref[:, i * D_MODEL:(i + 1) * D_MODEL]))

    merged = gate(0) * _dot(ya_ref[0], pa_ref[...])

    for k, (o_ref_g, l_ref_g, dil) in enumerate(((ob1_ref, l1_ref, B_PAIRS[1][1]), (ob2_ref, l2_ref, B_PAIRS[2][1]))):
        for r in range(dil):
            rows = pl.ds(r, tm // dil, stride=dil)
            nat_sc.at[2 * k][rows, :] = o_ref_g[0, r].astype(F32)
            nat_sc.at[2 * k + 1][rows, :] = l_ref_g[0, r]
    l0, l1, l2 = l0_ref[0], nat_sc[1], nat_sc[3]
    mx = jnp.maximum(jnp.maximum(l0, l1), l2)
    e0, e1, e2 = jnp.exp(l0 - mx), jnp.exp(l1 - mx), jnp.exp(l2 - mx)
    yb = (e0 * ob0_ref[0].astype(F32) + e1 * nat_sc[0] + e2 * nat_sc[2]) / (e0 + e1 + e2)
    merged = merged + gate(1) * _dot(yb.astype(BF16), pb_ref[...])

    head = lax.broadcasted_iota(jnp.int32, (C_CHUNK, C_W), 1) // C_HEAD_W
    chunks = []
    for c in range(tm // C_CHUNK):
        rs = slice(c * C_CHUNK, (c + 1) * C_CHUNK)
        mm = _dot(ws_ref[...], cvn_ref[0, rs, :])
        mixed = bs_ref[...]
        for hh in range(C_HEADS):
            mixed = mixed + jnp.where(head == hh, mm[hh * C_CHUNK:(hh + 1) * C_CHUNK], 0.0)
        chunks.append((cu_ref[0, rs, :].astype(F32) * mixed).astype(BF16))
    yc = jnp.concatenate(chunks, axis=0)
    merged = merged + gate(2) * _dot(yc, pc_ref[...])

    merged = merged + gate(3) * _dot(yd_ref[0], pd_ref[...])
    o_ref[0] = x + mod_ref[0, 2:3, :] * _dot(merged.astype(BF16), wo_ref[...])


def _merge(x, mod, lw, ya, band, cu, cvn, yd):
    B, S, _ = x.shape
    tm = ROW_TILE
    row = lambda w: pl.BlockSpec((1, tm, w), lambda b, i: (b, i, 0))
    weights = [lw["wg"], lw["pa"], lw["pb"], lw["pc"], lw["pd"], lw["wo"], lw["ws"], lw["bs"]]
    (ob0, l0), (ob1, l1), (ob2, l2) = band
    res = lambda dil: pl.BlockSpec((1, dil, tm // dil, B_GROUP_W), lambda b, i: (b, 0, i, 0))
    r1, r2 = res(B_PAIRS[1][1]), res(B_PAIRS[2][1])
    return pl.pallas_call(
        functools.partial(_merge_kernel, tm=tm),
        grid=(B, S // tm),
        in_specs=[row(D_MODEL), pl.BlockSpec((1, N_MOD, D_MODEL), lambda b, i: (b, 0, 0)), _resident(lw["g1"].shape),
                  row(A_W), row(B_GROUP_W), r1, r2, row(B_GROUP_W), r1, r2, row(C_W), row(C_W), row(D_PAD_W)]
                 + [_resident(w.shape) for w in weights],
        out_specs=row(D_MODEL),
        out_shape=jax.ShapeDtypeStruct((B, S, D_MODEL), F32),
        scratch_shapes=[pltpu.VMEM((4, tm, B_GROUP_W), F32)],
        compiler_params=_cparams("parallel", "parallel"),
        name="merge",
    )(x, mod, lw["g1"], ya, ob0, ob1, ob2, l0, l1, l2, cu, cvn, yd, *weights)


def _mlp_kernel(x_ref, mod_ref, g2_ref, w1_ref, w2_ref, gf_ref, o_ref, *, final):
    x = x_ref[0]
    h = _modulated_norm(x, g2_ref[...], mod_ref[0, 3:4, :], mod_ref[0, 4:5, :])
    a = jnp.maximum(_dot(h.astype(BF16), w1_ref[...]), 0.0)
    y = x + mod_ref[0, 5:6, :] * _dot((a * a).astype(BF16), w2_ref[...])
    if final:
        ms = jnp.mean(y * y, axis=-1, keepdims=True)
        y = y * lax.rsqrt(ms + EPS) * gf_ref[...]
    o_ref[0] = y


def _mlp(x, mod, lw, final_g, final):
    B, S, _ = x.shape
    tm = ROW_TILE
    row = pl.BlockSpec((1, tm, D_MODEL), lambda b, i: (b, i, 0))
    return pl.pallas_call(
        functools.partial(_mlp_kernel, final=final),
        grid=(B, S // tm),
        in_specs=[row, pl.BlockSpec((1, N_MOD, D_MODEL), lambda b, i: (b, 0, 0)), _resident(lw["g2"].shape),
                  _resident(lw["w1"].shape), _resident(lw["w2"].shape), _resident(final_g.shape)],
        out_specs=row,
        out_shape=jax.ShapeDtypeStruct((B, S, D_MODEL), F32),
        compiler_params=_cparams("parallel", "parallel"),
        name="mlp",
    )(x, mod, lw["g2"], lw["w1"], lw["w2"], final_g)


def _head_slots(w, per_head, take):
    k = w.shape[0]
    wh = w.reshape(k, D_HEADS, per_head)[:, :, take]
    pad = D_HEAD_PAD - wh.shape[-1]
    return jnp.pad(wh, ((0, 0), (0, 0), (0, pad))).reshape(k, D_PAD_W)


def _rope_rot_cols(w):
    half = D_ROPE // 2
    return jnp.concatenate([-w[..., half:], w[..., :half]], axis=-1)


def _prepare_layer(l, w_in, wfft, norm1_g, q_norm_g, kv_norm_g, w_uq, w_ukv, sgu_ln_g, sgu_ln_b, sgu_w, sgu_b,
                   p_a, p_b, p_c, p_d, w_o, norm2_g, w1, w2):
    wi = w_in[l]
    o = A_W
    wb = wi[:, o:o + 3 * B_QKV_W]; o += 3 * B_QKV_W
    qkv_scale = jnp.asarray([B_HEAD_DIM ** -0.5, 1.0, 1.0], F32)[None, None, :, None]
    wb = (wb.reshape(D_MODEL, 3, len(B_PAIRS), B_GROUP_W).transpose(0, 2, 1, 3) * qkv_scale).reshape(D_MODEL, 3 * B_QKV_W)
    wc = wi[:, o:o + 2 * C_W]; o += 2 * C_W
    wdq = wi[:, o:o + D_Q_LORA]; o += D_Q_LORA
    wdkv = wi[:, o:o + D_KV_LORA]; o += D_KV_LORA
    wkr = wi[:, o:o + D_ROPE]; o += D_ROPE
    wg = wi[:, o:]

    def rope_slot(w):
        return jnp.pad(w, ((0, 0), (D_NOPE, D_HEAD_PAD - D_NOPE - D_ROPE)))

    wkr2 = jnp.concatenate([rope_slot(wkr), rope_slot(_rope_rot_cols(wkr))], axis=1)

    uq = w_uq[l].reshape(D_Q_LORA, D_HEADS, D_NOPE + D_ROPE)
    uq_rot = jnp.concatenate([jnp.zeros_like(uq[..., :D_NOPE]), _rope_rot_cols(uq[..., D_NOPE:])], axis=-1)
    padq = ((0, 0), (0, 0), (0, D_HEAD_PAD - D_NOPE - D_ROPE))
    wq2 = jnp.concatenate([jnp.pad(uq, padq).reshape(D_Q_LORA, D_PAD_W),
                           jnp.pad(uq_rot, padq).reshape(D_Q_LORA, D_PAD_W)], axis=1)

    ukv = jnp.pad(w_ukv[l], ((0, D_KV_PAD - D_KV_LORA), (0, 0)))
    wkv2 = jnp.concatenate([_head_slots(ukv, D_NOPE + D_V, slice(0, D_NOPE)),
                            _head_slots(ukv, D_NOPE + D_V, slice(D_NOPE, D_NOPE + D_V))], axis=1)

    pd_pad = jnp.pad(p_d[l].reshape(D_HEADS, D_V, D_MODEL), ((0, 0), (0, D_HEAD_PAD - D_V), (0, 0)))
    bias_sgu = jnp.repeat(sgu_b[l].T, C_HEAD_W, axis=1)
    row = lambda v: v.reshape(1, -1).astype(F32)
    return {
        "g1": row(norm1_g[l]), "g2": row(norm2_g[l]),
        "wfft": wfft[l], "wb": wb.astype(BF16), "wc": wc.astype(BF16), "wdq": wdq.astype(BF16),
        "wdkv": jnp.pad(wdkv, ((0, 0), (0, D_KV_PAD - D_KV_LORA))).astype(BF16),
        "wkr": wkr2.astype(BF16), "wg": wg.astype(BF16),
        "qg": row(q_norm_g[l]), "kvg": row(jnp.pad(kv_norm_g[l], (0, D_KV_PAD - D_KV_LORA))),
        "wq2": wq2.astype(BF16), "wkv2": wkv2.astype(BF16),
        "lng": row(sgu_ln_g[l]), "lnb": row(sgu_ln_b[l]),
        "ws": sgu_w[l].reshape(C_HEADS * C_CHUNK, C_CHUNK).astype(BF16), "bs": bias_sgu.astype(F32),
        "pa": p_a[l].astype(BF16), "pb": p_b[l].astype(BF16), "pc": p_c[l].astype(BF16),
        "pd": pd_pad.reshape(D_PAD_W, D_MODEL).astype(BF16), "wo": w_o[l].astype(BF16),
        "w1": w1[l].astype(BF16), "w2": w2[l].astype(BF16),
    }


def _rope_tables(S):
    half = D_ROPE // 2
    inv = ROPE_THETA ** (-jnp.arange(half, dtype=F32) / half)
    ang = jnp.arange(S, dtype=F32)[:, None] * inv[None, :]
    cos = jnp.cos(ang)
    sin = jnp.sin(ang)
    pad = jnp.zeros((S, D_HEAD_PAD - D_NOPE - D_ROPE), F32)
    cos_t = jnp.concatenate([jnp.ones((S, D_NOPE), F32), cos, cos, pad], axis=1)
    sin_t = jnp.concatenate([jnp.zeros((S, D_NOPE), F32), sin, sin, pad], axis=1)
    return cos_t, sin_t


def _run_group(x, mods, layers, bias_tiles_fn, final_g):
    B, S, _ = x.shape
    cos_t, sin_t = _rope_tables(S)
    tables = _dft_tables(S)
    bias_tiles = bias_tiles_fn(S)
    ones_np = np.zeros((D_HEADS, D_HEAD_PAD), np.float32)
    ones_np[:, D_V] = 1.0
    ones_row = jnp.asarray(ones_np.reshape(1, D_PAD_W))
    gf = final_g.reshape(1, D_MODEL)
    for l, lw in enumerate(layers):
        mod = mods[l]
        y1, y2, b0, b1, b2, cu, cvn, mq, mk, mv = _inproj(x, mod, lw, cos_t, sin_t, ones_row)
        ya = _fourier_seq(tables, y1, y2)
        band = _dilated_branch(bias_tiles, b0, b1, b2)
        yd = _mla_attention(mq, mk, mv)
        x = _merge(x, mod, lw, ya, band, cu, cvn, yd)
        x = _mlp(x, mod, lw, gf, final=(l == len(layers) - 1))
    return x


def kernel(x_prompt, x_sample, c_prompt, c_sample, rel_bias, ada_w, ada_b, norm1_g, w_in, mla_q_norm_g, mla_kv_norm_g, mla_w_uq, mla_w_ukv, sgu_ln_g, sgu_ln_b, sgu_w, sgu_b, p_a, p_b, p_c, p_d, w_o, norm2_g, mlp_w1, mlp_w2, final_g):
    nb_p = c_prompt.shape[0]
    nb_s = c_sample.shape[0]
    rows = -(-(nb_p + nb_s) // 8) * 8
    c_all = jnp.concatenate([c_prompt, c_sample, jnp.zeros((rows - nb_p - nb_s, D_MODEL), F32)], axis=0)
    mod_all = _ada_mod(c_all, ada_w, ada_b).reshape(DEPTH, rows, N_MOD, D_MODEL)
    mods_p = mod_all[:, :nb_p]
    mods_s = mod_all[:, nb_p:nb_p + nb_s]

    wfft = _fold_channel_dft(w_in[:, :, :A_W])
    layers = [_prepare_layer(l, w_in, wfft, norm1_g, mla_q_norm_g, mla_kv_norm_g, mla_w_uq, mla_w_ukv,
                             sgu_ln_g, sgu_ln_b, sgu_w, sgu_b, p_a, p_b, p_c, p_d, w_o, norm2_g, mlp_w1, mlp_w2)
              for l in range(DEPTH)]

    tile_cache = {}

    def bias_tiles_fn(S):
        out = []
        for g, (_, dil) in enumerate(B_PAIRS):
            key = (g, min(2 * B_QBLK, S // dil))
            if key not in tile_cache:
                tile_cache[key] = _band_bias_tiles(rel_bias[:, 2 * g:2 * g + 2], dil, S // dil)
            out.append(tile_cache[key])
        return out

    y_prompt = _run_group(x_prompt, mods_p, layers, bias_tiles_fn, final_g)
    y_sample = _run_group(x_sample, mods_s, layers, bias_tiles_fn, final_g)
    return (y_prompt, y_sample)
```

```python
import functools
import math

import jax
import jax.numpy as jnp
import numpy as np
from jax import lax
from jax.experimental import pallas as pl
from jax.experimental.pallas import tpu as pltpu

F32 = jnp.float32
BF16 = jnp.bfloat16
HIGHEST = lax.Precision.HIGHEST

D_MODEL = 1024
DEPTH = 4
EPS = 1e-6
NEG_INF = -1e30
N_MOD = 6
A_GROUPS = 4
A_GROUP_W = 192
A_W = A_GROUPS * A_GROUP_W
B_PAIRS = ((128, 1), (512, 4), (2048, 16))
B_HEAD_DIM = 64
B_QKV_W = 384
B_GROUP_W = 128
B_HALF = 64
B_QBLK = 128
REL_BUCKETS = 32
REL_MAX_DIST = 1024
C_HEADS = 4
C_HEAD_W = 96
C_W = 384
C_CHUNK = 128
D_HEADS = 4
D_Q_LORA = 384
D_KV_LORA = 320
D_KV_PAD = 384
D_NOPE = 64
D_ROPE = 32
D_V = 64
D_HEAD_PAD = 128
D_PAD_W = D_HEADS * D_HEAD_PAD
ROPE_THETA = 10000.0
LOG2_E = math.log2(math.e)
D_FF = 4 * D_MODEL
MIX_W = A_W + 3 * B_QKV_W + 2 * C_W + D_Q_LORA + D_KV_LORA + D_ROPE

VMEM_LIMIT_BYTES = 56 * 1024 * 1024
LANE = 128
INPROJ_TILE = 1024
MERGE_TILE = 1024
MLP_TILE = 1024
DFT_N2 = 128
DFT_KRON_COLS = 64
DFT_PITCH_PAD = 8
DFT_COPY_UNROLL = 8
DFT_MATMUL_UNROLL = 4
DFT_SUB = 16
ATT_TQ = 2048
ATT_TK = 2048
BAND_ROWS = 4096
BAND_UNROLL = 16


def _cparams(*sem):
    return pltpu.CompilerParams(dimension_semantics=sem, vmem_limit_bytes=VMEM_LIMIT_BYTES)


def _resident(shape):
    nd = len(shape)
    return pl.BlockSpec(shape, lambda *_: (0,) * nd, pipeline_mode=pl.Buffered(1))


def _dot(a, b):
    return jnp.dot(a, b, preferred_element_type=F32)


def _dot_nt(a, b):
    return lax.dot_general(a, b, (((1,), (1,)), ((), ())), preferred_element_type=F32)


def _ada_kernel(c_ref, w_ref, b_ref, o_ref):
    c = c_ref[...]
    a = c * jax.nn.sigmoid(c)
    o_ref[0] = jnp.dot(a, w_ref[0], preferred_element_type=F32, precision=HIGHEST) + b_ref[0]


def _ada_mod(c_all, ada_w, ada_b):
    rows = c_all.shape[0]
    tn = 1536
    return pl.pallas_call(
        _ada_kernel,
        grid=(DEPTH, N_MOD * D_MODEL // tn),
        in_specs=[
            pl.BlockSpec((rows, D_MODEL), lambda l, j: (0, 0)),
            pl.BlockSpec((1, D_MODEL, tn), lambda l, j: (l, 0, j)),
            pl.BlockSpec((1, 1, tn), lambda l, j: (l, 0, j)),
        ],
        out_specs=pl.BlockSpec((1, rows, tn), lambda l, j: (l, 0, j)),
        out_shape=jax.ShapeDtypeStruct((DEPTH, rows, N_MOD * D_MODEL), F32),
        compiler_params=_cparams("parallel", "parallel"),
        name="ada_mod",
    )(c_all, ada_w, ada_b.reshape(DEPTH, 1, N_MOD * D_MODEL))


def _fold_kernel(w_ref, m_ref, o_ref):
    o_ref[0] = jnp.dot(w_ref[0], m_ref[...], preferred_element_type=F32, precision=HIGHEST).astype(BF16)


def _fold_channel_dft(w_a):
    n = np.arange(A_GROUP_W)
    idx = (n[:, None] * n[None, :]) % A_GROUP_W
    ang = 2.0 * np.pi * idx / A_GROUP_W
    eye = np.eye(A_GROUPS)
    m = np.concatenate([np.kron(eye, np.cos(ang)), np.kron(eye, np.sin(ang))], axis=1).astype(np.float32)
    return pl.pallas_call(
        _fold_kernel,
        grid=(DEPTH,),
        in_specs=[
            pl.BlockSpec((1, D_MODEL, A_W), lambda l: (l, 0, 0)),
            pl.BlockSpec((A_W, 2 * A_W), lambda l: (0, 0)),
        ],
        out_specs=pl.BlockSpec((1, D_MODEL, 2 * A_W), lambda l: (l, 0, 0)),
        out_shape=jax.ShapeDtypeStruct((DEPTH, D_MODEL, 2 * A_W), BF16),
        compiler_params=_cparams("parallel"),
        name="fold_channel_dft",
    )(w_a, jnp.asarray(m))


def _modulated_norm(x, g, shift, scale):
    ms = jnp.mean(x * x, axis=-1, keepdims=True)
    return (x * lax.rsqrt(ms + EPS) * g) * (1.0 + scale) + shift


def _inproj_kernel(x_ref, mod_ref, g1_ref, cos_ref, sin_ref,
                   wfft_ref, wb_ref, wc_ref, wdq_ref, wdkv_ref, wkr_ref,
                   qg_ref, kvg_ref, wq2_ref, wkv2_ref, lng_ref, lnb_ref, ones_ref,
                   y1_ref, y2_ref, b0_ref, b1_ref, b2_ref, cu_ref, cvn_ref, mq_ref, mk_ref, mv_ref, res_sc, *, tm):
    x = x_ref[0]
    h = _modulated_norm(x, g1_ref[...], mod_ref[0, 0:1, :], mod_ref[0, 1:2, :])
    hb = h.astype(BF16)

    y = _dot(hb, wfft_ref[...])
    y1_ref[0] = y[:, :A_W].astype(BF16)
    y2_ref[0] = y[:, A_W:].astype(BF16)

    zb = _dot(hb, wb_ref[...])
    b0_ref[0] = zb[:, :B_QKV_W].astype(BF16)
    nres = 2 * B_QKV_W // LANE
    for c in range(nres):
        res_sc[c] = zb[:, B_QKV_W + c * LANE:B_QKV_W + (c + 1) * LANE]
    for g, out_ref in ((1, b1_ref), (2, b2_ref)):
        dil = B_PAIRS[g][1]
        for r in range(dil):
            for c in range(3):
                out_ref[0, r, :, c * LANE:(c + 1) * LANE] = (
                    res_sc.at[3 * (g - 1) + c][pl.ds(r, tm // dil, stride=dil), :].astype(BF16))

    zc = _dot(hb, wc_ref[...])
    cu_ref[0] = zc[:, :C_W].astype(BF16)
    cv = zc[:, C_W:]
    mu = jnp.mean(cv, axis=-1, keepdims=True)
    cc = cv - mu
    var = jnp.mean(cc * cc, axis=-1, keepdims=True)
    cvn_ref[0] = (cc * lax.rsqrt(var + EPS) * lng_ref[...] + lnb_ref[...]).astype(BF16)

    cos = cos_ref[...]
    sin = sin_ref[...]
    cos4 = jnp.concatenate([cos] * D_HEADS, axis=1)
    sin4 = jnp.concatenate([sin] * D_HEADS, axis=1)

    dcq = _dot(hb, wdq_ref[...])
    msq = jnp.mean(dcq * dcq, axis=-1, keepdims=True)
    qn = (dcq * lax.rsqrt(msq + EPS) * qg_ref[...]).astype(BF16)
    qq = _dot(qn, wq2_ref[...])
    q = (qq[:, :D_PAD_W] * cos4 + qq[:, D_PAD_W:] * sin4) * ((D_NOPE + D_ROPE) ** -0.5 * LOG2_E)
    mq_ref[0] = q.astype(BF16)

    dckv = _dot(hb, wdkv_ref[...])
    mskv = jnp.sum(dckv * dckv, axis=-1, keepdims=True) * (1.0 / D_KV_LORA)
    kvn = (dckv * lax.rsqrt(mskv + EPS) * kvg_ref[...]).astype(BF16)
    kv = _dot(kvn, wkv2_ref[...])
    kr = _dot(hb, wkr_ref[...])
    kpe = kr[:, :D_HEAD_PAD] * cos + kr[:, D_HEAD_PAD:] * sin
    mk_ref[0] = (kv[:, :D_PAD_W] + jnp.concatenate([kpe] * D_HEADS, axis=1)).astype(BF16)
    mv_ref[0] = (kv[:, D_PAD_W:] + ones_ref[...]).astype(BF16)


def _inproj(x, mod, lw, cos_t, sin_t, ones_row):
    B, S, _ = x.shape
    tm = INPROJ_TILE
    row = lambda w: pl.BlockSpec((1, tm, w), lambda b, i: (b, i, 0))
    tab = pl.BlockSpec((tm, D_HEAD_PAD), lambda b, i: (i, 0))
    weights = [lw["wfft"], lw["wb"], lw["wc"], lw["wdq"], lw["wdkv"], lw["wkr"],
               lw["qg"], lw["kvg"], lw["wq2"], lw["wkv2"], lw["lng"], lw["lnb"], ones_row]

    def nat(w):
        return row(w), jax.ShapeDtypeStruct((B, S, w), BF16)

    def residue(dil):
        return (pl.BlockSpec((1, dil, tm // dil, B_QKV_W), lambda b, i: (b, 0, i, 0)),
                jax.ShapeDtypeStruct((B, dil, S // dil, B_QKV_W), BF16))

    outs = [nat(A_W), nat(A_W), nat(B_QKV_W), residue(B_PAIRS[1][1]), residue(B_PAIRS[2][1]),
            nat(C_W), nat(C_W), nat(D_PAD_W), nat(D_PAD_W), nat(D_PAD_W)]
    return pl.pallas_call(
        functools.partial(_inproj_kernel, tm=tm),
        grid=(B, S // tm),
        in_specs=[row(D_MODEL),
                  pl.BlockSpec((1, N_MOD, D_MODEL), lambda b, i: (b, 0, 0)),
                  _resident(lw["g1"].shape), tab, tab] + [_resident(w.shape) for w in weights],
        out_specs=[o[0] for o in outs],
        out_shape=[o[1] for o in outs],
        scratch_shapes=[pltpu.VMEM((2 * B_QKV_W // LANE, tm, LANE), F32)],
        compiler_params=_cparams("parallel", "parallel"),
        name="inproj",
    )(x, mod, lw["g1"], cos_t, sin_t, *weights)


def _fft_stage1_kron_kernel(m_ref, y1_ref, y2_ref, ar_ref, ai_ref, *, n1, nsub):
    rows = n1 * DFT_SUB
    for s in range(nsub):
        js = slice(s * DFT_SUB, (s + 1) * DFT_SUB)
        rhs = jnp.concatenate([y1_ref[0, :, js, :].reshape(rows, A_W), y2_ref[0, :, js, :].reshape(rows, A_W)], axis=0)
        out = _dot(m_ref[...], rhs).astype(BF16)
        ar_ref[0, :, js, :] = out[:rows].reshape(n1, DFT_SUB, A_W)
        ai_ref[0, :, js, :] = out[rows:].reshape(n1, DFT_SUB, A_W)


def _fft_stage1_kernel(m_ref, y1_ref, y2_ref, ar_ref, ai_ref, sc, *, n1):
    nl = A_W // LANE
    pitch = n1 + DFT_PITCH_PAD

    def fill(i, carry):
        rows = pl.ds(i, DFT_SUB, stride=pitch)
        for c in range(nl):
            sl = slice(c * LANE, (c + 1) * LANE)
            sc.at[c][rows, :] = y1_ref[0, i, :, sl].astype(F32)
            sc.at[nl + c][rows, :] = y2_ref[0, i, :, sl].astype(F32)
        return carry

    lax.fori_loop(0, n1, fill, 0, unroll=DFT_COPY_UNROLL)

    def body(j, carry):
        rows = pl.ds(pl.multiple_of(j * pitch, 8), n1)
        top = jnp.concatenate([sc.at[c][rows, :] for c in range(nl)], axis=1)
        bot = jnp.concatenate([sc.at[nl + c][rows, :] for c in range(nl)], axis=1)
        out = _dot(m_ref[...], jnp.concatenate([top, bot], axis=0).astype(BF16))
        for c in range(nl):
            sc.at[c][rows, :] = out[:n1, c * LANE:(c + 1) * LANE]
            sc.at[nl + c][rows, :] = out[n1:, c * LANE:(c + 1) * LANE]
        return carry

    lax.fori_loop(0, DFT_SUB, body, 0, unroll=DFT_MATMUL_UNROLL)

    def drain(i, carry):
        rows = pl.ds(i, DFT_SUB, stride=pitch)
        for c in range(nl):
            sl = slice(c * LANE, (c + 1) * LANE)
            ar_ref[0, i, :, sl] = sc.at[c][rows, :].astype(BF16)
            ai_ref[0, i, :, sl] = sc.at[nl + c][rows, :].astype(BF16)
        return carry

    lax.fori_loop(0, n1, drain, 0, unroll=DFT_COPY_UNROLL)


def _fft_stage2_kernel(h_ref, ar_ref, ai_ref, o_ref, sc, *, n2):
    nl = A_W // LANE
    pitch = n2 + DFT_PITCH_PAD

    def body(j, carry):
        rhs = jnp.concatenate([ar_ref[0, j], ai_ref[0, j]], axis=0)
        out = _dot(h_ref[j], rhs)
        rows = pl.ds(pl.multiple_of(j * pitch, 8), n2)
        for c in range(nl):
            sc.at[c][rows, :] = out[:, c * LANE:(c + 1) * LANE]
        return carry

    lax.fori_loop(0, DFT_SUB, body, 0, unroll=DFT_MATMUL_UNROLL)

    def drain(k2, carry):
        rows = pl.ds(k2, DFT_SUB, stride=pitch)
        for c in range(nl):
            o_ref[0, k2, :, c * LANE:(c + 1) * LANE] = sc.at[c][rows, :].astype(BF16)
        return carry

    lax.fori_loop(0, n2, drain, 0, unroll=DFT_COPY_UNROLL)


def _fft_two_stage(m1, h2, y1, y2):
    B, S, _ = y1.shape
    n2 = DFT_N2
    n1 = S // n2
    nl = A_W // LANE
    if n1 == DFT_SUB:
        jb = min(n2, DFT_KRON_COLS)
        stage1 = functools.partial(_fft_stage1_kron_kernel, n1=n1, nsub=jb // DFT_SUB)
        scratch = []
    else:
        jb = DFT_SUB
        stage1 = functools.partial(_fft_stage1_kernel, n1=n1)
        scratch = [pltpu.VMEM((2 * nl, DFT_SUB * (n1 + DFT_PITCH_PAD), LANE), F32)]
    blk = pl.BlockSpec((1, n1, jb, A_W), lambda b, j: (b, 0, j, 0))
    ar, ai = pl.pallas_call(
        stage1,
        grid=(B, n2 // jb),
        in_specs=[_resident(m1.shape), blk, blk],
        out_specs=[blk, blk],
        out_shape=[jax.ShapeDtypeStruct((B, n1, n2, A_W), BF16)] * 2,
        scratch_shapes=scratch,
        compiler_params=_cparams("parallel", "parallel"),
        name="fft_stage1",
    )(m1, y1.reshape(B, n1, n2, A_W), y2.reshape(B, n1, n2, A_W))
    slab = pl.BlockSpec((1, DFT_SUB, n2, A_W), lambda b, j: (b, j, 0, 0))
    out = pl.pallas_call(
        functools.partial(_fft_stage2_kernel, n2=n2),
        grid=(B, n1 // DFT_SUB),
        in_specs=[pl.BlockSpec((DFT_SUB, n2, 2 * n2), lambda b, j: (j, 0, 0)), slab, slab],
        out_specs=pl.BlockSpec((1, n2, DFT_SUB, A_W), lambda b, j: (b, 0, j, 0)),
        out_shape=jax.ShapeDtypeStruct((B, n2, n1, A_W), BF16),
        scratch_shapes=[pltpu.VMEM((nl, DFT_SUB * (n2 + DFT_PITCH_PAD), LANE), F32)],
        compiler_params=_cparams("parallel", "parallel"),
        name="fft_stage2",
    )(h2, ar, ai)
    return out.reshape(B, S, A_W)


def _dft_tables(S):
    scale = 1.0 / math.sqrt(S * A_GROUP_W)

    def cs(idx, period):
        ang = idx.astype(F32) * (2.0 * math.pi / period)
        return jnp.cos(ang), jnp.sin(ang)

    n2 = DFT_N2
    n1 = S // n2
    a = jnp.arange(n1, dtype=jnp.int32)
    c1, s1 = cs((a[:, None] * a[None, :]) % n1, n1)
    if n1 == DFT_SUB:
        eye = jnp.eye(DFT_SUB, dtype=F32)
        c1, s1 = jnp.kron(c1, eye), jnp.kron(s1, eye)
    m1 = jnp.concatenate([jnp.concatenate([c1, -s1], axis=1), jnp.concatenate([s1, c1], axis=1)], axis=0)
    k1 = a[:, None, None]
    k2 = jnp.arange(n2, dtype=jnp.int32)[None, :, None]
    nn = jnp.arange(n2, dtype=jnp.int32)[None, None, :]
    hc, hs = cs((nn * (k1 + n1 * k2)) % S, S)
    h2 = jnp.concatenate([hc * scale, -hs * scale], axis=2)
    return {"m1": m1.astype(BF16), "h2": h2.astype(BF16)}


def _fourier_seq(tables, y1, y2):
    return _fft_two_stage(tables["m1"], tables["h2"], y1, y2)


def _band_kernel(bias_ref, q_ref, k_ref, v_ref, o_ref, lse_ref, *, seq, rows, nseq, kwin):
    nsub = rows // B_QBLK
    last_blk = seq // B_QBLK - 1
    j = pl.program_id(1)
    head0 = lax.broadcasted_iota(jnp.int32, (B_QBLK, B_GROUP_W), 1) < B_HEAD_DIM

    def body(t, carry):
        si = t // nsub
        b = t % nsub
        r0 = pl.multiple_of(b * B_QBLK, B_QBLK)
        gb = j * nsub + b
        ws = pl.multiple_of(jnp.clip(gb * B_QBLK - B_HALF, 0, seq - kwin), B_HALF)
        tix = jnp.where(gb == 0, 0, jnp.where(gb == last_blk, 2, 1))
        q = q_ref[si, pl.ds(r0, B_QBLK), :]
        kw = k_ref[si, pl.ds(ws, kwin), :]
        vw = v_ref[si, pl.ds(ws, kwin), :]
        zero = jnp.zeros_like(q)
        q2 = jnp.concatenate([jnp.where(head0, q, zero), jnp.where(head0, zero, q)], axis=0)
        s = _dot_nt(q2, kw) + bias_ref[tix]
        m = jnp.max(s, axis=1, keepdims=True)
        p = jnp.exp(s - m)
        l = jnp.sum(p, axis=1, keepdims=True)
        o2 = _dot(p.astype(BF16), vw) * (1.0 / l)
        lse2 = jnp.broadcast_to(m + jnp.log(l), (2 * B_QBLK, B_GROUP_W))
        o_ref[si, pl.ds(r0, B_QBLK), :] = jnp.where(head0, o2[:B_QBLK], o2[B_QBLK:]).astype(BF16)
        lse_ref[si, pl.ds(r0, B_QBLK), :] = jnp.where(head0, lse2[:B_QBLK], lse2[B_QBLK:])
        return carry

    lax.fori_loop(0, nseq * nsub, body, 0, unroll=BAND_UNROLL)


def _band_attention(bias, arr, cols):
    N, L, _ = arr.shape
    kwin = bias.shape[-1]
    if L >= BAND_ROWS:
        nseq, rows = 1, BAND_ROWS
    else:
        nseq, rows = min(BAND_ROWS // L, N), L
    assert N % nseq == 0 and (nseq * rows // B_QBLK) % BAND_UNROLL == 0
    cq, ck, cv = cols
    out_spec = pl.BlockSpec((nseq, rows, B_GROUP_W), lambda n, j: (n, j, 0))
    return pl.pallas_call(
        functools.partial(_band_kernel, seq=L, rows=rows, nseq=nseq, kwin=kwin),
        grid=(N // nseq, L // rows),
        in_specs=[
            _resident(bias.shape),
            pl.BlockSpec((nseq, rows, B_GROUP_W), lambda n, j: (n, j, cq)),
            pl.BlockSpec((nseq, L, B_GROUP_W), lambda n, j: (n, 0, ck)),
            pl.BlockSpec((nseq, L, B_GROUP_W), lambda n, j: (n, 0, cv)),
        ],
        out_specs=[out_spec, out_spec],
        out_shape=[jax.ShapeDtypeStruct((N, L, B_GROUP_W), BF16), jax.ShapeDtypeStruct((N, L, B_GROUP_W), F32)],
        compiler_params=_cparams("parallel", "arbitrary"),
        name="band_attention",
    )(bias, arr, arr, arr)


def _t5_bucket(rel):
    half = REL_BUCKETS // 2
    max_exact = half // 2
    ret = jnp.where(rel > 0, half, 0)
    n = jnp.abs(rel)
    large = max_exact + (jnp.log(jnp.maximum(n, max_exact).astype(F32) / max_exact)
                         / math.log(REL_MAX_DIST / max_exact) * (half - max_exact)).astype(jnp.int32)
    large = jnp.minimum(large, half - 1)
    return ret + jnp.where(n < max_exact, n, large)


def _band_bias_tiles(rel_bias_g, dil, L):
    kwin = min(2 * B_QBLK, L)
    i = np.arange(B_QBLK)[:, None]
    jj = np.arange(kwin)[None, :]
    tiles = []
    for off in (0, -B_HALF, B_QBLK - kwin):
        rel = jj + off - i
        band = np.abs(rel) <= B_HALF
        bucket = _t5_bucket(jnp.asarray(rel * dil, dtype=jnp.int32))
        onehot = (bucket[..., None] == jnp.arange(REL_BUCKETS, dtype=jnp.int32)).astype(F32)
        vals = jnp.einsum("ijb,bh->hij", onehot, rel_bias_g.astype(F32), precision=HIGHEST)
        tiles.append(jnp.where(jnp.asarray(band)[None], vals, NEG_INF))
    return jnp.stack(tiles).reshape(3, 2 * B_QBLK, kwin)


def _dilated_branch(bias_tiles, b0, b1, b2):
    outs = []
    for g, arr in enumerate((b0, b1, b2)):
        lead = arr.shape[:-2]
        L = arr.shape[-2]
        o, lse = _band_attention(bias_tiles[g], arr.reshape(-1, L, B_QKV_W), (0, 1, 2))
        outs.append((o.reshape(*lead, L, B_GROUP_W), lse.reshape(*lead, L, B_GROUP_W)))
    return outs


def _mla_kernel(q_ref, k_ref, v_ref, o_ref, m_sc, acc_sc, *, tk):
    kv = pl.program_id(2)

    @pl.when(kv == 0)
    def _():
        m_sc[...] = jnp.full(m_sc.shape, -jnp.inf, F32)
        acc_sc[...] = jnp.zeros(acc_sc.shape, F32)

    for h in range(D_HEADS):
        sl = slice(h * D_HEAD_PAD, (h + 1) * D_HEAD_PAD)
        s = _dot_nt(q_ref[0, :, sl], k_ref[0, :, sl])
        m_prev = m_sc[h]
        m_new = jnp.maximum(m_prev, jnp.max(s, axis=1, keepdims=True))
        alpha = jnp.exp2(m_prev - m_new)
        p = jnp.exp2(s - jnp.concatenate([m_new] * (tk // LANE), axis=1))
        acc_sc[h] = alpha * acc_sc[h] + _dot(p.astype(BF16), v_ref[0, :, sl])
        m_sc[h] = m_new

    @pl.when(kv == pl.num_programs(2) - 1)
    def _():
        for h in range(D_HEADS):
            acc = acc_sc[h]
            o_ref[0, :, h * D_HEAD_PAD:(h + 1) * D_HEAD_PAD] = (acc * (1.0 / acc[:, D_V:D_V + 1])).astype(BF16)


def _mla_attention(q, k, v):
    B, S, _ = q.shape
    tq = min(ATT_TQ, S)
    tk = min(ATT_TK, S)
    return pl.pallas_call(
        functools.partial(_mla_kernel, tk=tk),
        grid=(B, S // tq, S // tk),
        in_specs=[
            pl.BlockSpec((1, tq, D_PAD_W), lambda b, i, j: (b, i, 0)),
            pl.BlockSpec((1, tk, D_PAD_W), lambda b, i, j: (b, j, 0)),
            pl.BlockSpec((1, tk, D_PAD_W), lambda b, i, j: (b, j, 0)),
        ],
        out_specs=pl.BlockSpec((1, tq, D_PAD_W), lambda b, i, j: (b, i, 0)),
        out_shape=jax.ShapeDtypeStruct((B, S, D_PAD_W), BF16),
        scratch_shapes=[pltpu.VMEM((D_HEADS, tq, LANE), F32), pltpu.VMEM((D_HEADS, tq, D_HEAD_PAD), F32)],
        compiler_params=_cparams("parallel", "parallel", "arbitrary"),
        name="mla_attention",
    )(q, k, v)


def _merge_kernel(x_ref, mod_ref, g1_ref, ya_ref, ob0_ref, ob1_ref, ob2_ref, l0_ref, l1_ref, l2_ref,
                  cu_ref, cvn_ref, yd_ref, wg_ref, pa_ref, pb_ref, pc_ref, pd_ref, wo_ref,
                  ws_ref, bs_ref, o_ref, nat_sc, *, tm):
    x = x_ref[0]
    h = _modulated_norm(x, g1_ref[...], mod_ref[0, 0:1, :], mod_ref[0, 1:2, :])
    hb = h.astype(BF16)

    def gate(i):
        return jax.nn.sigmoid(_dot(hb, wg_ref[:, i * D_MODEL:(i + 1) * D_MODEL]))

    merged = gate(0) * _dot(ya_ref[0], pa_ref[...])

    for k, (o_ref_g, l_ref_g, dil) in enumerate(((ob1_ref, l1_ref, B_PAIRS[1][1]), (ob2_ref, l2_ref, B_PAIRS[2][1]))):
        for r in range(dil):
            rows = pl.ds(r, tm // dil, stride=dil)
            nat_sc.at[2 * k][rows, :] = o_ref_g[0, r].astype(F32)
            nat_sc.at[2 * k + 1][rows, :] = l_ref_g[0, r]
    l0, l1, l2 = l0_ref[0], nat_sc[1], nat_sc[3]
    mx = jnp.maximum(jnp.maximum(l0, l1), l2)
    e0, e1, e2 = jnp.exp(l0 - mx), jnp.exp(l1 - mx), jnp.exp(l2 - mx)
    yb = (e0 * ob0_ref[0].astype(F32) + e1 * nat_sc[0] + e2 * nat_sc[2]) / (e0 + e1 + e2)
    merged = merged + gate(1) * _dot(yb.astype(BF16), pb_ref[...])

    head = lax.broadcasted_iota(jnp.int32, (C_CHUNK, C_W), 1) // C_HEAD_W
    chunks = []
    for c in range(tm // C_CHUNK):
        rs = slice(c * C_CHUNK, (c + 1) * C_CHUNK)
        mm = _dot(ws_ref[...], cvn_ref[0, rs, :])
        mixed = bs_ref[...]
        for hh in range(C_HEADS):
            mixed = mixed + jnp.where(head == hh, mm[hh * C_CHUNK:(hh + 1) * C_CHUNK], 0.0)
        chunks.append((cu_ref[0, rs, :].astype(F32) * mixed).astype(BF16))
    yc = jnp.concatenate(chunks, axis=0)
    merged = merged + gate(2) * _dot(yc, pc_ref[...])

    merged = merged + gate(3) * _dot(yd_ref[0], pd_ref[...])
    o_ref[0] = x + mod_ref[0, 2:3, :] * _dot(merged.astype(BF16), wo_ref[...])


def _merge(x, mod, lw, ya, band, cu, cvn, yd):
    B, S, _ = x.shape
    tm = MERGE_TILE
    row = lambda w: pl.BlockSpec((1, tm, w), lambda b, i: (b, i, 0))
    weights = [lw["wg"], lw["pa"], lw["pb"], lw["pc"], lw["pd"], lw["wo"], lw["ws"], lw["bs"]]
    (ob0, l0), (ob1, l1), (ob2, l2) = band
    res = lambda dil: pl.BlockSpec((1, dil, tm // dil, B_GROUP_W), lambda b, i: (b, 0, i, 0))
    r1, r2 = res(B_PAIRS[1][1]), res(B_PAIRS[2][1])
    return pl.pallas_call(
        functools.partial(_merge_kernel, tm=tm),
        grid=(B, S // tm),
        in_specs=[row(D_MODEL), pl.BlockSpec((1, N_MOD, D_MODEL), lambda b, i: (b, 0, 0)), _resident(lw["g1"].shape),
                  row(A_W), row(B_GROUP_W), r1, r2, row(B_GROUP_W), r1, r2, row(C_W), row(C_W), row(D_PAD_W)]
                 + [_resident(w.shape) for w in weights],
        out_specs=row(D_MODEL),
        out_shape=jax.ShapeDtypeStruct((B, S, D_MODEL), F32),
        scratch_shapes=[pltpu.VMEM((4, tm, B_GROUP_W), F32)],
        compiler_params=_cparams("parallel", "parallel"),
        name="merge",
    )(x, mod, lw["g1"], ya, ob0, ob1, ob2, l0, l1, l2, cu, cvn, yd, *weights)


def _mlp_kernel(x_ref, mod_ref, g2_ref, w1_ref, w2_ref, gf_ref, o_ref, *, final):
    x = x_ref[0]
    h = _modulated_norm(x, g2_ref[...], mod_ref[0, 3:4, :], mod_ref[0, 4:5, :])
    a = jnp.maximum(_dot(h.astype(BF16), w1_ref[...]), 0.0)
    y = x + mod_ref[0, 5:6, :] * _dot((a * a).astype(BF16), w2_ref[...])
    if final:
        ms = jnp.mean(y * y, axis=-1, keepdims=True)
        y = y * lax.rsqrt(ms + EPS) * gf_ref[...]
    o_ref[0] = y


def _mlp(x, mod, lw, final_g, final):
    B, S, _ = x.shape
    tm = MLP_TILE
    row = pl.BlockSpec((1, tm, D_MODEL), lambda b, i: (b, i, 0))
    return pl.pallas_call(
        functools.partial(_mlp_kernel, final=final),
        grid=(B, S // tm),
        in_specs=[row, pl.BlockSpec((1, N_MOD, D_MODEL), lambda b, i: (b, 0, 0)), _resident(lw["g2"].shape),
                  _resident(lw["w1"].shape), _resident(lw["w2"].shape), _resident(final_g.shape)],
        out_specs=row,
        out_shape=jax.ShapeDtypeStruct((B, S, D_MODEL), F32),
        compiler_params=_cparams("parallel", "parallel"),
        name="mlp",
    )(x, mod, lw["g2"], lw["w1"], lw["w2"], final_g)


def _head_slots(w, per_head, take):
    k = w.shape[0]
    wh = w.reshape(k, D_HEADS, per_head)[:, :, take]
    pad = D_HEAD_PAD - wh.shape[-1]
    return jnp.pad(wh, ((0, 0), (0, 0), (0, pad))).reshape(k, D_PAD_W)


def _rope_rot_cols(w):
    half = D_ROPE // 2
    return jnp.concatenate([-w[..., half:], w[..., :half]], axis=-1)


def _prepare_layer(l, w_in, wfft, norm1_g, q_norm_g, kv_norm_g, w_uq, w_ukv, sgu_ln_g, sgu_ln_b, sgu_w, sgu_b,
                   p_a, p_b, p_c, p_d, w_o, norm2_g, w1, w2):
    wi = w_in[l]
    o = A_W
    wb = wi[:, o:o + 3 * B_QKV_W]; o += 3 * B_QKV_W
    qkv_scale = jnp.asarray([B_HEAD_DIM ** -0.5, 1.0, 1.0], F32)[None, None, :, None]
    wb = (wb.reshape(D_MODEL, 3, len(B_PAIRS), B_GROUP_W).transpose(0, 2, 1, 3) * qkv_scale).reshape(D_MODEL, 3 * B_QKV_W)
    wc = wi[:, o:o + 2 * C_W]; o += 2 * C_W
    wdq = wi[:, o:o + D_Q_LORA]; o += D_Q_LORA
    wdkv = wi[:, o:o + D_KV_LORA]; o += D_KV_LORA
    wkr = wi[:, o:o + D_ROPE]; o += D_ROPE
    wg = wi[:, o:]

    def rope_slot(w):
        return jnp.pad(w, ((0, 0), (D_NOPE, D_HEAD_PAD - D_NOPE - D_ROPE)))

    wkr2 = jnp.concatenate([rope_slot(wkr), rope_slot(_rope_rot_cols(wkr))], axis=1)

    uq = w_uq[l].reshape(D_Q_LORA, D_HEADS, D_NOPE + D_ROPE)
    uq_rot = jnp.concatenate([jnp.zeros_like(uq[..., :D_NOPE]), _rope_rot_cols(uq[..., D_NOPE:])], axis=-1)
    padq = ((0, 0), (0, 0), (0, D_HEAD_PAD - D_NOPE - D_ROPE))
    wq2 = jnp.concatenate([jnp.pad(uq, padq).reshape(D_Q_LORA, D_PAD_W),
                           jnp.pad(uq_rot, padq).reshape(D_Q_LORA, D_PAD_W)], axis=1)

    ukv = jnp.pad(w_ukv[l], ((0, D_KV_PAD - D_KV_LORA), (0, 0)))
    wkv2 = jnp.concatenate([_head_slots(ukv, D_NOPE + D_V, slice(0, D_NOPE)),
                            _head_slots(ukv, D_NOPE + D_V, slice(D_NOPE, D_NOPE + D_V))], axis=1)

    pd_pad = jnp.pad(p_d[l].reshape(D_HEADS, D_V, D_MODEL), ((0, 0), (0, D_HEAD_PAD - D_V), (0, 0)))
    bias_sgu = jnp.repeat(sgu_b[l].T, C_HEAD_W, axis=1)
    row = lambda v: v.reshape(1, -1).astype(F32)
    return {
        "g1": row(norm1_g[l]), "g2": row(norm2_g[l]),
        "wfft": wfft[l], "wb": wb.astype(BF16), "wc": wc.astype(BF16), "wdq": wdq.astype(BF16),
        "wdkv": jnp.pad(wdkv, ((0, 0), (0, D_KV_PAD - D_KV_LORA))).astype(BF16),
        "wkr": wkr2.astype(BF16), "wg": wg.astype(BF16),
        "qg": row(q_norm_g[l]), "kvg": row(jnp.pad(kv_norm_g[l], (0, D_KV_PAD - D_KV_LORA))),
        "wq2": wq2.astype(BF16), "wkv2": wkv2.astype(BF16),
        "lng": row(sgu_ln_g[l]), "lnb": row(sgu_ln_b[l]),
        "ws": sgu_w[l].reshape(C_HEADS * C_CHUNK, C_CHUNK).astype(BF16), "bs": bias_sgu.astype(F32),
        "pa": p_a[l].astype(BF16), "pb": p_b[l].astype(BF16), "pc": p_c[l].astype(BF16),
        "pd": pd_pad.reshape(D_PAD_W, D_MODEL).astype(BF16), "wo": w_o[l].astype(BF16),
        "w1": w1[l].astype(BF16), "w2": w2[l].astype(BF16),
    }


def _rope_tables(S):
    half = D_ROPE // 2
    inv = ROPE_THETA ** (-jnp.arange(half, dtype=F32) / half)
    ang = jnp.arange(S, dtype=F32)[:, None] * inv[None, :]
    cos = jnp.cos(ang)
    sin = jnp.sin(ang)
    pad = jnp.zeros((S, D_HEAD_PAD - D_NOPE - D_ROPE), F32)
    cos_t = jnp.concatenate([jnp.ones((S, D_NOPE), F32), cos, cos, pad], axis=1)
    sin_t = jnp.concatenate([jnp.zeros((S, D_NOPE), F32), sin, sin, pad], axis=1)
    return cos_t, sin_t


def _run_group(x, mods, layers, bias_tiles_fn, final_g):
    B, S, _ = x.shape
    cos_t, sin_t = _rope_tables(S)
    tables = _dft_tables(S)
    bias_tiles = bias_tiles_fn(S)
    ones_np = np.zeros((D_HEADS, D_HEAD_PAD), np.float32)
    ones_np[:, D_V] = 1.0
    ones_row = jnp.asarray(ones_np.reshape(1, D_PAD_W))
    gf = final_g.reshape(1, D_MODEL)
    for l, lw in enumerate(layers):
        mod = mods[l]
        y1, y2, b0, b1, b2, cu, cvn, mq, mk, mv = _inproj(x, mod, lw, cos_t, sin_t, ones_row)
        ya = _fourier_seq(tables, y1, y2)
        band = _dilated_branch(bias_tiles, b0, b1, b2)
        yd = _mla_attention(mq, mk, mv)
        x = _merge(x, mod, lw, ya, band, cu, cvn, yd)
        x = _mlp(x, mod, lw, gf, final=(l == len(layers) - 1))
    return x


def kernel(x_prompt, x_sample, c_prompt, c_sample, rel_bias, ada_w, ada_b, norm1_g, w_in, mla_q_norm_g, mla_kv_norm_g, mla_w_uq, mla_w_ukv, sgu_ln_g, sgu_ln_b, sgu_w, sgu_b, p_a, p_b, p_c, p_d, w_o, norm2_g, mlp_w1, mlp_w2, final_g):
    nb_p = c_prompt.shape[0]
    nb_s = c_sample.shape[0]
    rows = -(-(nb_p + nb_s) // 8) * 8
    c_all = jnp.concatenate([c_prompt, c_sample, jnp.zeros((rows - nb_p - nb_s, D_MODEL), F32)], axis=0)
    mod_all = _ada_mod(c_all, ada_w, ada_b).reshape(DEPTH, rows, N_MOD, D_MODEL)
    mods_p = mod_all[:, :nb_p]
    mods_s = mod_all[:, nb_p:nb_p + nb_s]

    wfft = _fold_channel_dft(w_in[:, :, :A_W])
    layers = [_prepare_layer(l, w_in, wfft, norm1_g, mla_q_norm_g, mla_kv_norm_g, mla_w_uq, mla_w_ukv,
                             sgu_ln_g, sgu_ln_b, sgu_w, sgu_b, p_a, p_b, p_c, p_d, w_o, norm2_g, mlp_w1, mlp_w2)
              for l in range(DEPTH)]

    tile_cache = {}

    def bias_tiles_fn(S):
        out = []
        for g, (_, dil) in enumerate(B_PAIRS):
            key = (g, min(2 * B_QBLK, S // dil))
            if key not in tile_cache:
                tile_cache[key] = _band_bias_tiles(rel_bias[:, 2 * g:2 * g + 2], dil, S // dil)
            out.append(tile_cache[key])
        return out

    y_prompt = _run_group(x_prompt, mods_p, layers, bias_tiles_fn, final_g)
    y_sample = _run_group(x_sample, mods_s, layers, bias_tiles_fn, final_g)
    return (y_prompt, y_sample)
```

```python
import functools
import math

import jax
import jax.numpy as jnp
import numpy as np
from jax import lax
from jax.experimental import pallas as pl
from jax.experimental.pallas import tpu as pltpu

F32 = jnp.float32
BF16 = jnp.bfloat16
HIGHEST = lax.Precision.HIGHEST

D_MODEL = 1024
DEPTH = 4
EPS = 1e-6
NEG_INF = -1e30
N_MOD = 6
A_GROUPS = 4
A_GROUP_W = 192
A_W = A_GROUPS * A_GROUP_W
B_PAIRS = ((128, 1), (512, 4), (2048, 16))
B_HEAD_DIM = 64
B_QKV_W = 384
B_GROUP_W = 128
B_HALF = 64
B_QBLK = 128
REL_BUCKETS = 32
REL_MAX_DIST = 1024
C_HEADS = 4
C_HEAD_W = 96
C_W = 384
C_CHUNK = 128
D_HEADS = 4
D_Q_LORA = 384
D_KV_LORA = 320
D_KV_PAD = 384
D_NOPE = 64
D_ROPE = 32
D_V = 64
D_HEAD_PAD = 128
D_PAD_W = D_HEADS * D_HEAD_PAD
ROPE_THETA = 10000.0
LOG2_E = math.log2(math.e)
D_FF = 4 * D_MODEL
MIX_W = A_W + 3 * B_QKV_W + 2 * C_W + D_Q_LORA + D_KV_LORA + D_ROPE

VMEM_LIMIT_BYTES = 56 * 1024 * 1024
LANE = 128
INPROJ_TILE = 512
INPROJ_SPLITS = (2 * A_W, 3 * B_QKV_W, 2 * C_W, D_Q_LORA, D_KV_PAD, 2 * D_HEAD_PAD)
MERGE_TILE = 512
MLP_TILE = 1024
SUB_ROWS = 512
DFT_N2 = 128
DFT_KRON_COLS = 64
DFT_PITCH_PAD = 8
DFT_COPY_UNROLL = 8
DFT_MATMUL_UNROLL = 4
DFT_SUB = 16
ATT_TQ = 2048
ATT_TK = 2048
BAND_ROWS = 4096
BAND_UNROLL = 16


def _cparams(*sem):
    return pltpu.CompilerParams(dimension_semantics=sem, vmem_limit_bytes=VMEM_LIMIT_BYTES)


def _resident(shape):
    nd = len(shape)
    return pl.BlockSpec(shape, lambda *_: (0,) * nd, pipeline_mode=pl.Buffered(1))


def _dot(a, b):
    return jnp.dot(a, b, preferred_element_type=F32)


def _dot_nt(a, b):
    return lax.dot_general(a, b, (((1,), (1,)), ((), ())), preferred_element_type=F32)


def _ada_kernel(c_ref, w_ref, b_ref, o_ref):
    c = c_ref[...]
    a = c * jax.nn.sigmoid(c)
    o_ref[0] = jnp.dot(a, w_ref[0], preferred_element_type=F32, precision=HIGHEST) + b_ref[0]


def _ada_mod(c_all, ada_w, ada_b):
    rows = c_all.shape[0]
    tn = 1536
    return pl.pallas_call(
        _ada_kernel,
        grid=(DEPTH, N_MOD * D_MODEL // tn),
        in_specs=[
            pl.BlockSpec((rows, D_MODEL), lambda l, j: (0, 0)),
            pl.BlockSpec((1, D_MODEL, tn), lambda l, j: (l, 0, j)),
            pl.BlockSpec((1, 1, tn), lambda l, j: (l, 0, j)),
        ],
        out_specs=pl.BlockSpec((1, rows, tn), lambda l, j: (l, 0, j)),
        out_shape=jax.ShapeDtypeStruct((DEPTH, rows, N_MOD * D_MODEL), F32),
        compiler_params=_cparams("parallel", "parallel"),
        name="ada_mod",
    )(c_all, ada_w, ada_b.reshape(DEPTH, 1, N_MOD * D_MODEL))


def _fold_kernel(w_ref, m_ref, o_ref):
    o_ref[0] = jnp.dot(w_ref[0], m_ref[...], preferred_element_type=F32, precision=HIGHEST).astype(BF16)


def _fold_channel_dft(w_a):
    n = np.arange(A_GROUP_W)
    idx = (n[:, None] * n[None, :]) % A_GROUP_W
    ang = 2.0 * np.pi * idx / A_GROUP_W
    eye = np.eye(A_GROUPS)
    m = np.concatenate([np.kron(eye, np.cos(ang)), np.kron(eye, np.sin(ang))], axis=1).astype(np.float32)
    return pl.pallas_call(
        _fold_kernel,
        grid=(DEPTH,),
        in_specs=[
            pl.BlockSpec((1, D_MODEL, A_W), lambda l: (l, 0, 0)),
            pl.BlockSpec((A_W, 2 * A_W), lambda l: (0, 0)),
        ],
        out_specs=pl.BlockSpec((1, D_MODEL, 2 * A_W), lambda l: (l, 0, 0)),
        out_shape=jax.ShapeDtypeStruct((DEPTH, D_MODEL, 2 * A_W), BF16),
        compiler_params=_cparams("parallel"),
        name="fold_channel_dft",
    )(w_a, jnp.asarray(m))


def _modulated_norm(x, g, shift, scale):
    ms = jnp.mean(x * x, axis=-1, keepdims=True)
    return (x * lax.rsqrt(ms + EPS) * g) * (1.0 + scale) + shift


def _inproj_kernel(x_ref, mod_ref, g1_ref, cos_ref, sin_ref, wmain_ref,
                   qg_ref, kvg_ref, wq2_ref, wkv2_ref, lng_ref, lnb_ref, ones_ref,
                   y1_ref, y2_ref, b0_ref, b1_ref, b2_ref, cu_ref, cvn_ref, mq_ref, mk_ref, mv_ref, res_sc, *, tm):
    nres = 2 * B_QKV_W // LANE
    for t in range(tm // SUB_ROWS):
        rs = slice(t * SUB_ROWS, (t + 1) * SUB_ROWS)
        x = x_ref[0, rs, :]
        h = _modulated_norm(x, g1_ref[...], mod_ref[0, 0:1, :], mod_ref[0, 1:2, :])
        hb = h.astype(BF16)

        z = _dot(hb, wmain_ref[...])
        pieces, o = [], 0
        for w in INPROJ_SPLITS:
            pieces.append(z[:, o:o + w])
            o += w
        y, zb, zc, dcq, dckv, kr = pieces

        y1_ref[0, rs, :] = y[:, :A_W].astype(BF16)
        y2_ref[0, rs, :] = y[:, A_W:].astype(BF16)

        b0_ref[0, rs, :] = zb[:, :B_QKV_W].astype(BF16)
        for c in range(nres):
            res_sc[t * nres + c] = zb[:, B_QKV_W + c * LANE:B_QKV_W + (c + 1) * LANE]
        for g, out_ref in ((1, b1_ref), (2, b2_ref)):
            dil = B_PAIRS[g][1]
            n = SUB_ROWS // dil
            for r in range(dil):
                for c in range(3):
                    out_ref[0, r, t * n:(t + 1) * n, c * LANE:(c + 1) * LANE] = (
                        res_sc.at[t * nres + 3 * (g - 1) + c][pl.ds(r, n, stride=dil), :].astype(BF16))

        cu_ref[0, rs, :] = zc[:, :C_W].astype(BF16)
        cv = zc[:, C_W:]
        mu = jnp.mean(cv, axis=-1, keepdims=True)
        cc = cv - mu
        var = jnp.mean(cc * cc, axis=-1, keepdims=True)
        cvn_ref[0, rs, :] = (cc * lax.rsqrt(var + EPS) * lng_ref[...] + lnb_ref[...]).astype(BF16)

        cos = cos_ref[rs, :]
        sin = sin_ref[rs, :]
        cos4 = jnp.concatenate([cos] * D_HEADS, axis=1)
        sin4 = jnp.concatenate([sin] * D_HEADS, axis=1)

        msq = jnp.mean(dcq * dcq, axis=-1, keepdims=True)
        qn = (dcq * lax.rsqrt(msq + EPS) * qg_ref[...]).astype(BF16)
        qq = _dot(qn, wq2_ref[...])
        q = (qq[:, :D_PAD_W] * cos4 + qq[:, D_PAD_W:] * sin4) * ((D_NOPE + D_ROPE) ** -0.5 * LOG2_E)
        mq_ref[0, rs, :] = q.astype(BF16)

        mskv = jnp.sum(dckv * dckv, axis=-1, keepdims=True) * (1.0 / D_KV_LORA)
        kvn = (dckv * lax.rsqrt(mskv + EPS) * kvg_ref[...]).astype(BF16)
        kv = _dot(kvn, wkv2_ref[...])
        kpe = kr[:, :D_HEAD_PAD] * cos + kr[:, D_HEAD_PAD:] * sin
        mk_ref[0, rs, :] = (kv[:, :D_PAD_W] + jnp.concatenate([kpe] * D_HEADS, axis=1)).astype(BF16)
        mv_ref[0, rs, :] = (kv[:, D_PAD_W:] + ones_ref[...]).astype(BF16)


def _inproj(x, mod, lw, cos_t, sin_t, ones_row):
    B, S, _ = x.shape
    tm = INPROJ_TILE
    row = lambda w: pl.BlockSpec((1, tm, w), lambda b, i: (b, i, 0))
    tab = pl.BlockSpec((tm, D_HEAD_PAD), lambda b, i: (i, 0))
    weights = [lw["wmain"], lw["qg"], lw["kvg"], lw["wq2"], lw["wkv2"], lw["lng"], lw["lnb"], ones_row]

    def nat(w):
        return row(w), jax.ShapeDtypeStruct((B, S, w), BF16)

    def residue(dil):
        return (pl.BlockSpec((1, dil, tm // dil, B_QKV_W), lambda b, i: (b, 0, i, 0)),
                jax.ShapeDtypeStruct((B, dil, S // dil, B_QKV_W), BF16))

    outs = [nat(A_W), nat(A_W), nat(B_QKV_W), residue(B_PAIRS[1][1]), residue(B_PAIRS[2][1]),
            nat(C_W), nat(C_W), nat(D_PAD_W), nat(D_PAD_W), nat(D_PAD_W)]
    return pl.pallas_call(
        functools.partial(_inproj_kernel, tm=tm),
        grid=(B, S // tm),
        in_specs=[row(D_MODEL),
                  pl.BlockSpec((1, N_MOD, D_MODEL), lambda b, i: (b, 0, 0)),
                  _resident(lw["g1"].shape), tab, tab] + [_resident(w.shape) for w in weights],
        out_specs=[o[0] for o in outs],
        out_shape=[o[1] for o in outs],
        scratch_shapes=[pltpu.VMEM((tm // SUB_ROWS * 2 * B_QKV_W // LANE, SUB_ROWS, LANE), F32)],
        compiler_params=_cparams("parallel", "parallel"),
        name="inproj",
    )(x, mod, lw["g1"], cos_t, sin_t, *weights)


def _fft_stage1_kron_kernel(m_ref, y1_ref, y2_ref, ar_ref, ai_ref, *, n1, nsub):
    rows = n1 * DFT_SUB
    for s in range(nsub):
        js = slice(s * DFT_SUB, (s + 1) * DFT_SUB)
        rhs = jnp.concatenate([y1_ref[0, :, js, :].reshape(rows, A_W), y2_ref[0, :, js, :].reshape(rows, A_W)], axis=0)
        out = _dot(m_ref[...], rhs).astype(BF16)
        ar_ref[0, :, js, :] = out[:rows].reshape(n1, DFT_SUB, A_W)
        ai_ref[0, :, js, :] = out[rows:].reshape(n1, DFT_SUB, A_W)


def _fft_stage1_kernel(m_ref, y1_ref, y2_ref, ar_ref, ai_ref, sc, *, n1):
    nl = A_W // LANE
    pitch = n1 + DFT_PITCH_PAD

    def fill(i, carry):
        rows = pl.ds(i, DFT_SUB, stride=pitch)
        for c in range(nl):
            sl = slice(c * LANE, (c + 1) * LANE)
            sc.at[c][rows, :] = y1_ref[0, i, :, sl].astype(F32)
            sc.at[nl + c][rows, :] = y2_ref[0, i, :, sl].astype(F32)
        return carry

    lax.fori_loop(0, n1, fill, 0, unroll=DFT_COPY_UNROLL)

    def body(j, carry):
        rows = pl.ds(pl.multiple_of(j * pitch, 8), n1)
        top = jnp.concatenate([sc.at[c][rows, :] for c in range(nl)], axis=1)
        bot = jnp.concatenate([sc.at[nl + c][rows, :] for c in range(nl)], axis=1)
        out = _dot(m_ref[...], jnp.concatenate([top, bot], axis=0).astype(BF16))
        for c in range(nl):
            sc.at[c][rows, :] = out[:n1, c * LANE:(c + 1) * LANE]
            sc.at[nl + c][rows, :] = out[n1:, c * LANE:(c + 1) * LANE]
        return carry

    lax.fori_loop(0, DFT_SUB, body, 0, unroll=DFT_MATMUL_UNROLL)

    def drain(i, carry):
        rows = pl.ds(i, DFT_SUB, stride=pitch)
        for c in range(nl):
            sl = slice(c * LANE, (c + 1) * LANE)
            ar_ref[0, i, :, sl] = sc.at[c][rows, :].astype(BF16)
            ai_ref[0, i, :, sl] = sc.at[nl + c][rows, :].astype(BF16)
        return carry

    lax.fori_loop(0, n1, drain, 0, unroll=DFT_COPY_UNROLL)


def _fft_stage2_kernel(h_ref, ar_ref, ai_ref, o_ref, sc, *, n2):
    nl = A_W // LANE
    pitch = n2 + DFT_PITCH_PAD

    def body(j, carry):
        rhs = jnp.concatenate([ar_ref[0, j], ai_ref[0, j]], axis=0)
        out = _dot(h_ref[j], rhs)
        rows = pl.ds(pl.multiple_of(j * pitch, 8), n2)
        for c in range(nl):
            sc.at[c][rows, :] = out[:, c * LANE:(c + 1) * LANE]
        return carry

    lax.fori_loop(0, DFT_SUB, body, 0, unroll=DFT_MATMUL_UNROLL)

    def drain(k2, carry):
        rows = pl.ds(k2, DFT_SUB, stride=pitch)
        for c in range(nl):
            o_ref[0, k2, :, c * LANE:(c + 1) * LANE] = sc.at[c][rows, :].astype(BF16)
        return carry

    lax.fori_loop(0, n2, drain, 0, unroll=DFT_COPY_UNROLL)


def _fft_two_stage(m1, h2, y1, y2):
    B, S, _ = y1.shape
    n2 = DFT_N2
    n1 = S // n2
    nl = A_W // LANE
    if n1 == DFT_SUB:
        jb = min(n2, DFT_KRON_COLS)
        stage1 = functools.partial(_fft_stage1_kron_kernel, n1=n1, nsub=jb // DFT_SUB)
        scratch = []
    else:
        jb = DFT_SUB
        stage1 = functools.partial(_fft_stage1_kernel, n1=n1)
        scratch = [pltpu.VMEM((2 * nl, DFT_SUB * (n1 + DFT_PITCH_PAD), LANE), F32)]
    blk = pl.BlockSpec((1, n1, jb, A_W), lambda b, j: (b, 0, j, 0))
    ar, ai = pl.pallas_call(
        stage1,
        grid=(B, n2 // jb),
        in_specs=[_resident(m1.shape), blk, blk],
        out_specs=[blk, blk],
        out_shape=[jax.ShapeDtypeStruct((B, n1, n2, A_W), BF16)] * 2,
        scratch_shapes=scratch,
        compiler_params=_cparams("parallel", "parallel"),
        name="fft_stage1",
    )(m1, y1.reshape(B, n1, n2, A_W), y2.reshape(B, n1, n2, A_W))
    slab = pl.BlockSpec((1, DFT_SUB, n2, A_W), lambda b, j: (b, j, 0, 0))
    out = pl.pallas_call(
        functools.partial(_fft_stage2_kernel, n2=n2),
        grid=(B, n1 // DFT_SUB),
        in_specs=[pl.BlockSpec((DFT_SUB, n2, 2 * n2), lambda b, j: (j, 0, 0)), slab, slab],
        out_specs=pl.BlockSpec((1, n2, DFT_SUB, A_W), lambda b, j: (b, 0, j, 0)),
        out_shape=jax.ShapeDtypeStruct((B, n2, n1, A_W), BF16),
        scratch_shapes=[pltpu.VMEM((nl, DFT_SUB * (n2 + DFT_PITCH_PAD), LANE), F32)],
        compiler_params=_cparams("parallel", "parallel"),
        name="fft_stage2",
    )(h2, ar, ai)
    return out.reshape(B, S, A_W)


def _dft_tables(S):
    scale = 1.0 / math.sqrt(S * A_GROUP_W)

    def cs(idx, period):
        ang = idx.astype(F32) * (2.0 * math.pi / period)
        return jnp.cos(ang), jnp.sin(ang)

    n2 = DFT_N2
    n1 = S // n2
    a = jnp.arange(n1, dtype=jnp.int32)
    c1, s1 = cs((a[:, None] * a[None, :]) % n1, n1)
    if n1 == DFT_SUB:
        eye = jnp.eye(DFT_SUB, dtype=F32)
        c1, s1 = jnp.kron(c1, eye), jnp.kron(s1, eye)
    m1 = jnp.concatenate([jnp.concatenate([c1, -s1], axis=1), jnp.concatenate([s1, c1], axis=1)], axis=0)
    k1 = a[:, None, None]
    k2 = jnp.arange(n2, dtype=jnp.int32)[None, :, None]
    nn = jnp.arange(n2, dtype=jnp.int32)[None, None, :]
    hc, hs = cs((nn * (k1 + n1 * k2)) % S, S)
    h2 = jnp.concatenate([hc * scale, -hs * scale], axis=2)
    return {"m1": m1.astype(BF16), "h2": h2.astype(BF16)}


def _fourier_seq(tables, y1, y2):
    return _fft_two_stage(tables["m1"], tables["h2"], y1, y2)


def _band_kernel(bias_ref, q_ref, k_ref, v_ref, o_ref, lse_ref, *, seq, rows, nseq, kwin):
    nsub = rows // B_QBLK
    last_blk = seq // B_QBLK - 1
    j = pl.program_id(1)
    head0 = lax.broadcasted_iota(jnp.int32, (B_QBLK, B_GROUP_W), 1) < B_HEAD_DIM

    def body(t, carry):
        si = t // nsub
        b = t % nsub
        r0 = pl.multiple_of(b * B_QBLK, B_QBLK)
        gb = j * nsub + b
        ws = pl.multiple_of(jnp.clip(gb * B_QBLK - B_HALF, 0, seq - kwin), B_HALF)
        tix = jnp.where(gb == 0, 0, jnp.where(gb == last_blk, 2, 1))
        q = q_ref[si, pl.ds(r0, B_QBLK), :]
        kw = k_ref[si, pl.ds(ws, kwin), :]
        vw = v_ref[si, pl.ds(ws, kwin), :]
        zero = jnp.zeros_like(q)
        q2 = jnp.concatenate([jnp.where(head0, q, zero), jnp.where(head0, zero, q)], axis=0)
        s = _dot_nt(q2, kw) + bias_ref[tix]
        m = jnp.max(s, axis=1, keepdims=True)
        p = jnp.exp(s - m)
        l = jnp.sum(p, axis=1, keepdims=True)
        o2 = _dot(p.astype(BF16), vw) * (1.0 / l)
        lse2 = jnp.broadcast_to(m + jnp.log(l), (2 * B_QBLK, B_GROUP_W))
        o_ref[si, pl.ds(r0, B_QBLK), :] = jnp.where(head0, o2[:B_QBLK], o2[B_QBLK:]).astype(BF16)
        lse_ref[si, pl.ds(r0, B_QBLK), :] = jnp.where(head0, lse2[:B_QBLK], lse2[B_QBLK:])
        return carry

    lax.fori_loop(0, nseq * nsub, body, 0, unroll=BAND_UNROLL)


def _band_attention(bias, arr, cols):
    N, L, _ = arr.shape
    kwin = bias.shape[-1]
    if L >= BAND_ROWS:
        nseq, rows = 1, BAND_ROWS
    else:
        nseq, rows = min(BAND_ROWS // L, N), L
    assert N % nseq == 0 and (nseq * rows // B_QBLK) % BAND_UNROLL == 0
    cq, ck, cv = cols
    out_spec = pl.BlockSpec((nseq, rows, B_GROUP_W), lambda n, j: (n, j, 0))
    return pl.pallas_call(
        functools.partial(_band_kernel, seq=L, rows=rows, nseq=nseq, kwin=kwin),
        grid=(N // nseq, L // rows),
        in_specs=[
            _resident(bias.shape),
            pl.BlockSpec((nseq, rows, B_GROUP_W), lambda n, j: (n, j, cq)),
            pl.BlockSpec((nseq, L, B_GROUP_W), lambda n, j: (n, 0, ck)),
            pl.BlockSpec((nseq, L, B_GROUP_W), lambda n, j: (n, 0, cv)),
        ],
        out_specs=[out_spec, out_spec],
        out_shape=[jax.ShapeDtypeStruct((N, L, B_GROUP_W), BF16), jax.ShapeDtypeStruct((N, L, B_GROUP_W), F32)],
        compiler_params=_cparams("parallel", "arbitrary"),
        name="band_attention",
    )(bias, arr, arr, arr)


def _t5_bucket(rel):
    half = REL_BUCKETS // 2
    max_exact = half // 2
    ret = jnp.where(rel > 0, half, 0)
    n = jnp.abs(rel)
    large = max_exact + (jnp.log(jnp.maximum(n, max_exact).astype(F32) / max_exact)
                         / math.log(REL_MAX_DIST / max_exact) * (half - max_exact)).astype(jnp.int32)
    large = jnp.minimum(large, half - 1)
    return ret + jnp.where(n < max_exact, n, large)


def _band_bias_tiles(rel_bias_g, dil, L):
    kwin = min(2 * B_QBLK, L)
    i = np.arange(B_QBLK)[:, None]
    jj = np.arange(kwin)[None, :]
    tiles = []
    for off in (0, -B_HALF, B_QBLK - kwin):
        rel = jj + off - i
        band = np.abs(rel) <= B_HALF
        bucket = _t5_bucket(jnp.asarray(rel * dil, dtype=jnp.int32))
        onehot = (bucket[..., None] == jnp.arange(REL_BUCKETS, dtype=jnp.int32)).astype(F32)
        vals = jnp.einsum("ijb,bh->hij", onehot, rel_bias_g.astype(F32), precision=HIGHEST)
        tiles.append(jnp.where(jnp.asarray(band)[None], vals, NEG_INF))
    return jnp.stack(tiles).reshape(3, 2 * B_QBLK, kwin)


def _dilated_branch(bias_tiles, b0, b1, b2):
    outs = []
    for g, arr in enumerate((b0, b1, b2)):
        lead = arr.shape[:-2]
        L = arr.shape[-2]
        o, lse = _band_attention(bias_tiles[g], arr.reshape(-1, L, B_QKV_W), (0, 1, 2))
        outs.append((o.reshape(*lead, L, B_GROUP_W), lse.reshape(*lead, L, B_GROUP_W)))
    return outs


def _mla_kernel(q_ref, k_ref, v_ref, o_ref, m_sc, acc_sc, *, tk):
    kv = pl.program_id(2)

    @pl.when(kv == 0)
    def _():
        m_sc[...] = jnp.full(m_sc.shape, -jnp.inf, F32)
        acc_sc[...] = jnp.zeros(acc_sc.shape, F32)

    for h in range(D_HEADS):
        sl = slice(h * D_HEAD_PAD, (h + 1) * D_HEAD_PAD)
        s = _dot_nt(q_ref[0, :, sl], k_ref[0, :, sl])
        m_prev = m_sc[h]
        m_new = jnp.maximum(m_prev, jnp.max(s, axis=1, keepdims=True))
        alpha = jnp.exp2(m_prev - m_new)
        p = jnp.exp2(s - jnp.concatenate([m_new] * (tk // LANE), axis=1))
        acc_sc[h] = alpha * acc_sc[h] + _dot(p.astype(BF16), v_ref[0, :, sl])
        m_sc[h] = m_new

    @pl.when(kv == pl.num_programs(2) - 1)
    def _():
        for h in range(D_HEADS):
            acc = acc_sc[h]
            o_ref[0, :, h * D_HEAD_PAD:(h + 1) * D_HEAD_PAD] = (acc * (1.0 / acc[:, D_V:D_V + 1])).astype(BF16)


def _mla_attention(q, k, v):
    B, S, _ = q.shape
    tq = min(ATT_TQ, S)
    tk = min(ATT_TK, S)
    return pl.pallas_call(
        functools.partial(_mla_kernel, tk=tk),
        grid=(B, S // tq, S // tk),
        in_specs=[
            pl.BlockSpec((1, tq, D_PAD_W), lambda b, i, j: (b, i, 0)),
            pl.BlockSpec((1, tk, D_PAD_W), lambda b, i, j: (b, j, 0)),
            pl.BlockSpec((1, tk, D_PAD_W), lambda b, i, j: (b, j, 0)),
        ],
        out_specs=pl.BlockSpec((1, tq, D_PAD_W), lambda b, i, j: (b, i, 0)),
        out_shape=jax.ShapeDtypeStruct((B, S, D_PAD_W), BF16),
        scratch_shapes=[pltpu.VMEM((D_HEADS, tq, LANE), F32), pltpu.VMEM((D_HEADS, tq, D_HEAD_PAD), F32)],
        compiler_params=_cparams("parallel", "parallel", "arbitrary"),
        name="mla_attention",
    )(q, k, v)


def _merge_kernel(x_ref, mod_ref, g1_ref, ya_ref, ob0_ref, ob1_ref, ob2_ref, l0_ref, l1_ref, l2_ref,
                  cu_ref, cvn_ref, yd_ref, wg_ref, pa_ref, pb_ref, pc_ref, pd_ref, wo_ref,
                  ws_ref, bs_ref, o_ref, nat_sc, *, tm):
    x = x_ref[0]
    h = _modulated_norm(x, g1_ref[...], mod_ref[0, 0:1, :], mod_ref[0, 1:2, :])
    hb = h.astype(BF16)

    zg = _dot(hb, wg_ref[...])

    def gate(i):
        return jax.nn.sigmoid(zg[:, i * D_MODEL:(i + 1) * D_MODEL])

    merged = gate(0) * _dot(ya_ref[0], pa_ref[...])

    for k, (o_ref_g, l_ref_g, dil) in enumerate(((ob1_ref, l1_ref, B_PAIRS[1][1]), (ob2_ref, l2_ref, B_PAIRS[2][1]))):
        for r in range(dil):
            rows = pl.ds(r, tm // dil, stride=dil)
            nat_sc.at[2 * k][rows, :] = o_ref_g[0, r].astype(F32)
            nat_sc.at[2 * k + 1][rows, :] = l_ref_g[0, r]
    l0, l1, l2 = l0_ref[0], nat_sc[1], nat_sc[3]
    mx = jnp.maximum(jnp.maximum(l0, l1), l2)
    e0, e1, e2 = jnp.exp(l0 - mx), jnp.exp(l1 - mx), jnp.exp(l2 - mx)
    yb = (e0 * ob0_ref[0].astype(F32) + e1 * nat_sc[0] + e2 * nat_sc[2]) / (e0 + e1 + e2)
    merged = merged + gate(1) * _dot(yb.astype(BF16), pb_ref[...])

    head = lax.broadcasted_iota(jnp.int32, (C_CHUNK, C_W), 1) // C_HEAD_W
    chunks = []
    for c in range(tm // (2 * C_CHUNK)):
        mm = _dot(ws_ref[...], cvn_ref[0, 2 * c * C_CHUNK:2 * (c + 1) * C_CHUNK, :])
        for half in range(2):
            rs = slice((2 * c + half) * C_CHUNK, (2 * c + half + 1) * C_CHUNK)
            mixed = bs_ref[...]
            for hh in range(C_HEADS):
                r0 = (half * C_HEADS + hh) * C_CHUNK
                mixed = mixed + jnp.where(head == hh, mm[r0:r0 + C_CHUNK], 0.0)
            chunks.append((cu_ref[0, rs, :].astype(F32) * mixed).astype(BF16))
    yc = jnp.concatenate(chunks, axis=0)
    merged = merged + gate(2) * _dot(yc, pc_ref[...])

    merged = merged + gate(3) * _dot(yd_ref[0], pd_ref[...])
    o_ref[0] = x + mod_ref[0, 2:3, :] * _dot(merged.astype(BF16), wo_ref[...])


def _merge(x, mod, lw, ya, band, cu, cvn, yd):
    B, S, _ = x.shape
    tm = MERGE_TILE
    row = lambda w: pl.BlockSpec((1, tm, w), lambda b, i: (b, i, 0))
    weights = [lw["wg"], lw["pa"], lw["pb"], lw["pc"], lw["pd"], lw["wo"], lw["ws"], lw["bs"]]
    (ob0, l0), (ob1, l1), (ob2, l2) = band
    res = lambda dil: pl.BlockSpec((1, dil, tm // dil, B_GROUP_W), lambda b, i: (b, 0, i, 0))
    r1, r2 = res(B_PAIRS[1][1]), res(B_PAIRS[2][1])
    return pl.pallas_call(
        functools.partial(_merge_kernel, tm=tm),
        grid=(B, S // tm),
        in_specs=[row(D_MODEL), pl.BlockSpec((1, N_MOD, D_MODEL), lambda b, i: (b, 0, 0)), _resident(lw["g1"].shape),
                  row(A_W), row(B_GROUP_W), r1, r2, row(B_GROUP_W), r1, r2, row(C_W), row(C_W), row(D_PAD_W)]
                 + [_resident(w.shape) for w in weights],
        out_specs=row(D_MODEL),
        out_shape=jax.ShapeDtypeStruct((B, S, D_MODEL), F32),
        scratch_shapes=[pltpu.VMEM((4, tm, B_GROUP_W), F32)],
        compiler_params=_cparams("parallel", "parallel"),
        name="merge",
    )(x, mod, lw["g1"], ya, ob0, ob1, ob2, l0, l1, l2, cu, cvn, yd, *weights)


def _mlp_kernel(x_ref, mod_ref, g2_ref, w1_ref, w2_ref, gf_ref, o_ref, *, final):
    x = x_ref[0]
    h = _modulated_norm(x, g2_ref[...], mod_ref[0, 3:4, :], mod_ref[0, 4:5, :])
    a = jnp.maximum(_dot(h.astype(BF16), w1_ref[...]), 0.0)
    y = x + mod_ref[0, 5:6, :] * _dot((a * a).astype(BF16), w2_ref[...])
    if final:
        ms = jnp.mean(y * y, axis=-1, keepdims=True)
        y = y * lax.rsqrt(ms + EPS) * gf_ref[...]
    o_ref[0] = y


def _mlp(x, mod, lw, final_g, final):
    B, S, _ = x.shape
    tm = MLP_TILE
    row = pl.BlockSpec((1, tm, D_MODEL), lambda b, i: (b, i, 0))
    return pl.pallas_call(
        functools.partial(_mlp_kernel, final=final),
        grid=(B, S // tm),
        in_specs=[row, pl.BlockSpec((1, N_MOD, D_MODEL), lambda b, i: (b, 0, 0)), _resident(lw["g2"].shape),
                  _resident(lw["w1"].shape), _resident(lw["w2"].shape), _resident(final_g.shape)],
        out_specs=row,
        out_shape=jax.ShapeDtypeStruct((B, S, D_MODEL), F32),
        compiler_params=_cparams("parallel", "parallel"),
        name="mlp",
    )(x, mod, lw["g2"], lw["w1"], lw["w2"], final_g)


def _head_slots(w, per_head, take):
    k = w.shape[0]
    wh = w.reshape(k, D_HEADS, per_head)[:, :, take]
    pad = D_HEAD_PAD - wh.shape[-1]
    return jnp.pad(wh, ((0, 0), (0, 0), (0, pad))).reshape(k, D_PAD_W)


def _rope_rot_cols(w):
    half = D_ROPE // 2
    return jnp.concatenate([-w[..., half:], w[..., :half]], axis=-1)


def _prepare_layer(l, w_in, wfft, norm1_g, q_norm_g, kv_norm_g, w_uq, w_ukv, sgu_ln_g, sgu_ln_b, sgu_w, sgu_b,
                   p_a, p_b, p_c, p_d, w_o, norm2_g, w1, w2):
    wi = w_in[l]
    o = A_W
    wb = wi[:, o:o + 3 * B_QKV_W]; o += 3 * B_QKV_W
    qkv_scale = jnp.asarray([B_HEAD_DIM ** -0.5, 1.0, 1.0], F32)[None, None, :, None]
    wb = (wb.reshape(D_MODEL, 3, len(B_PAIRS), B_GROUP_W).transpose(0, 2, 1, 3) * qkv_scale).reshape(D_MODEL, 3 * B_QKV_W)
    wc = wi[:, o:o + 2 * C_W]; o += 2 * C_W
    wdq = wi[:, o:o + D_Q_LORA]; o += D_Q_LORA
    wdkv = wi[:, o:o + D_KV_LORA]; o += D_KV_LORA
    wkr = wi[:, o:o + D_ROPE]; o += D_ROPE
    wg = wi[:, o:]

    def rope_slot(w):
        return jnp.pad(w, ((0, 0), (D_NOPE, D_HEAD_PAD - D_NOPE - D_ROPE)))

    wkr2 = jnp.concatenate([rope_slot(wkr), rope_slot(_rope_rot_cols(wkr))], axis=1)

    uq = w_uq[l].reshape(D_Q_LORA, D_HEADS, D_NOPE + D_ROPE)
    uq_rot = jnp.concatenate([jnp.zeros_like(uq[..., :D_NOPE]), _rope_rot_cols(uq[..., D_NOPE:])], axis=-1)
    padq = ((0, 0), (0, 0), (0, D_HEAD_PAD - D_NOPE - D_ROPE))
    wq2 = jnp.concatenate([jnp.pad(uq, padq).reshape(D_Q_LORA, D_PAD_W),
                           jnp.pad(uq_rot, padq).reshape(D_Q_LORA, D_PAD_W)], axis=1)

    ukv = jnp.pad(w_ukv[l], ((0, D_KV_PAD - D_KV_LORA), (0, 0)))
    wkv2 = jnp.concatenate([_head_slots(ukv, D_NOPE + D_V, slice(0, D_NOPE)),
                            _head_slots(ukv, D_NOPE + D_V, slice(D_NOPE, D_NOPE + D_V))], axis=1)

    pd_pad = jnp.pad(p_d[l].reshape(D_HEADS, D_V, D_MODEL), ((0, 0), (0, D_HEAD_PAD - D_V), (0, 0)))
    bias_sgu = jnp.repeat(sgu_b[l].T, C_HEAD_W, axis=1)
    row = lambda v: v.reshape(1, -1).astype(F32)
    return {
        "g1": row(norm1_g[l]), "g2": row(norm2_g[l]),
        "wmain": jnp.concatenate([wfft[l], wb.astype(BF16), wc.astype(BF16), wdq.astype(BF16),
                                  jnp.pad(wdkv, ((0, 0), (0, D_KV_PAD - D_KV_LORA))).astype(BF16),
                                  wkr2.astype(BF16)], axis=1),
        "wg": wg.astype(BF16),
        "qg": row(q_norm_g[l]), "kvg": row(jnp.pad(kv_norm_g[l], (0, D_KV_PAD - D_KV_LORA))),
        "wq2": wq2.astype(BF16), "wkv2": wkv2.astype(BF16),
        "lng": row(sgu_ln_g[l]), "lnb": row(sgu_ln_b[l]),
        "ws": jnp.kron(jnp.eye(2, dtype=F32), sgu_w[l].reshape(C_HEADS * C_CHUNK, C_CHUNK)).astype(BF16),
        "bs": bias_sgu.astype(F32),
        "pa": p_a[l].astype(BF16), "pb": p_b[l].astype(BF16), "pc": p_c[l].astype(BF16),
        "pd": pd_pad.reshape(D_PAD_W, D_MODEL).astype(BF16), "wo": w_o[l].astype(BF16),
        "w1": w1[l].astype(BF16), "w2": w2[l].astype(BF16),
    }


def _rope_tables(S):
    half = D_ROPE // 2
    inv = ROPE_THETA ** (-jnp.arange(half, dtype=F32) / half)
    ang = jnp.arange(S, dtype=F32)[:, None] * inv[None, :]
    cos = jnp.cos(ang)
    sin = jnp.sin(ang)
    pad = jnp.zeros((S, D_HEAD_PAD - D_NOPE - D_ROPE), F32)
    cos_t = jnp.concatenate([jnp.ones((S, D_NOPE), F32), cos, cos, pad], axis=1)
    sin_t = jnp.concatenate([jnp.zeros((S, D_NOPE), F32), sin, sin, pad], axis=1)
    return cos_t, sin_t


def _run_group(x, mods, layers, bias_tiles_fn, final_g):
    B, S, _ = x.shape
    cos_t, sin_t = _rope_tables(S)
    tables = _dft_tables(S)
    bias_tiles = bias_tiles_fn(S)
    ones_np = np.zeros((D_HEADS, D_HEAD_PAD), np.float32)
    ones_np[:, D_V] = 1.0
    ones_row = jnp.asarray(ones_np.reshape(1, D_PAD_W))
    gf = final_g.reshape(1, D_MODEL)
    for l, lw in enumerate(layers):
        mod = mods[l]
        y1, y2, b0, b1, b2, cu, cvn, mq, mk, mv = _inproj(x, mod, lw, cos_t, sin_t, ones_row)
        ya = _fourier_seq(tables, y1, y2)
        band = _dilated_branch(bias_tiles, b0, b1, b2)
        yd = _mla_attention(mq, mk, mv)
        x = _merge(x, mod, lw, ya, band, cu, cvn, yd)
        x = _mlp(x, mod, lw, gf, final=(l == len(layers) - 1))
    return x


def kernel(x_prompt, x_sample, c_prompt, c_sample, rel_bias, ada_w, ada_b, norm1_g, w_in, mla_q_norm_g, mla_kv_norm_g, mla_w_uq, mla_w_ukv, sgu_ln_g, sgu_ln_b, sgu_w, sgu_b, p_a, p_b, p_c, p_d, w_o, norm2_g, mlp_w1, mlp_w2, final_g):
    nb_p = c_prompt.shape[0]
    nb_s = c_sample.shape[0]
    rows = -(-(nb_p + nb_s) // 8) * 8
    c_all = jnp.concatenate([c_prompt, c_sample, jnp.zeros((rows - nb_p - nb_s, D_MODEL), F32)], axis=0)
    mod_all = _ada_mod(c_all, ada_w, ada_b).reshape(DEPTH, rows, N_MOD, D_MODEL)
    mods_p = mod_all[:, :nb_p]
    mods_s = mod_all[:, nb_p:nb_p + nb_s]

    wfft = _fold_channel_dft(w_in[:, :, :A_W])
    layers = [_prepare_layer(l, w_in, wfft, norm1_g, mla_q_norm_g, mla_kv_norm_g, mla_w_uq, mla_w_ukv,
                             sgu_ln_g, sgu_ln_b, sgu_w, sgu_b, p_a, p_b, p_c, p_d, w_o, norm2_g, mlp_w1, mlp_w2)
              for l in range(DEPTH)]

    tile_cache = {}

    def bias_tiles_fn(S):
        out = []
        for g, (_, dil) in enumerate(B_PAIRS):
            key = (g, min(2 * B_QBLK, S // dil))
            if key not in tile_cache:
                tile_cache[key] = _band_bias_tiles(rel_bias[:, 2 * g:2 * g + 2], dil, S // dil)
            out.append(tile_cache[key])
        return out

    y_prompt = _run_group(x_prompt, mods_p, layers, bias_tiles_fn, final_g)
    y_sample = _run_group(x_sample, mods_s, layers, bias_tiles_fn, final_g)
    return (y_prompt, y_sample)
```

```python
import functools
import math

import jax
import jax.numpy as jnp
import numpy as np
from jax import lax
from jax.experimental import pallas as pl
from jax.experimental.pallas import tpu as pltpu

F32 = jnp.float32
BF16 = jnp.bfloat16
HIGHEST = lax.Precision.HIGHEST

D_MODEL = 1024
DEPTH = 4
EPS = 1e-6
NEG_INF = -1e30
N_MOD = 6
A_GROUPS = 4
A_GROUP_W = 192
A_W = A_GROUPS * A_GROUP_W
B_PAIRS = ((128, 1), (512, 4), (2048, 16))
B_HEAD_DIM = 64
B_QKV_W = 384
B_GROUP_W = 128
B_HALF = 64
B_QBLK = 128
REL_BUCKETS = 32
REL_MAX_DIST = 1024
C_HEADS = 4
C_HEAD_W = 96
C_W = 384
C_CHUNK = 128
D_HEADS = 4
D_Q_LORA = 384
D_KV_LORA = 320
D_KV_PAD = 384
D_NOPE = 64
D_ROPE = 32
D_V = 64
D_HEAD_PAD = 128
D_PAD_W = D_HEADS * D_HEAD_PAD
ROPE_THETA = 10000.0
LOG2_E = math.log2(math.e)

VMEM_LIMIT_BYTES = 56 * 1024 * 1024
LANE = 128
INPROJ_TILE = 512
INPROJ_SPLITS = (2 * A_W, 3 * B_QKV_W, 2 * C_W, D_Q_LORA, D_KV_PAD, 2 * D_HEAD_PAD)
MERGE_TILE = 1024
MLP_TILE = 1024
DFT_N2 = 128
DFT_KRON_COLS = 128
DFT_PITCH_PAD = 8
DFT_COPY_UNROLL = 8
DFT_MATMUL_UNROLL = 4
DFT_SUB = 16
ATT_TQ = 2048
ATT_TK = 2048
BAND_ROWS = 4096
BAND_UNROLL = 16


def _cparams(*sem):
    return pltpu.CompilerParams(dimension_semantics=sem, vmem_limit_bytes=VMEM_LIMIT_BYTES)


def _resident(shape):
    nd = len(shape)
    return pl.BlockSpec(shape, lambda *_: (0,) * nd, pipeline_mode=pl.Buffered(1))


def _dot(a, b):
    return jnp.dot(a, b, preferred_element_type=F32)


def _dot_nt(a, b):
    return lax.dot_general(a, b, (((1,), (1,)), ((), ())), preferred_element_type=F32)


def _ada_kernel(c_ref, w_ref, b_ref, o_ref):
    c = c_ref[...]
    a = c * jax.nn.sigmoid(c)
    o_ref[0] = jnp.dot(a, w_ref[0], preferred_element_type=F32, precision=HIGHEST) + b_ref[0]


def _ada_mod(c_all, ada_w, ada_b):
    rows = c_all.shape[0]
    tn = 1536
    return pl.pallas_call(
        _ada_kernel,
        grid=(DEPTH, N_MOD * D_MODEL // tn),
        in_specs=[
            pl.BlockSpec((rows, D_MODEL), lambda l, j: (0, 0)),
            pl.BlockSpec((1, D_MODEL, tn), lambda l, j: (l, 0, j)),
            pl.BlockSpec((1, 1, tn), lambda l, j: (l, 0, j)),
        ],
        out_specs=pl.BlockSpec((1, rows, tn), lambda l, j: (l, 0, j)),
        out_shape=jax.ShapeDtypeStruct((DEPTH, rows, N_MOD * D_MODEL), F32),
        compiler_params=_cparams("parallel", "parallel"),
        name="ada_mod",
    )(c_all, ada_w, ada_b.reshape(DEPTH, 1, N_MOD * D_MODEL))


def _fold_kernel(w_ref, m_ref, o_ref):
    o_ref[0] = jnp.dot(w_ref[0], m_ref[...], preferred_element_type=F32, precision=HIGHEST).astype(BF16)


def _fold_channel_dft(w_a):
    n = np.arange(A_GROUP_W)
    idx = (n[:, None] * n[None, :]) % A_GROUP_W
    ang = 2.0 * np.pi * idx / A_GROUP_W
    eye = np.eye(A_GROUPS)
    m = np.concatenate([np.kron(eye, np.cos(ang)), np.kron(eye, np.sin(ang))], axis=1).astype(np.float32)
    return pl.pallas_call(
        _fold_kernel,
        grid=(DEPTH,),
        in_specs=[
            pl.BlockSpec((1, D_MODEL, A_W), lambda l: (l, 0, 0)),
            pl.BlockSpec((A_W, 2 * A_W), lambda l: (0, 0)),
        ],
        out_specs=pl.BlockSpec((1, D_MODEL, 2 * A_W), lambda l: (l, 0, 0)),
        out_shape=jax.ShapeDtypeStruct((DEPTH, D_MODEL, 2 * A_W), BF16),
        compiler_params=_cparams("parallel"),
        name="fold_channel_dft",
    )(w_a, jnp.asarray(m))


def _modulated_norm(x, g, shift, scale):
    ms = jnp.mean(x * x, axis=-1, keepdims=True)
    return (x * lax.rsqrt(ms + EPS) * g) * (1.0 + scale) + shift


def _inproj_kernel(x_ref, mod_ref, g1_ref, cos_ref, sin_ref, wmain_ref,
                   qg_ref, kvg_ref, wq2_ref, wkv2_ref, lng_ref, lnb_ref, ones_ref,
                   y1_ref, y2_ref, b0_ref, b1_ref, b2_ref, cu_ref, cvn_ref, mq_ref, mk_ref, mv_ref, res_sc, *, tm):
    x = x_ref[0]
    h = _modulated_norm(x, g1_ref[...], mod_ref[0, 0:1, :], mod_ref[0, 1:2, :])
    hb = h.astype(BF16)

    z = _dot(hb, wmain_ref[...])
    pieces, o = [], 0
    for w in INPROJ_SPLITS:
        pieces.append(z[:, o:o + w])
        o += w
    y, zb, zc, dcq, dckv, kr = pieces

    y1_ref[0] = y[:, :A_W].astype(BF16)
    y2_ref[0] = y[:, A_W:].astype(BF16)

    b0_ref[0] = zb[:, :B_QKV_W].astype(BF16)
    for c in range(2 * B_QKV_W // LANE):
        res_sc[c] = zb[:, B_QKV_W + c * LANE:B_QKV_W + (c + 1) * LANE]
    for g, out_ref in ((1, b1_ref), (2, b2_ref)):
        dil = B_PAIRS[g][1]
        for r in range(dil):
            for c in range(3):
                out_ref[0, r, :, c * LANE:(c + 1) * LANE] = (
                    res_sc.at[3 * (g - 1) + c][pl.ds(r, tm // dil, stride=dil), :].astype(BF16))

    cu_ref[0] = zc[:, :C_W].astype(BF16)
    cv = zc[:, C_W:]
    mu = jnp.mean(cv, axis=-1, keepdims=True)
    cc = cv - mu
    var = jnp.mean(cc * cc, axis=-1, keepdims=True)
    cvn_ref[0] = (cc * lax.rsqrt(var + EPS) * lng_ref[...] + lnb_ref[...]).astype(BF16)

    cos = cos_ref[...]
    sin = sin_ref[...]
    cos4 = jnp.concatenate([cos] * D_HEADS, axis=1)
    sin4 = jnp.concatenate([sin] * D_HEADS, axis=1)

    msq = jnp.mean(dcq * dcq, axis=-1, keepdims=True)
    qn = (dcq * lax.rsqrt(msq + EPS) * qg_ref[...]).astype(BF16)
    qq = _dot(qn, wq2_ref[...])
    q = (qq[:, :D_PAD_W] * cos4 + qq[:, D_PAD_W:] * sin4) * ((D_NOPE + D_ROPE) ** -0.5 * LOG2_E)
    mq_ref[0] = q.astype(BF16)

    mskv = jnp.sum(dckv * dckv, axis=-1, keepdims=True) * (1.0 / D_KV_LORA)
    kvn = (dckv * lax.rsqrt(mskv + EPS) * kvg_ref[...]).astype(BF16)
    kv = _dot(kvn, wkv2_ref[...])
    kpe = kr[:, :D_HEAD_PAD] * cos + kr[:, D_HEAD_PAD:] * sin
    mk_ref[0] = (kv[:, :D_PAD_W] + jnp.concatenate([kpe] * D_HEADS, axis=1)).astype(BF16)
    mv_ref[0] = (kv[:, D_PAD_W:] + ones_ref[...]).astype(BF16)


def _inproj(x, mod, lw, cos_t, sin_t, ones_row):
    B, S, _ = x.shape
    tm = INPROJ_TILE
    row = lambda w: pl.BlockSpec((1, tm, w), lambda b, i: (b, i, 0))
    tab = pl.BlockSpec((tm, D_HEAD_PAD), lambda b, i: (i, 0))
    weights = [lw["wmain"], lw["qg"], lw["kvg"], lw["wq2"], lw["wkv2"], lw["lng"], lw["lnb"], ones_row]

    def nat(w):
        return row(w), jax.ShapeDtypeStruct((B, S, w), BF16)

    def residue(dil):
        return (pl.BlockSpec((1, dil, tm // dil, B_QKV_W), lambda b, i: (b, 0, i, 0)),
                jax.ShapeDtypeStruct((B, dil, S // dil, B_QKV_W), BF16))

    outs = [nat(A_W), nat(A_W), nat(B_QKV_W), residue(B_PAIRS[1][1]), residue(B_PAIRS[2][1]),
            nat(C_W), nat(C_W), nat(D_PAD_W), nat(D_PAD_W), nat(D_PAD_W)]
    return pl.pallas_call(
        functools.partial(_inproj_kernel, tm=tm),
        grid=(B, S // tm),
        in_specs=[row(D_MODEL),
                  pl.BlockSpec((1, N_MOD, D_MODEL), lambda b, i: (b, 0, 0)),
                  _resident(lw["g1"].shape), tab, tab] + [_resident(w.shape) for w in weights],
        out_specs=[o[0] for o in outs],
        out_shape=[o[1] for o in outs],
        scratch_shapes=[pltpu.VMEM((2 * B_QKV_W // LANE, tm, LANE), F32)],
        compiler_params=_cparams("parallel", "parallel"),
        name="inproj",
    )(x, mod, lw["g1"], cos_t, sin_t, *weights)


def _fft_stage1_kron_kernel(m_ref, y1_ref, y2_ref, ar_ref, ai_ref, *, n1, nsub):
    rows = n1 * DFT_SUB
    for s in range(nsub):
        js = slice(s * DFT_SUB, (s + 1) * DFT_SUB)
        rhs = jnp.concatenate([y1_ref[0, :, js, :].reshape(rows, A_W), y2_ref[0, :, js, :].reshape(rows, A_W)], axis=0)
        out = _dot(m_ref[...], rhs).astype(BF16)
        ar_ref[0, :, js, :] = out[:rows].reshape(n1, DFT_SUB, A_W)
        ai_ref[0, :, js, :] = out[rows:].reshape(n1, DFT_SUB, A_W)


def _fft_stage1_kernel(m_ref, y1_ref, y2_ref, ar_ref, ai_ref, sc, *, n1):
    nl = A_W // LANE
    pitch = n1 + DFT_PITCH_PAD

    def fill(i, carry):
        rows = pl.ds(i, DFT_SUB, stride=pitch)
        for c in range(nl):
            sl = slice(c * LANE, (c + 1) * LANE)
            sc.at[c][rows, :] = y1_ref[0, i, :, sl].astype(F32)
            sc.at[nl + c][rows, :] = y2_ref[0, i, :, sl].astype(F32)
        return carry

    lax.fori_loop(0, n1, fill, 0, unroll=DFT_COPY_UNROLL)

    def body(j, carry):
        rows = pl.ds(pl.multiple_of(j * pitch, 8), n1)
        top = jnp.concatenate([sc.at[c][rows, :] for c in range(nl)], axis=1)
        bot = jnp.concatenate([sc.at[nl + c][rows, :] for c in range(nl)], axis=1)
        out = _dot(m_ref[...], jnp.concatenate([top, bot], axis=0).astype(BF16))
        for c in range(nl):
            sc.at[c][rows, :] = out[:n1, c * LANE:(c + 1) * LANE]
            sc.at[nl + c][rows, :] = out[n1:, c * LANE:(c + 1) * LANE]
        return carry

    lax.fori_loop(0, DFT_SUB, body, 0, unroll=DFT_MATMUL_UNROLL)

    def drain(i, carry):
        rows = pl.ds(i, DFT_SUB, stride=pitch)
        for c in range(nl):
            sl = slice(c * LANE, (c + 1) * LANE)
            ar_ref[0, i, :, sl] = sc.at[c][rows, :].astype(BF16)
            ai_ref[0, i, :, sl] = sc.at[nl + c][rows, :].astype(BF16)
        return carry

    lax.fori_loop(0, n1, drain, 0, unroll=DFT_COPY_UNROLL)


def _fft_stage2_kernel(h_ref, ar_ref, ai_ref, o_ref, sc, *, n2):
    nl = A_W // LANE
    pitch = n2 + DFT_PITCH_PAD

    def body(j, carry):
        rhs = jnp.concatenate([ar_ref[0, j], ai_ref[0, j]], axis=0)
        out = _dot(h_ref[j], rhs)
        rows = pl.ds(pl.multiple_of(j * pitch, 8), n2)
        for c in range(nl):
            sc.at[c][rows, :] = out[:, c * LANE:(c + 1) * LANE]
        return carry

    lax.fori_loop(0, DFT_SUB, body, 0, unroll=DFT_MATMUL_UNROLL)

    def drain(k2, carry):
        rows = pl.ds(k2, DFT_SUB, stride=pitch)
        for c in range(nl):
            o_ref[0, k2, :, c * LANE:(c + 1) * LANE] = sc.at[c][rows, :].astype(BF16)
        return carry

    lax.fori_loop(0, n2, drain, 0, unroll=DFT_COPY_UNROLL)


def _fft_two_stage(m1, h2, y1, y2):
    B, S, _ = y1.shape
    n2 = DFT_N2
    n1 = S // n2
    nl = A_W // LANE
    if n1 == DFT_SUB:
        jb = min(n2, DFT_KRON_COLS)
        stage1 = functools.partial(_fft_stage1_kron_kernel, n1=n1, nsub=jb // DFT_SUB)
        scratch = []
    else:
        jb = DFT_SUB
        stage1 = functools.partial(_fft_stage1_kernel, n1=n1)
        scratch = [pltpu.VMEM((2 * nl, DFT_SUB * (n1 + DFT_PITCH_PAD), LANE), F32)]
    blk = pl.BlockSpec((1, n1, jb, A_W), lambda b, j: (b, 0, j, 0))
    ar, ai = pl.pallas_call(
        stage1,
        grid=(B, n2 // jb),
        in_specs=[_resident(m1.shape), blk, blk],
        out_specs=[blk, blk],
        out_shape=[jax.ShapeDtypeStruct((B, n1, n2, A_W), BF16)] * 2,
        scratch_shapes=scratch,
        compiler_params=_cparams("parallel", "parallel"),
        name="fft_stage1",
    )(m1, y1.reshape(B, n1, n2, A_W), y2.reshape(B, n1, n2, A_W))
    slab = pl.BlockSpec((1, DFT_SUB, n2, A_W), lambda b, j: (b, j, 0, 0))
    out = pl.pallas_call(
        functools.partial(_fft_stage2_kernel, n2=n2),
        grid=(B, n1 // DFT_SUB),
        in_specs=[pl.BlockSpec((DFT_SUB, n2, 2 * n2), lambda b, j: (j, 0, 0)), slab, slab],
        out_specs=pl.BlockSpec((1, n2, DFT_SUB, A_W), lambda b, j: (b, 0, j, 0)),
        out_shape=jax.ShapeDtypeStruct((B, n2, n1, A_W), BF16),
        scratch_shapes=[pltpu.VMEM((nl, DFT_SUB * (n2 + DFT_PITCH_PAD), LANE), F32)],
        compiler_params=_cparams("parallel", "parallel"),
        name="fft_stage2",
    )(h2, ar, ai)
    return out.reshape(B, S, A_W)


def _dft_tables(S):
    scale = 1.0 / math.sqrt(S * A_GROUP_W)

    def cs(idx, period):
        ang = idx.astype(F32) * (2.0 * math.pi / period)
        return jnp.cos(ang), jnp.sin(ang)

    n2 = DFT_N2
    n1 = S // n2
    a = jnp.arange(n1, dtype=jnp.int32)
    c1, s1 = cs((a[:, None] * a[None, :]) % n1, n1)
    if n1 == DFT_SUB:
        eye = jnp.eye(DFT_SUB, dtype=F32)
        c1, s1 = jnp.kron(c1, eye), jnp.kron(s1, eye)
    m1 = jnp.concatenate([jnp.concatenate([c1, -s1], axis=1), jnp.concatenate([s1, c1], axis=1)], axis=0)
    k1 = a[:, None, None]
    k2 = jnp.arange(n2, dtype=jnp.int32)[None, :, None]
    nn = jnp.arange(n2, dtype=jnp.int32)[None, None, :]
    hc, hs = cs((nn * (k1 + n1 * k2)) % S, S)
    h2 = jnp.concatenate([hc * scale, -hs * scale], axis=2)
    return {"m1": m1.astype(BF16), "h2": h2.astype(BF16)}


def _fourier_seq(tables, y1, y2):
    return _fft_two_stage(tables["m1"], tables["h2"], y1, y2)


def _band_kernel(bias_ref, q_ref, k_ref, v_ref, o_ref, lse_ref, *, seq, rows, nseq, kwin):
    nsub = rows // B_QBLK
    last_blk = seq // B_QBLK - 1
    j = pl.program_id(1)
    head0 = lax.broadcasted_iota(jnp.int32, (B_QBLK, B_GROUP_W), 1) < B_HEAD_DIM

    def body(t, carry):
        si = t // nsub
        b = t % nsub
        r0 = pl.multiple_of(b * B_QBLK, B_QBLK)
        gb = j * nsub + b
        ws = pl.multiple_of(jnp.clip(gb * B_QBLK - B_HALF, 0, seq - kwin), B_HALF)
        tix = jnp.where(gb == 0, 0, jnp.where(gb == last_blk, 2, 1))
        q = q_ref[si, pl.ds(r0, B_QBLK), :]
        kw = k_ref[si, pl.ds(ws, kwin), :]
        vw = v_ref[si, pl.ds(ws, kwin), :]
        zero = jnp.zeros_like(q)
        q2 = jnp.concatenate([jnp.where(head0, q, zero), jnp.where(head0, zero, q)], axis=0)
        s = _dot_nt(q2, kw) + bias_ref[tix]
        m = jnp.max(s, axis=1, keepdims=True)
        p = jnp.exp(s - m)
        l = jnp.sum(p, axis=1, keepdims=True)
        o2 = _dot(p.astype(BF16), vw) * (1.0 / l)
        lse2 = jnp.broadcast_to(m + jnp.log(l), (2 * B_QBLK, B_GROUP_W))
        o_ref[si, pl.ds(r0, B_QBLK), :] = jnp.where(head0, o2[:B_QBLK], o2[B_QBLK:]).astype(BF16)
        lse_ref[si, pl.ds(r0, B_QBLK), :] = jnp.where(head0, lse2[:B_QBLK], lse2[B_QBLK:])
        return carry

    lax.fori_loop(0, nseq * nsub, body, 0, unroll=BAND_UNROLL)


def _band_attention(bias, arr, cols):
    N, L, _ = arr.shape
    kwin = bias.shape[-1]
    if L >= BAND_ROWS:
        nseq, rows = 1, BAND_ROWS
    else:
        nseq, rows = min(BAND_ROWS // L, N), L
    assert N % nseq == 0 and (nseq * rows // B_QBLK) % BAND_UNROLL == 0
    cq, ck, cv = cols
    out_spec = pl.BlockSpec((nseq, rows, B_GROUP_W), lambda n, j: (n, j, 0))
    return pl.pallas_call(
        functools.partial(_band_kernel, seq=L, rows=rows, nseq=nseq, kwin=kwin),
        grid=(N // nseq, L // rows),
        in_specs=[
            _resident(bias.shape),
            pl.BlockSpec((nseq, rows, B_GROUP_W), lambda n, j: (n, j, cq)),
            pl.BlockSpec((nseq, L, B_GROUP_W), lambda n, j: (n, 0, ck)),
            pl.BlockSpec((nseq, L, B_GROUP_W), lambda n, j: (n, 0, cv)),
        ],
        out_specs=[out_spec, out_spec],
        out_shape=[jax.ShapeDtypeStruct((N, L, B_GROUP_W), BF16), jax.ShapeDtypeStruct((N, L, B_GROUP_W), F32)],
        compiler_params=_cparams("parallel", "arbitrary"),
        name="band_attention",
    )(bias, arr, arr, arr)


def _t5_bucket(rel):
    half = REL_BUCKETS // 2
    max_exact = half // 2
    ret = jnp.where(rel > 0, half, 0)
    n = jnp.abs(rel)
    large = max_exact + (jnp.log(jnp.maximum(n, max_exact).astype(F32) / max_exact)
                         / math.log(REL_MAX_DIST / max_exact) * (half - max_exact)).astype(jnp.int32)
    large = jnp.minimum(large, half - 1)
    return ret + jnp.where(n < max_exact, n, large)


def _band_bias_tiles(rel_bias_g, dil, L):
    kwin = min(2 * B_QBLK, L)
    i = np.arange(B_QBLK)[:, None]
    jj = np.arange(kwin)[None, :]
    tiles = []
    for off in (0, -B_HALF, B_QBLK - kwin):
        rel = jj + off - i
        band = np.abs(rel) <= B_HALF
        bucket = _t5_bucket(jnp.asarray(rel * dil, dtype=jnp.int32))
        onehot = (bucket[..., None] == jnp.arange(REL_BUCKETS, dtype=jnp.int32)).astype(F32)
        vals = jnp.einsum("ijb,bh->hij", onehot, rel_bias_g.astype(F32), precision=HIGHEST)
        tiles.append(jnp.where(jnp.asarray(band)[None], vals, NEG_INF))
    return jnp.stack(tiles).reshape(3, 2 * B_QBLK, kwin)


def _dilated_branch(bias_tiles, b0, b1, b2):
    outs = []
    for g, arr in enumerate((b0, b1, b2)):
        lead = arr.shape[:-2]
        L = arr.shape[-2]
        o, lse = _band_attention(bias_tiles[g], arr.reshape(-1, L, B_QKV_W), (0, 1, 2))
        outs.append((o.reshape(*lead, L, B_GROUP_W), lse.reshape(*lead, L, B_GROUP_W)))
    return outs


def _mla_kernel(q_ref, k_ref, v_ref, o_ref, m_sc, acc_sc, *, tk):
    kv = pl.program_id(2)

    @pl.when(kv == 0)
    def _():
        m_sc[...] = jnp.full(m_sc.shape, -jnp.inf, F32)
        acc_sc[...] = jnp.zeros(acc_sc.shape, F32)

    for h in range(D_HEADS):
        sl = slice(h * D_HEAD_PAD, (h + 1) * D_HEAD_PAD)
        s = _dot_nt(q_ref[0, :, sl], k_ref[0, :, sl])
        m_prev = m_sc[h]
        m_new = jnp.maximum(m_prev, jnp.max(s, axis=1, keepdims=True))
        alpha = jnp.exp2(m_prev - m_new)
        p = jnp.exp2(s - jnp.concatenate([m_new] * (tk // LANE), axis=1))
        acc_sc[h] = alpha * acc_sc[h] + _dot(p.astype(BF16), v_ref[0, :, sl])
        m_sc[h] = m_new

    @pl.when(kv == pl.num_programs(2) - 1)
    def _():
        for h in range(D_HEADS):
            acc = acc_sc[h]
            o_ref[0, :, h * D_HEAD_PAD:(h + 1) * D_HEAD_PAD] = (acc * (1.0 / acc[:, D_V:D_V + 1])).astype(BF16)


def _mla_attention(q, k, v):
    B, S, _ = q.shape
    tq = min(ATT_TQ, S)
    tk = min(ATT_TK, S)
    return pl.pallas_call(
        functools.partial(_mla_kernel, tk=tk),
        grid=(B, S // tq, S // tk),
        in_specs=[
            pl.BlockSpec((1, tq, D_PAD_W), lambda b, i, j: (b, i, 0)),
            pl.BlockSpec((1, tk, D_PAD_W), lambda b, i, j: (b, j, 0)),
            pl.BlockSpec((1, tk, D_PAD_W), lambda b, i, j: (b, j, 0)),
        ],
        out_specs=pl.BlockSpec((1, tq, D_PAD_W), lambda b, i, j: (b, i, 0)),
        out_shape=jax.ShapeDtypeStruct((B, S, D_PAD_W), BF16),
        scratch_shapes=[pltpu.VMEM((D_HEADS, tq, LANE), F32), pltpu.VMEM((D_HEADS, tq, D_HEAD_PAD), F32)],
        compiler_params=_cparams("parallel", "parallel", "arbitrary"),
        name="mla_attention",
    )(q, k, v)


def _merge_kernel(x_ref, mod_ref, g1_ref, ya_ref, ob0_ref, ob1_ref, ob2_ref, l0_ref, l1_ref, l2_ref,
                  cu_ref, cvn_ref, yd_ref, wg_ref, pa_ref, pb_ref, pc_ref, pd_ref, wo_ref,
                  ws_ref, bs_ref, o_ref, nat_sc, *, tm):
    x = x_ref[0]
    h = _modulated_norm(x, g1_ref[...], mod_ref[0, 0:1, :], mod_ref[0, 1:2, :])
    hb = h.astype(BF16)

    def gate(i):
        return jax.nn.sigmoid(_dot(hb, wg_ref[:, i * D_MODEL:(i + 1) * D_MODEL]))

    merged = gate(0) * _dot(ya_ref[0], pa_ref[...])

    for k, (o_ref_g, l_ref_g, dil) in enumerate(((ob1_ref, l1_ref, B_PAIRS[1][1]), (ob2_ref, l2_ref, B_PAIRS[2][1]))):
        for r in range(dil):
            rows = pl.ds(r, tm // dil, stride=dil)
            nat_sc.at[2 * k][rows, :] = o_ref_g[0, r].astype(F32)
            nat_sc.at[2 * k + 1][rows, :] = l_ref_g[0, r]
    l0, l1, l2 = l0_ref[0], nat_sc[1], nat_sc[3]
    mx = jnp.maximum(jnp.maximum(l0, l1), l2)
    e0, e1, e2 = jnp.exp(l0 - mx), jnp.exp(l1 - mx), jnp.exp(l2 - mx)
    yb = (e0 * ob0_ref[0].astype(F32) + e1 * nat_sc[0] + e2 * nat_sc[2]) / (e0 + e1 + e2)
    merged = merged + gate(1) * _dot(yb.astype(BF16), pb_ref[...])

    head = lax.broadcasted_iota(jnp.int32, (C_CHUNK, C_W), 1) // C_HEAD_W
    chunks = []
    for c in range(tm // (2 * C_CHUNK)):
        mm = _dot(ws_ref[...], cvn_ref[0, 2 * c * C_CHUNK:2 * (c + 1) * C_CHUNK, :])
        for half in range(2):
            rs = slice((2 * c + half) * C_CHUNK, (2 * c + half + 1) * C_CHUNK)
            mixed = bs_ref[...]
            for hh in range(C_HEADS):
                r0 = (half * C_HEADS + hh) * C_CHUNK
                mixed = mixed + jnp.where(head == hh, mm[r0:r0 + C_CHUNK], 0.0)
            chunks.append((cu_ref[0, rs, :].astype(F32) * mixed).astype(BF16))
    yc = jnp.concatenate(chunks, axis=0)
    merged = merged + gate(2) * _dot(yc, pc_ref[...])

    merged = merged + gate(3) * _dot(yd_ref[0], pd_ref[...])
    o_ref[0] = x + mod_ref[0, 2:3, :] * _dot(merged.astype(BF16), wo_ref[...])


def _merge(x, mod, lw, ya, band, cu, cvn, yd):
    B, S, _ = x.shape
    tm = MERGE_TILE
    row = lambda w: pl.BlockSpec((1, tm, w), lambda b, i: (b, i, 0))
    weights = [lw["wg"], lw["pa"], lw["pb"], lw["pc"], lw["pd"], lw["wo"], lw["ws"], lw["bs"]]
    (ob0, l0), (ob1, l1), (ob2, l2) = band
    res = lambda dil: pl.BlockSpec((1, dil, tm // dil, B_GROUP_W), lambda b, i: (b, 0, i, 0))
    r1, r2 = res(B_PAIRS[1][1]), res(B_PAIRS[2][1])
    return pl.pallas_call(
        functools.partial(_merge_kernel, tm=tm),
        grid=(B, S // tm),
        in_specs=[row(D_MODEL), pl.BlockSpec((1, N_MOD, D_MODEL), lambda b, i: (b, 0, 0)), _resident(lw["g1"].shape),
                  row(A_W), row(B_GROUP_W), r1, r2, row(B_GROUP_W), r1, r2, row(C_W), row(C_W), row(D_PAD_W)]
                 + [_resident(w.shape) for w in weights],
        out_specs=row(D_MODEL),
        out_shape=jax.ShapeDtypeStruct((B, S, D_MODEL), F32),
        scratch_shapes=[pltpu.VMEM((4, tm, B_GROUP_W), F32)],
        compiler_params=_cparams("parallel", "parallel"),
        name="merge",
    )(x, mod, lw["g1"], ya, ob0, ob1, ob2, l0, l1, l2, cu, cvn, yd, *weights)


def _mlp_kernel(x_ref, mod_ref, g2_ref, w1_ref, w2_ref, gf_ref, o_ref, *, final):
    x = x_ref[0]
    h = _modulated_norm(x, g2_ref[...], mod_ref[0, 3:4, :], mod_ref[0, 4:5, :])
    a = jnp.maximum(_dot(h.astype(BF16), w1_ref[...]), 0.0)
    y = x + mod_ref[0, 5:6, :] * _dot((a * a).astype(BF16), w2_ref[...])
    if final:
        ms = jnp.mean(y * y, axis=-1, keepdims=True)
        y = y * lax.rsqrt(ms + EPS) * gf_ref[...]
    o_ref[0] = y


def _mlp(x, mod, lw, final_g, final):
    B, S, _ = x.shape
    tm = MLP_TILE
    row = pl.BlockSpec((1, tm, D_MODEL), lambda b, i: (b, i, 0))
    return pl.pallas_call(
        functools.partial(_mlp_kernel, final=final),
        grid=(B, S // tm),
        in_specs=[row, pl.BlockSpec((1, N_MOD, D_MODEL), lambda b, i: (b, 0, 0)), _resident(lw["g2"].shape),
                  _resident(lw["w1"].shape), _resident(lw["w2"].shape), _resident(final_g.shape)],
        out_specs=row,
        out_shape=jax.ShapeDtypeStruct((B, S, D_MODEL), F32),
        compiler_params=_cparams("parallel", "parallel"),
        name="mlp",
    )(x, mod, lw["g2"], lw["w1"], lw["w2"], final_g)


def _head_slots(w, per_head, take):
    k = w.shape[0]
    wh = w.reshape(k, D_HEADS, per_head)[:, :, take]
    pad = D_HEAD_PAD - wh.shape[-1]
    return jnp.pad(wh, ((0, 0), (0, 0), (0, pad))).reshape(k, D_PAD_W)


def _rope_rot_cols(w):
    half = D_ROPE // 2
    return jnp.concatenate([-w[..., half:], w[..., :half]], axis=-1)


def _prepare_layer(l, w_in, wfft, norm1_g, q_norm_g, kv_norm_g, w_uq, w_ukv, sgu_ln_g, sgu_ln_b, sgu_w, sgu_b,
                   p_a, p_b, p_c, p_d, w_o, norm2_g, w1, w2):
    wi = w_in[l]
    o = A_W
    wb = wi[:, o:o + 3 * B_QKV_W]; o += 3 * B_QKV_W
    qkv_scale = jnp.asarray([B_HEAD_DIM ** -0.5, 1.0, 1.0], F32)[None, None, :, None]
    wb = (wb.reshape(D_MODEL, 3, len(B_PAIRS), B_GROUP_W).transpose(0, 2, 1, 3) * qkv_scale).reshape(D_MODEL, 3 * B_QKV_W)
    wc = wi[:, o:o + 2 * C_W]; o += 2 * C_W
    wdq = wi[:, o:o + D_Q_LORA]; o += D_Q_LORA
    wdkv = wi[:, o:o + D_KV_LORA]; o += D_KV_LORA
    wkr = wi[:, o:o + D_ROPE]; o += D_ROPE
    wg = wi[:, o:]

    def rope_slot(w):
        return jnp.pad(w, ((0, 0), (D_NOPE, D_HEAD_PAD - D_NOPE - D_ROPE)))

    wkr2 = jnp.concatenate([rope_slot(wkr), rope_slot(_rope_rot_cols(wkr))], axis=1)

    uq = w_uq[l].reshape(D_Q_LORA, D_HEADS, D_NOPE + D_ROPE)
    uq_rot = jnp.concatenate([jnp.zeros_like(uq[..., :D_NOPE]), _rope_rot_cols(uq[..., D_NOPE:])], axis=-1)
    padq = ((0, 0), (0, 0), (0, D_HEAD_PAD - D_NOPE - D_ROPE))
    wq2 = jnp.concatenate([jnp.pad(uq, padq).reshape(D_Q_LORA, D_PAD_W),
                           jnp.pad(uq_rot, padq).reshape(D_Q_LORA, D_PAD_W)], axis=1)

    ukv = jnp.pad(w_ukv[l], ((0, D_KV_PAD - D_KV_LORA), (0, 0)))
    wkv2 = jnp.concatenate([_head_slots(ukv, D_NOPE + D_V, slice(0, D_NOPE)),
                            _head_slots(ukv, D_NOPE + D_V, slice(D_NOPE, D_NOPE + D_V))], axis=1)

    pd_pad = jnp.pad(p_d[l].reshape(D_HEADS, D_V, D_MODEL), ((0, 0), (0, D_HEAD_PAD - D_V), (0, 0)))
    bias_sgu = jnp.repeat(sgu_b[l].T, C_HEAD_W, axis=1)
    row = lambda v: v.reshape(1, -1).astype(F32)
    return {
        "g1": row(norm1_g[l]), "g2": row(norm2_g[l]),
        "wmain": jnp.concatenate([wfft[l], wb.astype(BF16), wc.astype(BF16), wdq.astype(BF16),
                                  jnp.pad(wdkv, ((0, 0), (0, D_KV_PAD - D_KV_LORA))).astype(BF16),
                                  wkr2.astype(BF16)], axis=1),
        "wg": wg.astype(BF16),
        "qg": row(q_norm_g[l]), "kvg": row(jnp.pad(kv_norm_g[l], (0, D_KV_PAD - D_KV_LORA))),
        "wq2": wq2.astype(BF16), "wkv2": wkv2.astype(BF16),
        "lng": row(sgu_ln_g[l]), "lnb": row(sgu_ln_b[l]),
        "ws": jnp.kron(jnp.eye(2, dtype=F32), sgu_w[l].reshape(C_HEADS * C_CHUNK, C_CHUNK)).astype(BF16),
        "bs": bias_sgu.astype(F32),
        "pa": p_a[l].astype(BF16), "pb": p_b[l].astype(BF16), "pc": p_c[l].astype(BF16),
        "pd": pd_pad.reshape(D_PAD_W, D_MODEL).astype(BF16), "wo": w_o[l].astype(BF16),
        "w1": w1[l].astype(BF16), "w2": w2[l].astype(BF16),
    }


def _rope_tables(S):
    half = D_ROPE // 2
    inv = ROPE_THETA ** (-jnp.arange(half, dtype=F32) / half)
    ang = jnp.arange(S, dtype=F32)[:, None] * inv[None, :]
    cos = jnp.cos(ang)
    sin = jnp.sin(ang)
    pad = jnp.zeros((S, D_HEAD_PAD - D_NOPE - D_ROPE), F32)
    cos_t = jnp.concatenate([jnp.ones((S, D_NOPE), F32), cos, cos, pad], axis=1)
    sin_t = jnp.concatenate([jnp.zeros((S, D_NOPE), F32), sin, sin, pad], axis=1)
    return cos_t, sin_t


def _run_group(x, mods, layers, bias_tiles_fn, final_g):
    B, S, _ = x.shape
    cos_t, sin_t = _rope_tables(S)
    tables = _dft_tables(S)
    bias_tiles = bias_tiles_fn(S)
    ones_np = np.zeros((D_HEADS, D_HEAD_PAD), np.float32)
    ones_np[:, D_V] = 1.0
    ones_row = jnp.asarray(ones_np.reshape(1, D_PAD_W))
    gf = final_g.reshape(1, D_MODEL)
    for l, lw in enumerate(layers):
        mod = mods[l]
        y1, y2, b0, b1, b2, cu, cvn, mq, mk, mv = _inproj(x, mod, lw, cos_t, sin_t, ones_row)
        ya = _fourier_seq(tables, y1, y2)
        band = _dilated_branch(bias_tiles, b0, b1, b2)
        yd = _mla_attention(mq, mk, mv)
        x = _merge(x, mod, lw, ya, band, cu, cvn, yd)
        x = _mlp(x, mod, lw, gf, final=(l == len(layers) - 1))
    return x


def kernel(x_prompt, x_sample, c_prompt, c_sample, rel_bias, ada_w, ada_b, norm1_g, w_in, mla_q_norm_g, mla_kv_norm_g, mla_w_uq, mla_w_ukv, sgu_ln_g, sgu_ln_b, sgu_w, sgu_b, p_a, p_b, p_c, p_d, w_o, norm2_g, mlp_w1, mlp_w2, final_g):
    nb_p = c_prompt.shape[0]
    nb_s = c_sample.shape[0]
    rows = -(-(nb_p + nb_s) // 8) * 8
    c_all = jnp.concatenate([c_prompt, c_sample, jnp.zeros((rows - nb_p - nb_s, D_MODEL), F32)], axis=0)
    mod_all = _ada_mod(c_all, ada_w, ada_b).reshape(DEPTH, rows, N_MOD, D_MODEL)
    mods_p = mod_all[:, :nb_p]
    mods_s = mod_all[:, nb_p:nb_p + nb_s]

    wfft = _fold_channel_dft(w_in[:, :, :A_W])
    layers = [_prepare_layer(l, w_in, wfft, norm1_g, mla_q_norm_g, mla_kv_norm_g, mla_w_uq, mla_w_ukv,
                             sgu_ln_g, sgu_ln_b, sgu_w, sgu_b, p_a, p_b, p_c, p_d, w_o, norm2_g, mlp_w1, mlp_w2)
              for l in range(DEPTH)]

    tile_cache = {}

    def bias_tiles_fn(S):
        out = []
        for g, (_, dil) in enumerate(B_PAIRS):
            key = (g, min(2 * B_QBLK, S // dil))
            if key not in tile_cache:
                tile_cache[key] = _band_bias_tiles(rel_bias[:, 2 * g:2 * g + 2], dil, S // dil)
            out.append(tile_cache[key])
        return out

    y_prompt = _run_group(x_prompt, mods_p, layers, bias_tiles_fn, final_g)
    y_sample = _run_group(x_sample, mods_s, layers, bias_tiles_fn, final_g)
    return (y_prompt, y_sample)
```

```python
import functools
import math

import jax
import jax.numpy as jnp
import numpy as np
from jax import lax
from jax.experimental import pallas as pl
from jax.experimental.pallas import tpu as pltpu

F32 = jnp.float32
BF16 = jnp.bfloat16
HIGHEST = lax.Precision.HIGHEST

D_MODEL = 1024
DEPTH = 4
EPS = 1e-6
NEG_INF = -1e30
N_MOD = 6
A_GROUPS = 4
A_GROUP_W = 192
A_W = A_GROUPS * A_GROUP_W
B_PAIRS = ((128, 1), (512, 4), (2048, 16))
B_HEAD_DIM = 64
B_QKV_W = 384
B_GROUP_W = 128
B_HALF = 64
B_QBLK = 128
REL_BUCKETS = 32
REL_MAX_DIST = 1024
C_HEADS = 4
C_HEAD_W = 96
C_W = 384
C_CHUNK = 128
D_HEADS = 4
D_Q_LORA = 384
D_KV_LORA = 320
D_KV_PAD = 384
D_NOPE = 64
D_ROPE = 32
D_V = 64
D_HEAD_PAD = 128
D_PAD_W = D_HEADS * D_HEAD_PAD
D_OUT_W = D_HEADS * D_V
ROPE_THETA = 10000.0
LOG2_E = math.log2(math.e)

VMEM_LIMIT_BYTES = 56 * 1024 * 1024
LANE = 128
INPROJ_TILE = 512
INPROJ_SPLITS = (2 * A_W, 3 * B_QKV_W, 2 * C_W, D_Q_LORA, D_KV_PAD, 2 * D_HEAD_PAD)
MERGE_TILE = 1024
MLP_TILE = 1024
DFT_N2 = 128
DFT_KRON_COLS = 128
DFT_PITCH_PAD = 8
DFT_COPY_UNROLL = 8
DFT_MATMUL_UNROLL = 4
DFT_SUB = 16
ATT_TQ = 2048
ATT_TK = 2048
BAND_ROWS = 4096
BAND_UNROLL = 16


def _cparams(*sem):
    return pltpu.CompilerParams(dimension_semantics=sem, vmem_limit_bytes=VMEM_LIMIT_BYTES)


def _resident(shape):
    nd = len(shape)
    return pl.BlockSpec(shape, lambda *_: (0,) * nd, pipeline_mode=pl.Buffered(1))


def _dot(a, b):
    return jnp.dot(a, b, preferred_element_type=F32)


def _dot_nt(a, b):
    return lax.dot_general(a, b, (((1,), (1,)), ((), ())), preferred_element_type=F32)


def _ada_kernel(c_ref, w_ref, b_ref, o_ref):
    c = c_ref[...]
    a = c * jax.nn.sigmoid(c)
    o_ref[0] = jnp.dot(a, w_ref[0], preferred_element_type=F32, precision=HIGHEST) + b_ref[0]


def _ada_mod(c_all, ada_w, ada_b):
    rows = c_all.shape[0]
    tn = 1536
    return pl.pallas_call(
        _ada_kernel,
        grid=(DEPTH, N_MOD * D_MODEL // tn),
        in_specs=[
            pl.BlockSpec((rows, D_MODEL), lambda l, j: (0, 0)),
            pl.BlockSpec((1, D_MODEL, tn), lambda l, j: (l, 0, j)),
            pl.BlockSpec((1, 1, tn), lambda l, j: (l, 0, j)),
        ],
        out_specs=pl.BlockSpec((1, rows, tn), lambda l, j: (l, 0, j)),
        out_shape=jax.ShapeDtypeStruct((DEPTH, rows, N_MOD * D_MODEL), F32),
        compiler_params=_cparams("parallel", "parallel"),
        name="ada_mod",
    )(c_all, ada_w, ada_b.reshape(DEPTH, 1, N_MOD * D_MODEL))


def _fold_kernel(w_ref, m_ref, o_ref):
    o_ref[0] = jnp.dot(w_ref[0], m_ref[...], preferred_element_type=F32, precision=HIGHEST).astype(BF16)


def _fold_channel_dft(w_a):
    n = np.arange(A_GROUP_W)
    idx = (n[:, None] * n[None, :]) % A_GROUP_W
    ang = 2.0 * np.pi * idx / A_GROUP_W
    eye = np.eye(A_GROUPS)
    m = np.concatenate([np.kron(eye, np.cos(ang)), np.kron(eye, np.sin(ang))], axis=1).astype(np.float32)
    return pl.pallas_call(
        _fold_kernel,
        grid=(DEPTH,),
        in_specs=[
            pl.BlockSpec((1, D_MODEL, A_W), lambda l: (l, 0, 0)),
            pl.BlockSpec((A_W, 2 * A_W), lambda l: (0, 0)),
        ],
        out_specs=pl.BlockSpec((1, D_MODEL, 2 * A_W), lambda l: (l, 0, 0)),
        out_shape=jax.ShapeDtypeStruct((DEPTH, D_MODEL, 2 * A_W), BF16),
        compiler_params=_cparams("parallel"),
        name="fold_channel_dft",
    )(w_a, jnp.asarray(m))


def _modulated_norm(x, g, shift, scale):
    ms = jnp.mean(x * x, axis=-1, keepdims=True)
    return (x * lax.rsqrt(ms + EPS) * g) * (1.0 + scale) + shift


def _inproj_kernel(x_ref, mod_ref, g1_ref, cos_ref, sin_ref, wmain_ref,
                   qg_ref, kvg_ref, wq2_ref, wkv2_ref, lng_ref, lnb_ref, ones_ref,
                   y1_ref, y2_ref, b0_ref, b1_ref, b2_ref, cu_ref, cvn_ref, mq_ref, mk_ref, mv_ref, res_sc, *, tm):
    x = x_ref[0]
    h = _modulated_norm(x, g1_ref[...], mod_ref[0, 0:1, :], mod_ref[0, 1:2, :])
    hb = h.astype(BF16)

    z = _dot(hb, wmain_ref[...])
    pieces, o = [], 0
    for w in INPROJ_SPLITS:
        pieces.append(z[:, o:o + w])
        o += w
    y, zb, zc, dcq, dckv, kr = pieces

    y1_ref[0] = y[:, :A_W].astype(BF16)
    y2_ref[0] = y[:, A_W:].astype(BF16)

    b0_ref[0] = zb[:, :B_QKV_W].astype(BF16)
    for c in range(2 * B_QKV_W // LANE):
        res_sc[c] = zb[:, B_QKV_W + c * LANE:B_QKV_W + (c + 1) * LANE]
    for g, out_ref in ((1, b1_ref), (2, b2_ref)):
        dil = B_PAIRS[g][1]
        for r in range(dil):
            for c in range(3):
                out_ref[0, r, :, c * LANE:(c + 1) * LANE] = (
                    res_sc.at[3 * (g - 1) + c][pl.ds(r, tm // dil, stride=dil), :].astype(BF16))

    cu_ref[0] = zc[:, :C_W].astype(BF16)
    cv = zc[:, C_W:]
    mu = jnp.mean(cv, axis=-1, keepdims=True)
    cc = cv - mu
    var = jnp.mean(cc * cc, axis=-1, keepdims=True)
    cvn_ref[0] = (cc * lax.rsqrt(var + EPS) * lng_ref[...] + lnb_ref[...]).astype(BF16)

    cos = cos_ref[...]
    sin = sin_ref[...]
    cos4 = jnp.concatenate([cos] * D_HEADS, axis=1)
    sin4 = jnp.concatenate([sin] * D_HEADS, axis=1)

    msq = jnp.mean(dcq * dcq, axis=-1, keepdims=True)
    qn = (dcq * lax.rsqrt(msq + EPS) * qg_ref[...]).astype(BF16)
    qq = _dot(qn, wq2_ref[...])
    q = (qq[:, :D_PAD_W] * cos4 + qq[:, D_PAD_W:] * sin4) * ((D_NOPE + D_ROPE) ** -0.5 * LOG2_E)
    mq_ref[0] = q.astype(BF16)

    mskv = jnp.sum(dckv * dckv, axis=-1, keepdims=True) * (1.0 / D_KV_LORA)
    kvn = (dckv * lax.rsqrt(mskv + EPS) * kvg_ref[...]).astype(BF16)
    kv = _dot(kvn, wkv2_ref[...])
    kpe = kr[:, :D_HEAD_PAD] * cos + kr[:, D_HEAD_PAD:] * sin
    mk_ref[0] = (kv[:, :D_PAD_W] + jnp.concatenate([kpe] * D_HEADS, axis=1)).astype(BF16)
    mv_ref[0] = (kv[:, D_PAD_W:] + ones_ref[...]).astype(BF16)


def _inproj(x, mod, lw, cos_t, sin_t, ones_row):
    B, S, _ = x.shape
    tm = INPROJ_TILE
    row = lambda w: pl.BlockSpec((1, tm, w), lambda b, i: (b, i, 0))
    tab = pl.BlockSpec((tm, D_HEAD_PAD), lambda b, i: (i, 0))
    weights = [lw["wmain"], lw["qg"], lw["kvg"], lw["wq2"], lw["wkv2"], lw["lng"], lw["lnb"], ones_row]

    def nat(w):
        return row(w), jax.ShapeDtypeStruct((B, S, w), BF16)

    def residue(dil):
        return (pl.BlockSpec((1, dil, tm // dil, B_QKV_W), lambda b, i: (b, 0, i, 0)),
                jax.ShapeDtypeStruct((B, dil, S // dil, B_QKV_W), BF16))

    outs = [nat(A_W), nat(A_W), nat(B_QKV_W), residue(B_PAIRS[1][1]), residue(B_PAIRS[2][1]),
            nat(C_W), nat(C_W), nat(D_PAD_W), nat(D_PAD_W), nat(D_PAD_W)]
    return pl.pallas_call(
        functools.partial(_inproj_kernel, tm=tm),
        grid=(B, S // tm),
        in_specs=[row(D_MODEL),
                  pl.BlockSpec((1, N_MOD, D_MODEL), lambda b, i: (b, 0, 0)),
                  _resident(lw["g1"].shape), tab, tab] + [_resident(w.shape) for w in weights],
        out_specs=[o[0] for o in outs],
        out_shape=[o[1] for o in outs],
        scratch_shapes=[pltpu.VMEM((2 * B_QKV_W // LANE, tm, LANE), F32)],
        compiler_params=_cparams("parallel", "parallel"),
        name="inproj",
    )(x, mod, lw["g1"], cos_t, sin_t, *weights)


def _fft_stage1_kron_kernel(m_ref, y1_ref, y2_ref, ar_ref, ai_ref, *, n1, nsub):
    rows = n1 * DFT_SUB
    for s in range(nsub):
        js = slice(s * DFT_SUB, (s + 1) * DFT_SUB)
        rhs = jnp.concatenate([y1_ref[0, :, js, :].reshape(rows, A_W), y2_ref[0, :, js, :].reshape(rows, A_W)], axis=0)
        out = _dot(m_ref[...], rhs).astype(BF16)
        ar_ref[0, :, js, :] = out[:rows].reshape(n1, DFT_SUB, A_W)
        ai_ref[0, :, js, :] = out[rows:].reshape(n1, DFT_SUB, A_W)


def _fft_stage1_kernel(m_ref, y1_ref, y2_ref, ar_ref, ai_ref, sc, *, n1):
    nl = A_W // LANE
    pitch = n1 + DFT_PITCH_PAD

    def fill(i, carry):
        rows = pl.ds(i, DFT_SUB, stride=pitch)
        for c in range(nl):
            sl = slice(c * LANE, (c + 1) * LANE)
            sc.at[c][rows, :] = y1_ref[0, i, :, sl].astype(F32)
            sc.at[nl + c][rows, :] = y2_ref[0, i, :, sl].astype(F32)
        return carry

    lax.fori_loop(0, n1, fill, 0, unroll=DFT_COPY_UNROLL)

    def body(j, carry):
        rows = pl.ds(pl.multiple_of(j * pitch, 8), n1)
        top = jnp.concatenate([sc.at[c][rows, :] for c in range(nl)], axis=1)
        bot = jnp.concatenate([sc.at[nl + c][rows, :] for c in range(nl)], axis=1)
        out = _dot(m_ref[...], jnp.concatenate([top, bot], axis=0).astype(BF16))
        for c in range(nl):
            sc.at[c][rows, :] = out[:n1, c * LANE:(c + 1) * LANE]
            sc.at[nl + c][rows, :] = out[n1:, c * LANE:(c + 1) * LANE]
        return carry

    lax.fori_loop(0, DFT_SUB, body, 0, unroll=DFT_MATMUL_UNROLL)

    def drain(i, carry):
        rows = pl.ds(i, DFT_SUB, stride=pitch)
        for c in range(nl):
            sl = slice(c * LANE, (c + 1) * LANE)
            ar_ref[0, i, :, sl] = sc.at[c][rows, :].astype(BF16)
            ai_ref[0, i, :, sl] = sc.at[nl + c][rows, :].astype(BF16)
        return carry

    lax.fori_loop(0, n1, drain, 0, unroll=DFT_COPY_UNROLL)


def _fft_stage2_kernel(h_ref, ar_ref, ai_ref, o_ref, sc, *, n2):
    nl = A_W // LANE
    pitch = n2 + DFT_PITCH_PAD

    def body(j, carry):
        rhs = jnp.concatenate([ar_ref[0, j], ai_ref[0, j]], axis=0)
        out = _dot(h_ref[j], rhs)
        rows = pl.ds(pl.multiple_of(j * pitch, 8), n2)
        for c in range(nl):
            sc.at[c][rows, :] = out[:, c * LANE:(c + 1) * LANE]
        return carry

    lax.fori_loop(0, DFT_SUB, body, 0, unroll=DFT_MATMUL_UNROLL)

    def drain(k2, carry):
        rows = pl.ds(k2, DFT_SUB, stride=pitch)
        for c in range(nl):
            o_ref[0, k2, :, c * LANE:(c + 1) * LANE] = sc.at[c][rows, :].astype(BF16)
        return carry

    lax.fori_loop(0, n2, drain, 0, unroll=DFT_COPY_UNROLL)


def _fft_two_stage(m1, h2, y1, y2):
    B, S, _ = y1.shape
    n2 = DFT_N2
    n1 = S // n2
    nl = A_W // LANE
    if n1 == DFT_SUB:
        jb = min(n2, DFT_KRON_COLS)
        stage1 = functools.partial(_fft_stage1_kron_kernel, n1=n1, nsub=jb // DFT_SUB)
        scratch = []
    else:
        jb = DFT_SUB
        stage1 = functools.partial(_fft_stage1_kernel, n1=n1)
        scratch = [pltpu.VMEM((2 * nl, DFT_SUB * (n1 + DFT_PITCH_PAD), LANE), F32)]
    blk = pl.BlockSpec((1, n1, jb, A_W), lambda b, j: (b, 0, j, 0))
    ar, ai = pl.pallas_call(
        stage1,
        grid=(B, n2 // jb),
        in_specs=[_resident(m1.shape), blk, blk],
        out_specs=[blk, blk],
        out_shape=[jax.ShapeDtypeStruct((B, n1, n2, A_W), BF16)] * 2,
        scratch_shapes=scratch,
        compiler_params=_cparams("parallel", "parallel"),
        name="fft_stage1",
    )(m1, y1.reshape(B, n1, n2, A_W), y2.reshape(B, n1, n2, A_W))
    slab = pl.BlockSpec((1, DFT_SUB, n2, A_W), lambda b, j: (b, j, 0, 0))
    out = pl.pallas_call(
        functools.partial(_fft_stage2_kernel, n2=n2),
        grid=(B, n1 // DFT_SUB),
        in_specs=[pl.BlockSpec((DFT_SUB, n2, 2 * n2), lambda b, j: (j, 0, 0)), slab, slab],
        out_specs=pl.BlockSpec((1, n2, DFT_SUB, A_W), lambda b, j: (b, 0, j, 0)),
        out_shape=jax.ShapeDtypeStruct((B, n2, n1, A_W), BF16),
        scratch_shapes=[pltpu.VMEM((nl, DFT_SUB * (n2 + DFT_PITCH_PAD), LANE), F32)],
        compiler_params=_cparams("parallel", "parallel"),
        name="fft_stage2",
    )(h2, ar, ai)
    return out.reshape(B, S, A_W)


def _dft_tables(S):
    scale = 1.0 / math.sqrt(S * A_GROUP_W)

    def cs(idx, period):
        ang = idx.astype(F32) * (2.0 * math.pi / period)
        return jnp.cos(ang), jnp.sin(ang)

    n2 = DFT_N2
    n1 = S // n2
    a = jnp.arange(n1, dtype=jnp.int32)
    c1, s1 = cs((a[:, None] * a[None, :]) % n1, n1)
    if n1 == DFT_SUB:
        eye = jnp.eye(DFT_SUB, dtype=F32)
        c1, s1 = jnp.kron(c1, eye), jnp.kron(s1, eye)
    m1 = jnp.concatenate([jnp.concatenate([c1, -s1], axis=1), jnp.concatenate([s1, c1], axis=1)], axis=0)
    k1 = a[:, None, None]
    k2 = jnp.arange(n2, dtype=jnp.int32)[None, :, None]
    nn = jnp.arange(n2, dtype=jnp.int32)[None, None, :]
    hc, hs = cs((nn * (k1 + n1 * k2)) % S, S)
    h2 = jnp.concatenate([hc * scale, -hs * scale], axis=2)
    return {"m1": m1.astype(BF16), "h2": h2.astype(BF16)}


def _fourier_seq(tables, y1, y2):
    return _fft_two_stage(tables["m1"], tables["h2"], y1, y2)


def _band_kernel(bias_ref, q_ref, k_ref, v_ref, o_ref, lse_ref, *, seq, rows, nseq, kwin):
    nsub = rows // B_QBLK
    last_blk = seq // B_QBLK - 1
    j = pl.program_id(1)
    head0 = lax.broadcasted_iota(jnp.int32, (B_QBLK, B_GROUP_W), 1) < B_HEAD_DIM

    def body(t, carry):
        si = t // nsub
        b = t % nsub
        r0 = pl.multiple_of(b * B_QBLK, B_QBLK)
        gb = j * nsub + b
        ws = pl.multiple_of(jnp.clip(gb * B_QBLK - B_HALF, 0, seq - kwin), B_HALF)
        tix = jnp.where(gb == 0, 0, jnp.where(gb == last_blk, 2, 1))
        q = q_ref[si, pl.ds(r0, B_QBLK), :]
        kw = k_ref[si, pl.ds(ws, kwin), :]
        vw = v_ref[si, pl.ds(ws, kwin), :]
        zero = jnp.zeros_like(q)
        q2 = jnp.concatenate([jnp.where(head0, q, zero), jnp.where(head0, zero, q)], axis=0)
        s = _dot_nt(q2, kw) + bias_ref[tix]
        m = jnp.max(s, axis=1, keepdims=True)
        p = jnp.exp(s - m)
        l = jnp.sum(p, axis=1, keepdims=True)
        o2 = _dot(p.astype(BF16), vw) * (1.0 / l)
        lse2 = jnp.broadcast_to(m + jnp.log(l), (2 * B_QBLK, B_GROUP_W))
        o_ref[si, pl.ds(r0, B_QBLK), :] = jnp.where(head0, o2[:B_QBLK], o2[B_QBLK:]).astype(BF16)
        lse_ref[si, pl.ds(r0, B_QBLK), :] = jnp.where(head0, lse2[:B_QBLK], lse2[B_QBLK:])
        return carry

    lax.fori_loop(0, nseq * nsub, body, 0, unroll=BAND_UNROLL)


def _band_attention(bias, arr, cols):
    N, L, _ = arr.shape
    kwin = bias.shape[-1]
    if L >= BAND_ROWS:
        nseq, rows = 1, BAND_ROWS
    else:
        nseq, rows = min(BAND_ROWS // L, N), L
    assert N % nseq == 0 and (nseq * rows // B_QBLK) % BAND_UNROLL == 0
    cq, ck, cv = cols
    out_spec = pl.BlockSpec((nseq, rows, B_GROUP_W), lambda n, j: (n, j, 0))
    return pl.pallas_call(
        functools.partial(_band_kernel, seq=L, rows=rows, nseq=nseq, kwin=kwin),
        grid=(N // nseq, L // rows),
        in_specs=[
            _resident(bias.shape),
            pl.BlockSpec((nseq, rows, B_GROUP_W), lambda n, j: (n, j, cq)),
            pl.BlockSpec((nseq, L, B_GROUP_W), lambda n, j: (n, 0, ck)),
            pl.BlockSpec((nseq, L, B_GROUP_W), lambda n, j: (n, 0, cv)),
        ],
        out_specs=[out_spec, out_spec],
        out_shape=[jax.ShapeDtypeStruct((N, L, B_GROUP_W), BF16), jax.ShapeDtypeStruct((N, L, B_GROUP_W), F32)],
        compiler_params=_cparams("parallel", "arbitrary"),
        name="band_attention",
    )(bias, arr, arr, arr)


def _t5_bucket(rel):
    half = REL_BUCKETS // 2
    max_exact = half // 2
    ret = jnp.where(rel > 0, half, 0)
    n = jnp.abs(rel)
    large = max_exact + (jnp.log(jnp.maximum(n, max_exact).astype(F32) / max_exact)
                         / math.log(REL_MAX_DIST / max_exact) * (half - max_exact)).astype(jnp.int32)
    large = jnp.minimum(large, half - 1)
    return ret + jnp.where(n < max_exact, n, large)


def _band_bias_tiles(rel_bias_g, dil, L):
    kwin = min(2 * B_QBLK, L)
    i = np.arange(B_QBLK)[:, None]
    jj = np.arange(kwin)[None, :]
    tiles = []
    for off in (0, -B_HALF, B_QBLK - kwin):
        rel = jj + off - i
        band = np.abs(rel) <= B_HALF
        bucket = _t5_bucket(jnp.asarray(rel * dil, dtype=jnp.int32))
        onehot = (bucket[..., None] == jnp.arange(REL_BUCKETS, dtype=jnp.int32)).astype(F32)
        vals = jnp.einsum("ijb,bh->hij", onehot, rel_bias_g.astype(F32), precision=HIGHEST)
        tiles.append(jnp.where(jnp.asarray(band)[None], vals, NEG_INF))
    return jnp.stack(tiles).reshape(3, 2 * B_QBLK, kwin)


def _dilated_branch(bias_tiles, b0, b1, b2):
    outs = []
    for g, arr in enumerate((b0, b1, b2)):
        lead = arr.shape[:-2]
        L = arr.shape[-2]
        o, lse = _band_attention(bias_tiles[g], arr.reshape(-1, L, B_QKV_W), (0, 1, 2))
        outs.append((o.reshape(*lead, L, B_GROUP_W), lse.reshape(*lead, L, B_GROUP_W)))
    return outs


def _mla_kernel(q_ref, k_ref, v_ref, o_ref, m_sc, acc_sc, *, tk):
    kv = pl.program_id(2)

    @pl.when(kv == 0)
    def _():
        m_sc[...] = jnp.full(m_sc.shape, -jnp.inf, F32)
        acc_sc[...] = jnp.zeros(acc_sc.shape, F32)

    for h in range(D_HEADS):
        sl = slice(h * D_HEAD_PAD, (h + 1) * D_HEAD_PAD)
        s = _dot_nt(q_ref[0, :, sl], k_ref[0, :, sl])
        m_prev = m_sc[h]
        m_new = jnp.maximum(m_prev, jnp.max(s, axis=1, keepdims=True))
        alpha = jnp.exp2(m_prev - m_new)
        p = jnp.exp2(s - jnp.concatenate([m_new] * (tk // LANE), axis=1))
        acc_sc[h] = alpha * acc_sc[h] + _dot(p.astype(BF16), v_ref[0, :, sl])
        m_sc[h] = m_new

    @pl.when(kv == pl.num_programs(2) - 1)
    def _():
        _mla_store(o_ref, [acc_sc[h] for h in range(D_HEADS)])


def _mla_store(o_ref, accs):
    outs = [acc[:, :D_V] * (1.0 / acc[:, D_V:D_V + 1]) for acc in accs]
    o_ref[0] = jnp.concatenate(outs, axis=1).astype(BF16)


def _mla_single_kernel(q_ref, k_ref, v_ref, o_ref):
    accs = []
    for h in range(D_HEADS):
        sl = slice(h * D_HEAD_PAD, (h + 1) * D_HEAD_PAD)
        s = _dot_nt(q_ref[0, :, sl], k_ref[0, :, sl])
        p = jnp.exp2(s - jnp.max(s, axis=1, keepdims=True))
        accs.append(_dot(p.astype(BF16), v_ref[0, :, sl]))
    _mla_store(o_ref, accs)


def _mla_attention(q, k, v):
    B, S, _ = q.shape
    tq = min(ATT_TQ, S)
    tk = min(ATT_TK, S)
    single = tk == S
    kernel = _mla_single_kernel if single else functools.partial(_mla_kernel, tk=tk)
    scratch = [] if single else [pltpu.VMEM((D_HEADS, tq, LANE), F32), pltpu.VMEM((D_HEADS, tq, D_HEAD_PAD), F32)]
    return pl.pallas_call(
        kernel,
        grid=(B, S // tq, S // tk),
        in_specs=[
            pl.BlockSpec((1, tq, D_PAD_W), lambda b, i, j: (b, i, 0)),
            pl.BlockSpec((1, tk, D_PAD_W), lambda b, i, j: (b, j, 0)),
            pl.BlockSpec((1, tk, D_PAD_W), lambda b, i, j: (b, j, 0)),
        ],
        out_specs=pl.BlockSpec((1, tq, D_OUT_W), lambda b, i, j: (b, i, 0)),
        out_shape=jax.ShapeDtypeStruct((B, S, D_OUT_W), BF16),
        scratch_shapes=scratch,
        compiler_params=_cparams("parallel", "parallel", "arbitrary"),
        name="mla_attention",
    )(q, k, v)


def _merge_kernel(x_ref, mod_ref, g1_ref, ya_ref, ob0_ref, ob1_ref, ob2_ref, l0_ref, l1_ref, l2_ref,
                  cu_ref, cvn_ref, yd_ref, wg_ref, pa_ref, pb_ref, pc_ref, pd_ref, wo_ref,
                  ws_ref, bs_ref, o_ref, nat_sc, *, tm):
    x = x_ref[0]
    h = _modulated_norm(x, g1_ref[...], mod_ref[0, 0:1, :], mod_ref[0, 1:2, :])
    hb = h.astype(BF16)

    def gate(i):
        return jax.nn.sigmoid(_dot(hb, wg_ref[:, i * D_MODEL:(i + 1) * D_MODEL]))

    merged = gate(0) * _dot(ya_ref[0], pa_ref[...])

    for k, (o_ref_g, l_ref_g, dil) in enumerate(((ob1_ref, l1_ref, B_PAIRS[1][1]), (ob2_ref, l2_ref, B_PAIRS[2][1]))):
        for r in range(dil):
            rows = pl.ds(r, tm // dil, stride=dil)
            nat_sc.at[2 * k][rows, :] = o_ref_g[0, r].astype(F32)
            nat_sc.at[2 * k + 1][rows, :] = l_ref_g[0, r]
    l0, l1, l2 = l0_ref[0], nat_sc[1], nat_sc[3]
    mx = jnp.maximum(jnp.maximum(l0, l1), l2)
    e0, e1, e2 = jnp.exp(l0 - mx), jnp.exp(l1 - mx), jnp.exp(l2 - mx)
    yb = (e0 * ob0_ref[0].astype(F32) + e1 * nat_sc[0] + e2 * nat_sc[2]) / (e0 + e1 + e2)
    merged = merged + gate(1) * _dot(yb.astype(BF16), pb_ref[...])

    head = lax.broadcasted_iota(jnp.int32, (C_CHUNK, C_W), 1) // C_HEAD_W
    chunks = []
    for c in range(tm // (2 * C_CHUNK)):
        mm = _dot(ws_ref[...], cvn_ref[0, 2 * c * C_CHUNK:2 * (c + 1) * C_CHUNK, :])
        for half in range(2):
            rs = slice((2 * c + half) * C_CHUNK, (2 * c + half + 1) * C_CHUNK)
            mixed = bs_ref[...]
            for hh in range(C_HEADS):
                r0 = (half * C_HEADS + hh) * C_CHUNK
                mixed = mixed + jnp.where(head == hh, mm[r0:r0 + C_CHUNK], 0.0)
            chunks.append((cu_ref[0, rs, :].astype(F32) * mixed).astype(BF16))
    yc = jnp.concatenate(chunks, axis=0)
    merged = merged + gate(2) * _dot(yc, pc_ref[...])

    merged = merged + gate(3) * _dot(yd_ref[0], pd_ref[...])
    o_ref[0] = x + mod_ref[0, 2:3, :] * _dot(merged.astype(BF16), wo_ref[...])


def _merge(x, mod, lw, ya, band, cu, cvn, yd):
    B, S, _ = x.shape
    tm = MERGE_TILE
    row = lambda w: pl.BlockSpec((1, tm, w), lambda b, i: (b, i, 0))
    weights = [lw["wg"], lw["pa"], lw["pb"], lw["pc"], lw["pd"], lw["wo"], lw["ws"], lw["bs"]]
    (ob0, l0), (ob1, l1), (ob2, l2) = band
    res = lambda dil: pl.BlockSpec((1, dil, tm // dil, B_GROUP_W), lambda b, i: (b, 0, i, 0))
    r1, r2 = res(B_PAIRS[1][1]), res(B_PAIRS[2][1])
    return pl.pallas_call(
        functools.partial(_merge_kernel, tm=tm),
        grid=(B, S // tm),
        in_specs=[row(D_MODEL), pl.BlockSpec((1, N_MOD, D_MODEL), lambda b, i: (b, 0, 0)), _resident(lw["g1"].shape),
                  row(A_W), row(B_GROUP_W), r1, r2, row(B_GROUP_W), r1, r2, row(C_W), row(C_W), row(D_OUT_W)]
                 + [_resident(w.shape) for w in weights],
        out_specs=row(D_MODEL),
        out_shape=jax.ShapeDtypeStruct((B, S, D_MODEL), F32),
        scratch_shapes=[pltpu.VMEM((4, tm, B_GROUP_W), F32)],
        compiler_params=_cparams("parallel", "parallel"),
        name="merge",
    )(x, mod, lw["g1"], ya, ob0, ob1, ob2, l0, l1, l2, cu, cvn, yd, *weights)


def _mlp_kernel(x_ref, mod_ref, g2_ref, w1_ref, w2_ref, gf_ref, o_ref, *, final):
    x = x_ref[0]
    h = _modulated_norm(x, g2_ref[...], mod_ref[0, 3:4, :], mod_ref[0, 4:5, :])
    a = jnp.maximum(_dot(h.astype(BF16), w1_ref[...]), 0.0)
    y = x + mod_ref[0, 5:6, :] * _dot((a * a).astype(BF16), w2_ref[...])
    if final:
        ms = jnp.mean(y * y, axis=-1, keepdims=True)
        y = y * lax.rsqrt(ms + EPS) * gf_ref[...]
    o_ref[0] = y


def _mlp(x, mod, lw, final_g, final):
    B, S, _ = x.shape
    tm = MLP_TILE
    row = pl.BlockSpec((1, tm, D_MODEL), lambda b, i: (b, i, 0))
    return pl.pallas_call(
        functools.partial(_mlp_kernel, final=final),
        grid=(B, S // tm),
        in_specs=[row, pl.BlockSpec((1, N_MOD, D_MODEL), lambda b, i: (b, 0, 0)), _resident(lw["g2"].shape),
                  _resident(lw["w1"].shape), _resident(lw["w2"].shape), _resident(final_g.shape)],
        out_specs=row,
        out_shape=jax.ShapeDtypeStruct((B, S, D_MODEL), F32),
        compiler_params=_cparams("parallel", "parallel"),
        name="mlp",
    )(x, mod, lw["g2"], lw["w1"], lw["w2"], final_g)


def _head_slots(w, per_head, take):
    k = w.shape[0]
    wh = w.reshape(k, D_HEADS, per_head)[:, :, take]
    pad = D_HEAD_PAD - wh.shape[-1]
    return jnp.pad(wh, ((0, 0), (0, 0), (0, pad))).reshape(k, D_PAD_W)


def _rope_rot_cols(w):
    half = D_ROPE // 2
    return jnp.concatenate([-w[..., half:], w[..., :half]], axis=-1)


def _prepare_layer(l, w_in, wfft, norm1_g, q_norm_g, kv_norm_g, w_uq, w_ukv, sgu_ln_g, sgu_ln_b, sgu_w, sgu_b,
                   p_a, p_b, p_c, p_d, w_o, norm2_g, w1, w2):
    wi = w_in[l]
    o = A_W
    wb = wi[:, o:o + 3 * B_QKV_W]; o += 3 * B_QKV_W
    qkv_scale = jnp.asarray([B_HEAD_DIM ** -0.5, 1.0, 1.0], F32)[None, None, :, None]
    wb = (wb.reshape(D_MODEL, 3, len(B_PAIRS), B_GROUP_W).transpose(0, 2, 1, 3) * qkv_scale).reshape(D_MODEL, 3 * B_QKV_W)
    wc = wi[:, o:o + 2 * C_W]; o += 2 * C_W
    wdq = wi[:, o:o + D_Q_LORA]; o += D_Q_LORA
    wdkv = wi[:, o:o + D_KV_LORA]; o += D_KV_LORA
    wkr = wi[:, o:o + D_ROPE]; o += D_ROPE
    wg = wi[:, o:]

    def rope_slot(w):
        return jnp.pad(w, ((0, 0), (D_NOPE, D_HEAD_PAD - D_NOPE - D_ROPE)))

    wkr2 = jnp.concatenate([rope_slot(wkr), rope_slot(_rope_rot_cols(wkr))], axis=1)

    uq = w_uq[l].reshape(D_Q_LORA, D_HEADS, D_NOPE + D_ROPE)
    uq_rot = jnp.concatenate([jnp.zeros_like(uq[..., :D_NOPE]), _rope_rot_cols(uq[..., D_NOPE:])], axis=-1)
    padq = ((0, 0), (0, 0), (0, D_HEAD_PAD - D_NOPE - D_ROPE))
    wq2 = jnp.concatenate([jnp.pad(uq, padq).reshape(D_Q_LORA, D_PAD_W),
                           jnp.pad(uq_rot, padq).reshape(D_Q_LORA, D_PAD_W)], axis=1)

    ukv = jnp.pad(w_ukv[l], ((0, D_KV_PAD - D_KV_LORA), (0, 0)))
    wkv2 = jnp.concatenate([_head_slots(ukv, D_NOPE + D_V, slice(0, D_NOPE)),
                            _head_slots(ukv, D_NOPE + D_V, slice(D_NOPE, D_NOPE + D_V))], axis=1)

    bias_sgu = jnp.repeat(sgu_b[l].T, C_HEAD_W, axis=1)
    row = lambda v: v.reshape(1, -1).astype(F32)
    return {
        "g1": row(norm1_g[l]), "g2": row(norm2_g[l]),
        "wmain": jnp.concatenate([wfft[l], wb.astype(BF16), wc.astype(BF16), wdq.astype(BF16),
                                  jnp.pad(wdkv, ((0, 0), (0, D_KV_PAD - D_KV_LORA))).astype(BF16),
                                  wkr2.astype(BF16)], axis=1),
        "wg": wg.astype(BF16),
        "qg": row(q_norm_g[l]), "kvg": row(jnp.pad(kv_norm_g[l], (0, D_KV_PAD - D_KV_LORA))),
        "wq2": wq2.astype(BF16), "wkv2": wkv2.astype(BF16),
        "lng": row(sgu_ln_g[l]), "lnb": row(sgu_ln_b[l]),
        "ws": jnp.kron(jnp.eye(2, dtype=F32), sgu_w[l].reshape(C_HEADS * C_CHUNK, C_CHUNK)).astype(BF16),
        "bs": bias_sgu.astype(F32),
        "pa": p_a[l].astype(BF16), "pb": p_b[l].astype(BF16), "pc": p_c[l].astype(BF16),
        "pd": p_d[l].astype(BF16), "wo": w_o[l].astype(BF16),
        "w1": w1[l].astype(BF16), "w2": w2[l].astype(BF16),
    }


def _rope_tables(S):
    half = D_ROPE // 2
    inv = ROPE_THETA ** (-jnp.arange(half, dtype=F32) / half)
    ang = jnp.arange(S, dtype=F32)[:, None] * inv[None, :]
    cos = jnp.cos(ang)
    sin = jnp.sin(ang)
    pad = jnp.zeros((S, D_HEAD_PAD - D_NOPE - D_ROPE), F32)
    cos_t = jnp.concatenate([jnp.ones((S, D_NOPE), F32), cos, cos, pad], axis=1)
    sin_t = jnp.concatenate([jnp.zeros((S, D_NOPE), F32), sin, sin, pad], axis=1)
    return cos_t, sin_t


def _run_group(x, mods, layers, bias_tiles_fn, final_g):
    B, S, _ = x.shape
    cos_t, sin_t = _rope_tables(S)
    tables = _dft_tables(S)
    bias_tiles = bias_tiles_fn(S)
    ones_np = np.zeros((D_HEADS, D_HEAD_PAD), np.float32)
    ones_np[:, D_V] = 1.0
    ones_row = jnp.asarray(ones_np.reshape(1, D_PAD_W))
    gf = final_g.reshape(1, D_MODEL)
    for l, lw in enumerate(layers):
        mod = mods[l]
        y1, y2, b0, b1, b2, cu, cvn, mq, mk, mv = _inproj(x, mod, lw, cos_t, sin_t, ones_row)
        ya = _fourier_seq(tables, y1, y2)
        band = _dilated_branch(bias_tiles, b0, b1, b2)
        yd = _mla_attention(mq, mk, mv)
        x = _merge(x, mod, lw, ya, band, cu, cvn, yd)
        x = _mlp(x, mod, lw, gf, final=(l == len(layers) - 1))
    return x


def kernel(x_prompt, x_sample, c_prompt, c_sample, rel_bias, ada_w, ada_b, norm1_g, w_in, mla_q_norm_g, mla_kv_norm_g, mla_w_uq, mla_w_ukv, sgu_ln_g, sgu_ln_b, sgu_w, sgu_b, p_a, p_b, p_c, p_d, w_o, norm2_g, mlp_w1, mlp_w2, final_g):
    nb_p = c_prompt.shape[0]
    nb_s = c_sample.shape[0]
    rows = -(-(nb_p + nb_s) // 8) * 8
    c_all = jnp.concatenate([c_prompt, c_sample, jnp.zeros((rows - nb_p - nb_s, D_MODEL), F32)], axis=0)
    mod_all = _ada_mod(c_all, ada_w, ada_b).reshape(DEPTH, rows, N_MOD, D_MODEL)
    mods_p = mod_all[:, :nb_p]
    mods_s = mod_all[:, nb_p:nb_p + nb_s]

    wfft = _fold_channel_dft(w_in[:, :, :A_W])
    layers = [_prepare_layer(l, w_in, wfft, norm1_g, mla_q_norm_g, mla_kv_norm_g, mla_w_uq, mla_w_ukv,
                             sgu_ln_g, sgu_ln_b, sgu_w, sgu_b, p_a, p_b, p_c, p_d, w_o, norm2_g, mlp_w1, mlp_w2)
              for l in range(DEPTH)]

    tile_cache = {}

    def bias_tiles_fn(S):
        out = []
        for g, (_, dil) in enumerate(B_PAIRS):
            key = (g, min(2 * B_QBLK, S // dil))
            if key not in tile_cache:
                tile_cache[key] = _band_bias_tiles(rel_bias[:, 2 * g:2 * g + 2], dil, S // dil)
            out.append(tile_cache[key])
        return out

    y_prompt = _run_group(x_prompt, mods_p, layers, bias_tiles_fn, final_g)
    y_sample = _run_group(x_sample, mods_s, layers, bias_tiles_fn, final_g)
    return (y_prompt, y_sample)
```

```python
import functools
import math

import jax
import jax.numpy as jnp
import numpy as np
from jax import lax
from jax.experimental import pallas as pl
from jax.experimental.pallas import tpu as pltpu

F32 = jnp.float32
BF16 = jnp.bfloat16
HIGHEST = lax.Precision.HIGHEST

D_MODEL = 1024
DEPTH = 4
EPS = 1e-6
NEG_INF = -1e30
N_MOD = 6
A_GROUPS = 4
A_GROUP_W = 192
A_W = A_GROUPS * A_GROUP_W
B_PAIRS = ((128, 1), (512, 4), (2048, 16))
B_HEAD_DIM = 64
B_QKV_W = 384
B_GROUP_W = 128
B_HALF = 64
B_QBLK = 128
REL_BUCKETS = 32
REL_MAX_DIST = 1024
C_HEADS = 4
C_HEAD_W = 96
C_W = 384
C_CHUNK = 128
D_HEADS = 4
D_Q_LORA = 384
D_KV_LORA = 320
D_KV_PAD = 384
D_NOPE = 64
D_ROPE = 32
D_V = 64
D_HEAD_PAD = 128
D_PAD_W = D_HEADS * D_HEAD_PAD
D_OUT_W = D_HEADS * D_V
ROPE_THETA = 10000.0
LOG2_E = math.log2(math.e)

VMEM_LIMIT_BYTES = 56 * 1024 * 1024
LANE = 128
INPROJ_TILE = 512
INPROJ_SPLITS = (2 * A_W, 3 * B_QKV_W, 2 * C_W, D_Q_LORA, D_KV_PAD, 2 * D_HEAD_PAD)
MERGE_TILE = 1024
MLP_TILE = 1024
DFT_N2 = 128
DFT_KRON_COLS = 128
DFT_PITCH_PAD = 8
DFT_COPY_UNROLL = 8
DFT_MATMUL_UNROLL = 4
DFT_SUB = 16
ATT_TQ = 2048
ATT_TK = 2048
BAND_ROWS = 4096
BAND_UNROLL = 16


def _cparams(*sem):
    return pltpu.CompilerParams(dimension_semantics=sem, vmem_limit_bytes=VMEM_LIMIT_BYTES)


def _resident(shape):
    nd = len(shape)
    return pl.BlockSpec(shape, lambda *_: (0,) * nd, pipeline_mode=pl.Buffered(1))


def _dot(a, b):
    return jnp.dot(a, b, preferred_element_type=F32)


def _dot_nt(a, b):
    return lax.dot_general(a, b, (((1,), (1,)), ((), ())), preferred_element_type=F32)


def _ada_kernel(c_ref, w_ref, b_ref, o_ref):
    c = c_ref[...]
    a = c * jax.nn.sigmoid(c)
    o_ref[0] = jnp.dot(a, w_ref[0], preferred_element_type=F32, precision=HIGHEST) + b_ref[0]


def _ada_mod(c_all, ada_w, ada_b):
    rows = c_all.shape[0]
    tn = 1536
    return pl.pallas_call(
        _ada_kernel,
        grid=(DEPTH, N_MOD * D_MODEL // tn),
        in_specs=[
            pl.BlockSpec((rows, D_MODEL), lambda l, j: (0, 0)),
            pl.BlockSpec((1, D_MODEL, tn), lambda l, j: (l, 0, j)),
            pl.BlockSpec((1, 1, tn), lambda l, j: (l, 0, j)),
        ],
        out_specs=pl.BlockSpec((1, rows, tn), lambda l, j: (l, 0, j)),
        out_shape=jax.ShapeDtypeStruct((DEPTH, rows, N_MOD * D_MODEL), F32),
        compiler_params=_cparams("parallel", "parallel"),
        name="ada_mod",
    )(c_all, ada_w, ada_b.reshape(DEPTH, 1, N_MOD * D_MODEL))


def _fold_kernel(w_ref, m_ref, o_ref):
    o_ref[0] = jnp.dot(w_ref[0], m_ref[...], preferred_element_type=F32, precision=HIGHEST).astype(BF16)


def _fold_channel_dft(w_a):
    n = np.arange(A_GROUP_W)
    idx = (n[:, None] * n[None, :]) % A_GROUP_W
    ang = 2.0 * np.pi * idx / A_GROUP_W
    eye = np.eye(A_GROUPS)
    m = np.concatenate([np.kron(eye, np.cos(ang)), np.kron(eye, np.sin(ang))], axis=1).astype(np.float32)
    return pl.pallas_call(
        _fold_kernel,
        grid=(DEPTH,),
        in_specs=[
            pl.BlockSpec((1, D_MODEL, A_W), lambda l: (l, 0, 0)),
            pl.BlockSpec((A_W, 2 * A_W), lambda l: (0, 0)),
        ],
        out_specs=pl.BlockSpec((1, D_MODEL, 2 * A_W), lambda l: (l, 0, 0)),
        out_shape=jax.ShapeDtypeStruct((DEPTH, D_MODEL, 2 * A_W), BF16),
        compiler_params=_cparams("parallel"),
        name="fold_channel_dft",
    )(w_a, jnp.asarray(m))


def _modulated_norm(x, g, shift, scale):
    ms = jnp.mean(x * x, axis=-1, keepdims=True)
    return (x * lax.rsqrt(ms + EPS) * g) * (1.0 + scale) + shift


def _inproj_kernel(x_ref, mod_ref, g1_ref, cos_ref, sin_ref, wmain_ref,
                   qg_ref, kvg_ref, wq2_ref, wkv2_ref, lng_ref, lnb_ref, ones_ref,
                   y1_ref, y2_ref, b0_ref, b1_ref, b2_ref, cu_ref, cvn_ref, mq_ref, mk_ref, mv_ref, res_sc, *, tm):
    x = x_ref[0]
    h = _modulated_norm(x, g1_ref[...], mod_ref[0, 0:1, :], mod_ref[0, 1:2, :])
    hb = h.astype(BF16)

    z = _dot(hb, wmain_ref[...])
    pieces, o = [], 0
    for w in INPROJ_SPLITS:
        pieces.append(z[:, o:o + w])
        o += w
    y, zb, zc, dcq, dckv, kr = pieces

    y1_ref[0] = y[:, :A_W].astype(BF16)
    y2_ref[0] = y[:, A_W:].astype(BF16)

    b0_ref[0] = zb[:, :B_QKV_W].astype(BF16)
    for c in range(2 * B_QKV_W // LANE):
        res_sc[c] = zb[:, B_QKV_W + c * LANE:B_QKV_W + (c + 1) * LANE]
    for g, out_ref in ((1, b1_ref), (2, b2_ref)):
        dil = B_PAIRS[g][1]
        for r in range(dil):
            for c in range(3):
                out_ref[0, r, :, c * LANE:(c + 1) * LANE] = (
                    res_sc.at[3 * (g - 1) + c][pl.ds(r, tm // dil, stride=dil), :].astype(BF16))

    cu_ref[0] = zc[:, :C_W].astype(BF16)
    cv = zc[:, C_W:]
    mu = jnp.mean(cv, axis=-1, keepdims=True)
    cc = cv - mu
    var = jnp.mean(cc * cc, axis=-1, keepdims=True)
    cvn_ref[0] = (cc * lax.rsqrt(var + EPS) * lng_ref[...] + lnb_ref[...]).astype(BF16)

    cos = cos_ref[...]
    sin = sin_ref[...]
    cos4 = jnp.concatenate([cos] * D_HEADS, axis=1)
    sin4 = jnp.concatenate([sin] * D_HEADS, axis=1)

    msq = jnp.mean(dcq * dcq, axis=-1, keepdims=True)
    qn = (dcq * lax.rsqrt(msq + EPS) * qg_ref[...]).astype(BF16)
    qq = _dot(qn, wq2_ref[...])
    q = (qq[:, :D_PAD_W] * cos4 + qq[:, D_PAD_W:] * sin4) * ((D_NOPE + D_ROPE) ** -0.5 * LOG2_E)
    mq_ref[0] = q.astype(BF16)

    mskv = jnp.sum(dckv * dckv, axis=-1, keepdims=True) * (1.0 / D_KV_LORA)
    kvn = (dckv * lax.rsqrt(mskv + EPS) * kvg_ref[...]).astype(BF16)
    kv = _dot(kvn, wkv2_ref[...])
    kpe = kr[:, :D_HEAD_PAD] * cos + kr[:, D_HEAD_PAD:] * sin
    mk_ref[0] = (kv[:, :D_PAD_W] + jnp.concatenate([kpe] * D_HEADS, axis=1)).astype(BF16)
    mv_ref[0] = (kv[:, D_PAD_W:] + ones_ref[...]).astype(BF16)


def _inproj(x, mod, lw, cos_t, sin_t, ones_row):
    B, S, _ = x.shape
    tm = INPROJ_TILE
    row = lambda w: pl.BlockSpec((1, tm, w), lambda b, i: (b, i, 0))
    tab = pl.BlockSpec((tm, D_HEAD_PAD), lambda b, i: (i, 0))
    weights = [lw["wmain"], lw["qg"], lw["kvg"], lw["wq2"], lw["wkv2"], lw["lng"], lw["lnb"], ones_row]

    def nat(w):
        return row(w), jax.ShapeDtypeStruct((B, S, w), BF16)

    def residue(dil):
        return (pl.BlockSpec((1, dil, tm // dil, B_QKV_W), lambda b, i: (b, 0, i, 0)),
                jax.ShapeDtypeStruct((B, dil, S // dil, B_QKV_W), BF16))

    outs = [nat(A_W), nat(A_W), nat(B_QKV_W), residue(B_PAIRS[1][1]), residue(B_PAIRS[2][1]),
            nat(C_W), nat(C_W), nat(D_PAD_W), nat(D_PAD_W), nat(D_PAD_W)]
    return pl.pallas_call(
        functools.partial(_inproj_kernel, tm=tm),
        grid=(B, S // tm),
        in_specs=[row(D_MODEL),
                  pl.BlockSpec((1, N_MOD, D_MODEL), lambda b, i: (b, 0, 0)),
                  _resident(lw["g1"].shape), tab, tab] + [_resident(w.shape) for w in weights],
        out_specs=[o[0] for o in outs],
        out_shape=[o[1] for o in outs],
        scratch_shapes=[pltpu.VMEM((2 * B_QKV_W // LANE, tm, LANE), F32)],
        compiler_params=_cparams("parallel", "parallel"),
        name="inproj",
    )(x, mod, lw["g1"], cos_t, sin_t, *weights)


def _fft_stage1_kron_kernel(m_ref, y1_ref, y2_ref, ar_ref, ai_ref, *, n1, nsub):
    rows = n1 * DFT_SUB
    for s in range(nsub):
        js = slice(s * DFT_SUB, (s + 1) * DFT_SUB)
        rhs = jnp.concatenate([y1_ref[0, :, js, :].reshape(rows, A_W), y2_ref[0, :, js, :].reshape(rows, A_W)], axis=0)
        out = _dot(m_ref[...], rhs).astype(BF16)
        ar_ref[0, :, js, :] = out[:rows].reshape(n1, DFT_SUB, A_W)
        ai_ref[0, :, js, :] = out[rows:].reshape(n1, DFT_SUB, A_W)


def _fft_stage1_kernel(m_ref, y1_ref, y2_ref, ar_ref, ai_ref, sc, *, n1):
    nl = A_W // LANE
    pitch = n1 + DFT_PITCH_PAD

    def fill(i, carry):
        rows = pl.ds(i, DFT_SUB, stride=pitch)
        for c in range(nl):
            sl = slice(c * LANE, (c + 1) * LANE)
            sc.at[c][rows, :] = y1_ref[0, i, :, sl].astype(F32)
            sc.at[nl + c][rows, :] = y2_ref[0, i, :, sl].astype(F32)
        return carry

    lax.fori_loop(0, n1, fill, 0, unroll=DFT_COPY_UNROLL)

    def body(j, carry):
        rows = pl.ds(pl.multiple_of(j * pitch, 8), n1)
        top = jnp.concatenate([sc.at[c][rows, :] for c in range(nl)], axis=1)
        bot = jnp.concatenate([sc.at[nl + c][rows, :] for c in range(nl)], axis=1)
        out = _dot(m_ref[...], jnp.concatenate([top, bot], axis=0).astype(BF16))
        for c in range(nl):
            sc.at[c][rows, :] = out[:n1, c * LANE:(c + 1) * LANE]
            sc.at[nl + c][rows, :] = out[n1:, c * LANE:(c + 1) * LANE]
        return carry

    lax.fori_loop(0, DFT_SUB, body, 0, unroll=DFT_MATMUL_UNROLL)

    def drain(i, carry):
        rows = pl.ds(i, DFT_SUB, stride=pitch)
        for c in range(nl):
            sl = slice(c * LANE, (c + 1) * LANE)
            ar_ref[0, i, :, sl] = sc.at[c][rows, :].astype(BF16)
            ai_ref[0, i, :, sl] = sc.at[nl + c][rows, :].astype(BF16)
        return carry

    lax.fori_loop(0, n1, drain, 0, unroll=DFT_COPY_UNROLL)


def _fft_stage2_kernel(h_ref, ar_ref, ai_ref, o_ref, sc, *, n2):
    nl = A_W // LANE
    pitch = n2 + DFT_PITCH_PAD

    def body(j, carry):
        rhs = jnp.concatenate([ar_ref[0, j], ai_ref[0, j]], axis=0)
        out = _dot(h_ref[j], rhs)
        rows = pl.ds(pl.multiple_of(j * pitch, 8), n2)
        for c in range(nl):
            sc.at[c][rows, :] = out[:, c * LANE:(c + 1) * LANE]
        return carry

    lax.fori_loop(0, DFT_SUB, body, 0, unroll=DFT_MATMUL_UNROLL)

    def drain(k2, carry):
        rows = pl.ds(k2, DFT_SUB, stride=pitch)
        for c in range(nl):
            o_ref[0, k2, :, c * LANE:(c + 1) * LANE] = sc.at[c][rows, :].astype(BF16)
        return carry

    lax.fori_loop(0, n2, drain, 0, unroll=DFT_COPY_UNROLL)


def _fft_two_stage(m1, h2, y1, y2):
    B, S, _ = y1.shape
    n2 = DFT_N2
    n1 = S // n2
    nl = A_W // LANE
    if n1 == DFT_SUB:
        jb = min(n2, DFT_KRON_COLS)
        stage1 = functools.partial(_fft_stage1_kron_kernel, n1=n1, nsub=jb // DFT_SUB)
        scratch = []
    else:
        jb = DFT_SUB
        stage1 = functools.partial(_fft_stage1_kernel, n1=n1)
        scratch = [pltpu.VMEM((2 * nl, DFT_SUB * (n1 + DFT_PITCH_PAD), LANE), F32)]
    blk = pl.BlockSpec((1, n1, jb, A_W), lambda b, j: (b, 0, j, 0))
    ar, ai = pl.pallas_call(
        stage1,
        grid=(B, n2 // jb),
        in_specs=[_resident(m1.shape), blk, blk],
        out_specs=[blk, blk],
        out_shape=[jax.ShapeDtypeStruct((B, n1, n2, A_W), BF16)] * 2,
        scratch_shapes=scratch,
        compiler_params=_cparams("parallel", "parallel"),
        name="fft_stage1",
    )(m1, y1.reshape(B, n1, n2, A_W), y2.reshape(B, n1, n2, A_W))
    slab = pl.BlockSpec((1, DFT_SUB, n2, A_W), lambda b, j: (b, j, 0, 0))
    out = pl.pallas_call(
        functools.partial(_fft_stage2_kernel, n2=n2),
        grid=(B, n1 // DFT_SUB),
        in_specs=[pl.BlockSpec((DFT_SUB, n2, 2 * n2), lambda b, j: (j, 0, 0)), slab, slab],
        out_specs=pl.BlockSpec((1, n2, DFT_SUB, A_W), lambda b, j: (b, 0, j, 0)),
        out_shape=jax.ShapeDtypeStruct((B, n2, n1, A_W), BF16),
        scratch_shapes=[pltpu.VMEM((nl, DFT_SUB * (n2 + DFT_PITCH_PAD), LANE), F32)],
        compiler_params=_cparams("parallel", "parallel"),
        name="fft_stage2",
    )(h2, ar, ai)
    return out.reshape(B, S, A_W)


def _dft_tables(S):
    scale = 1.0 / math.sqrt(S * A_GROUP_W)

    def cs(idx, period):
        ang = idx.astype(F32) * (2.0 * math.pi / period)
        return jnp.cos(ang), jnp.sin(ang)

    n2 = DFT_N2
    n1 = S // n2
    a = jnp.arange(n1, dtype=jnp.int32)
    c1, s1 = cs((a[:, None] * a[None, :]) % n1, n1)
    if n1 == DFT_SUB:
        eye = jnp.eye(DFT_SUB, dtype=F32)
        c1, s1 = jnp.kron(c1, eye), jnp.kron(s1, eye)
    m1 = jnp.concatenate([jnp.concatenate([c1, -s1], axis=1), jnp.concatenate([s1, c1], axis=1)], axis=0)
    k1 = a[:, None, None]
    k2 = jnp.arange(n2, dtype=jnp.int32)[None, :, None]
    nn = jnp.arange(n2, dtype=jnp.int32)[None, None, :]
    hc, hs = cs((nn * (k1 + n1 * k2)) % S, S)
    h2 = jnp.concatenate([hc * scale, -hs * scale], axis=2)
    return {"m1": m1.astype(BF16), "h2": h2.astype(BF16)}


def _fourier_seq(tables, y1, y2):
    return _fft_two_stage(tables["m1"], tables["h2"], y1, y2)


def _band_kernel(bias_ref, q_ref, k_ref, v_ref, o_ref, lse_ref, *, seq, rows, nseq, kwin):
    nsub = rows // B_QBLK
    last_blk = seq // B_QBLK - 1
    j = pl.program_id(1)
    head0 = lax.broadcasted_iota(jnp.int32, (B_QBLK, B_GROUP_W), 1) < B_HEAD_DIM

    def body(t, carry):
        si = t // nsub
        b = t % nsub
        r0 = pl.multiple_of(b * B_QBLK, B_QBLK)
        gb = j * nsub + b
        ws = pl.multiple_of(jnp.clip(gb * B_QBLK - B_HALF, 0, seq - kwin), B_HALF)
        tix = jnp.where(gb == 0, 0, jnp.where(gb == last_blk, 2, 1))
        q = q_ref[si, pl.ds(r0, B_QBLK), :]
        kw = k_ref[si, pl.ds(ws, kwin), :]
        vw = v_ref[si, pl.ds(ws, kwin), :]
        zero = jnp.zeros_like(q)
        q2 = jnp.concatenate([jnp.where(head0, q, zero), jnp.where(head0, zero, q)], axis=0)
        s = _dot_nt(q2, kw) + bias_ref[tix]
        m = jnp.max(s, axis=1, keepdims=True)
        p = jnp.exp(s - m)
        l = jnp.sum(p, axis=1, keepdims=True)
        o2 = _dot(p.astype(BF16), vw) * (1.0 / l)
        lse2 = jnp.broadcast_to(m + jnp.log(l), (2 * B_QBLK, B_GROUP_W))
        o_ref[si, pl.ds(r0, B_QBLK), :] = jnp.where(head0, o2[:B_QBLK], o2[B_QBLK:]).astype(BF16)
        lse_ref[si, pl.ds(r0, B_QBLK), :] = jnp.where(head0, lse2[:B_QBLK], lse2[B_QBLK:])
        return carry

    lax.fori_loop(0, nseq * nsub, body, 0, unroll=BAND_UNROLL)


def _band_attention(bias, arr, cols):
    N, L, _ = arr.shape
    kwin = bias.shape[-1]
    if L >= BAND_ROWS:
        nseq, rows = 1, BAND_ROWS
    else:
        nseq, rows = min(BAND_ROWS // L, N), L
    assert N % nseq == 0 and (nseq * rows // B_QBLK) % BAND_UNROLL == 0
    cq, ck, cv = cols
    out_spec = pl.BlockSpec((nseq, rows, B_GROUP_W), lambda n, j: (n, j, 0))
    return pl.pallas_call(
        functools.partial(_band_kernel, seq=L, rows=rows, nseq=nseq, kwin=kwin),
        grid=(N // nseq, L // rows),
        in_specs=[
            _resident(bias.shape),
            pl.BlockSpec((nseq, rows, B_GROUP_W), lambda n, j: (n, j, cq)),
            pl.BlockSpec((nseq, L, B_GROUP_W), lambda n, j: (n, 0, ck)),
            pl.BlockSpec((nseq, L, B_GROUP_W), lambda n, j: (n, 0, cv)),
        ],
        out_specs=[out_spec, out_spec],
        out_shape=[jax.ShapeDtypeStruct((N, L, B_GROUP_W), BF16), jax.ShapeDtypeStruct((N, L, B_GROUP_W), F32)],
        compiler_params=_cparams("parallel", "arbitrary"),
        name="band_attention",
    )(bias, arr, arr, arr)


def _t5_bucket(rel):
    half = REL_BUCKETS // 2
    max_exact = half // 2
    ret = jnp.where(rel > 0, half, 0)
    n = jnp.abs(rel)
    large = max_exact + (jnp.log(jnp.maximum(n, max_exact).astype(F32) / max_exact)
                         / math.log(REL_MAX_DIST / max_exact) * (half - max_exact)).astype(jnp.int32)
    large = jnp.minimum(large, half - 1)
    return ret + jnp.where(n < max_exact, n, large)


def _band_bias_tiles(rel_bias_g, dil, L):
    kwin = min(2 * B_QBLK, L)
    i = np.arange(B_QBLK)[:, None]
    jj = np.arange(kwin)[None, :]
    tiles = []
    for off in (0, -B_HALF, B_QBLK - kwin):
        rel = jj + off - i
        band = np.abs(rel) <= B_HALF
        bucket = _t5_bucket(jnp.asarray(rel * dil, dtype=jnp.int32))
        onehot = (bucket[..., None] == jnp.arange(REL_BUCKETS, dtype=jnp.int32)).astype(F32)
        vals = jnp.einsum("ijb,bh->hij", onehot, rel_bias_g.astype(F32), precision=HIGHEST)
        tiles.append(jnp.where(jnp.asarray(band)[None], vals, NEG_INF))
    return jnp.stack(tiles).reshape(3, 2 * B_QBLK, kwin)


def _dilated_branch(bias_tiles, b0, b1, b2):
    outs = []
    for g, arr in enumerate((b0, b1, b2)):
        lead = arr.shape[:-2]
        L = arr.shape[-2]
        o, lse = _band_attention(bias_tiles[g], arr.reshape(-1, L, B_QKV_W), (0, 1, 2))
        outs.append((o.reshape(*lead, L, B_GROUP_W), lse.reshape(*lead, L, B_GROUP_W)))
    return outs


def _mla_kernel(q_ref, k_ref, v_ref, o_ref, m_sc, acc_sc, *, tk):
    kv = pl.program_id(2)

    @pl.when(kv == 0)
    def _():
        m_sc[...] = jnp.full(m_sc.shape, -jnp.inf, F32)
        acc_sc[...] = jnp.zeros(acc_sc.shape, F32)

    for h in range(D_HEADS):
        sl = slice(h * D_HEAD_PAD, (h + 1) * D_HEAD_PAD)
        s = _dot_nt(q_ref[0, :, sl], k_ref[0, :, sl])
        m_prev = m_sc[h]
        m_new = jnp.maximum(m_prev, jnp.max(s, axis=1, keepdims=True))
        alpha = jnp.exp2(m_prev - m_new)
        p = jnp.exp2(s - jnp.concatenate([m_new] * (tk // LANE), axis=1))
        acc_sc[h] = alpha * acc_sc[h] + _dot(p.astype(BF16), v_ref[0, :, sl])
        m_sc[h] = m_new

    @pl.when(kv == pl.num_programs(2) - 1)
    def _():
        _mla_store(o_ref, [acc_sc[h] for h in range(D_HEADS)])


def _mla_store(o_ref, accs):
    outs = [acc[:, :D_V] * (1.0 / acc[:, D_V:D_V + 1]) for acc in accs]
    o_ref[0] = jnp.concatenate(outs, axis=1).astype(BF16)


def _mla_attention(q, k, v):
    B, S, _ = q.shape
    tq = min(ATT_TQ, S)
    tk = min(ATT_TK, S)
    return pl.pallas_call(
        functools.partial(_mla_kernel, tk=tk),
        grid=(B, S // tq, S // tk),
        in_specs=[
            pl.BlockSpec((1, tq, D_PAD_W), lambda b, i, j: (b, i, 0)),
            pl.BlockSpec((1, tk, D_PAD_W), lambda b, i, j: (b, j, 0)),
            pl.BlockSpec((1, tk, D_PAD_W), lambda b, i, j: (b, j, 0)),
        ],
        out_specs=pl.BlockSpec((1, tq, D_OUT_W), lambda b, i, j: (b, i, 0)),
        out_shape=jax.ShapeDtypeStruct((B, S, D_OUT_W), BF16),
        scratch_shapes=[pltpu.VMEM((D_HEADS, tq, LANE), F32), pltpu.VMEM((D_HEADS, tq, D_HEAD_PAD), F32)],
        compiler_params=_cparams("parallel", "parallel", "arbitrary"),
        name="mla_attention",
    )(q, k, v)


def _merge_kernel(x_ref, mod_ref, g1_ref, ya_ref, ob0_ref, ob1_ref, ob2_ref, l0_ref, l1_ref, l2_ref,
                  cu_ref, cvn_ref, yd_ref, wg_ref, pa_ref, pb_ref, pc_ref, pd_ref, wo_ref,
                  ws_ref, bs_ref, o_ref, nat_sc, *, tm):
    x = x_ref[0]
    h = _modulated_norm(x, g1_ref[...], mod_ref[0, 0:1, :], mod_ref[0, 1:2, :])
    hb = h.astype(BF16)

    def gate(i):
        return jax.nn.sigmoid(_dot(hb, wg_ref[:, i * D_MODEL:(i + 1) * D_MODEL]))

    merged = gate(0) * _dot(ya_ref[0], pa_ref[...])

    for k, (o_ref_g, l_ref_g, dil) in enumerate(((ob1_ref, l1_ref, B_PAIRS[1][1]), (ob2_ref, l2_ref, B_PAIRS[2][1]))):
        for r in range(dil):
            rows = pl.ds(r, tm // dil, stride=dil)
            nat_sc.at[2 * k][rows, :] = o_ref_g[0, r].astype(F32)
            nat_sc.at[2 * k + 1][rows, :] = l_ref_g[0, r]
    l0, l1, l2 = l0_ref[0], nat_sc[1], nat_sc[3]
    mx = jnp.maximum(jnp.maximum(l0, l1), l2)
    e0, e1, e2 = jnp.exp(l0 - mx), jnp.exp(l1 - mx), jnp.exp(l2 - mx)
    yb = (e0 * ob0_ref[0].astype(F32) + e1 * nat_sc[0] + e2 * nat_sc[2]) / (e0 + e1 + e2)
    merged = merged + gate(1) * _dot(yb.astype(BF16), pb_ref[...])

    head = lax.broadcasted_iota(jnp.int32, (C_CHUNK, C_W), 1) // C_HEAD_W
    chunks = []
    for c in range(tm // (2 * C_CHUNK)):
        mm = _dot(ws_ref[...], cvn_ref[0, 2 * c * C_CHUNK:2 * (c + 1) * C_CHUNK, :])
        for half in range(2):
            rs = slice((2 * c + half) * C_CHUNK, (2 * c + half + 1) * C_CHUNK)
            mixed = bs_ref[...]
            for hh in range(C_HEADS):
                r0 = (half * C_HEADS + hh) * C_CHUNK
                mixed = mixed + jnp.where(head == hh, mm[r0:r0 + C_CHUNK], 0.0)
            chunks.append((cu_ref[0, rs, :].astype(F32) * mixed).astype(BF16))
    yc = jnp.concatenate(chunks, axis=0)
    merged = merged + gate(2) * _dot(yc, pc_ref[...])

    merged = merged + gate(3) * _dot(yd_ref[0], pd_ref[...])
    o_ref[0] = x + mod_ref[0, 2:3, :] * _dot(merged.astype(BF16), wo_ref[...])


def _merge(x, mod, lw, ya, band, cu, cvn, yd):
    B, S, _ = x.shape
    tm = MERGE_TILE
    row = lambda w: pl.BlockSpec((1, tm, w), lambda b, i: (b, i, 0))
    weights = [lw["wg"], lw["pa"], lw["pb"], lw["pc"], lw["pd"], lw["wo"], lw["ws"], lw["bs"]]
    (ob0, l0), (ob1, l1), (ob2, l2) = band
    res = lambda dil: pl.BlockSpec((1, dil, tm // dil, B_GROUP_W), lambda b, i: (b, 0, i, 0))
    r1, r2 = res(B_PAIRS[1][1]), res(B_PAIRS[2][1])
    return pl.pallas_call(
        functools.partial(_merge_kernel, tm=tm),
        grid=(B, S // tm),
        in_specs=[row(D_MODEL), pl.BlockSpec((1, N_MOD, D_MODEL), lambda b, i: (b, 0, 0)), _resident(lw["g1"].shape),
                  row(A_W), row(B_GROUP_W), r1, r2, row(B_GROUP_W), r1, r2, row(C_W), row(C_W), row(D_OUT_W)]
                 + [_resident(w.shape) for w in weights],
        out_specs=row(D_MODEL),
        out_shape=jax.ShapeDtypeStruct((B, S, D_MODEL), F32),
        scratch_shapes=[pltpu.VMEM((4, tm, B_GROUP_W), F32)],
        compiler_params=_cparams("parallel", "parallel"),
        name="merge",
    )(x, mod, lw["g1"], ya, ob0, ob1, ob2, l0, l1, l2, cu, cvn, yd, *weights)


def _mlp_kernel(x_ref, mod_ref, g2_ref, w1_ref, w2_ref, gf_ref, o_ref, *, final):
    x = x_ref[0]
    h = _modulated_norm(x, g2_ref[...], mod_ref[0, 3:4, :], mod_ref[0, 4:5, :])
    a = jnp.maximum(_dot(h.astype(BF16), w1_ref[...]), 0.0)
    y = x + mod_ref[0, 5:6, :] * _dot((a * a).astype(BF16), w2_ref[...])
    if final:
        ms = jnp.mean(y * y, axis=-1, keepdims=True)
        y = y * lax.rsqrt(ms + EPS) * gf_ref[...]
    o_ref[0] = y


def _mlp(x, mod, lw, final_g, final):
    B, S, _ = x.shape
    tm = MLP_TILE
    row = pl.BlockSpec((1, tm, D_MODEL), lambda b, i: (b, i, 0))
    return pl.pallas_call(
        functools.partial(_mlp_kernel, final=final),
        grid=(B, S // tm),
        in_specs=[row, pl.BlockSpec((1, N_MOD, D_MODEL), lambda b, i: (b, 0, 0)), _resident(lw["g2"].shape),
                  _resident(lw["w1"].shape), _resident(lw["w2"].shape), _resident(final_g.shape)],
        out_specs=row,
        out_shape=jax.ShapeDtypeStruct((B, S, D_MODEL), F32),
        compiler_params=_cparams("parallel", "parallel"),
        name="mlp",
    )(x, mod, lw["g2"], lw["w1"], lw["w2"], final_g)


def _head_slots(w, per_head, take):
    k = w.shape[0]
    wh = w.reshape(k, D_HEADS, per_head)[:, :, take]
    pad = D_HEAD_PAD - wh.shape[-1]
    return jnp.pad(wh, ((0, 0), (0, 0), (0, pad))).reshape(k, D_PAD_W)


def _rope_rot_cols(w):
    half = D_ROPE // 2
    return jnp.concatenate([-w[..., half:], w[..., :half]], axis=-1)


def _prepare_layer(l, w_in, wfft, norm1_g, q_norm_g, kv_norm_g, w_uq, w_ukv, sgu_ln_g, sgu_ln_b, sgu_w, sgu_b,
                   p_a, p_b, p_c, p_d, w_o, norm2_g, w1, w2):
    wi = w_in[l]
    o = A_W
    wb = wi[:, o:o + 3 * B_QKV_W]; o += 3 * B_QKV_W
    qkv_scale = jnp.asarray([B_HEAD_DIM ** -0.5, 1.0, 1.0], F32)[None, None, :, None]
    wb = (wb.reshape(D_MODEL, 3, len(B_PAIRS), B_GROUP_W).transpose(0, 2, 1, 3) * qkv_scale).reshape(D_MODEL, 3 * B_QKV_W)
    wc = wi[:, o:o + 2 * C_W]; o += 2 * C_W
    wdq = wi[:, o:o + D_Q_LORA]; o += D_Q_LORA
    wdkv = wi[:, o:o + D_KV_LORA]; o += D_KV_LORA
    wkr = wi[:, o:o + D_ROPE]; o += D_ROPE
    wg = wi[:, o:]

    def rope_slot(w):
        return jnp.pad(w, ((0, 0), (D_NOPE, D_HEAD_PAD - D_NOPE - D_ROPE)))

    wkr2 = jnp.concatenate([rope_slot(wkr), rope_slot(_rope_rot_cols(wkr))], axis=1)

    uq = w_uq[l].reshape(D_Q_LORA, D_HEADS, D_NOPE + D_ROPE)
    uq_rot = jnp.concatenate([jnp.zeros_like(uq[..., :D_NOPE]), _rope_rot_cols(uq[..., D_NOPE:])], axis=-1)
    padq = ((0, 0), (0, 0), (0, D_HEAD_PAD - D_NOPE - D_ROPE))
    wq2 = jnp.concatenate([jnp.pad(uq, padq).reshape(D_Q_LORA, D_PAD_W),
                           jnp.pad(uq_rot, padq).reshape(D_Q_LORA, D_PAD_W)], axis=1)

    ukv = jnp.pad(w_ukv[l], ((0, D_KV_PAD - D_KV_LORA), (0, 0)))
    wkv2 = jnp.concatenate([_head_slots(ukv, D_NOPE + D_V, slice(0, D_NOPE)),
                            _head_slots(ukv, D_NOPE + D_V, slice(D_NOPE, D_NOPE + D_V))], axis=1)

    bias_sgu = jnp.repeat(sgu_b[l].T, C_HEAD_W, axis=1)
    row = lambda v: v.reshape(1, -1).astype(F32)
    return {
        "g1": row(norm1_g[l]), "g2": row(norm2_g[l]),
        "wmain": jnp.concatenate([wfft[l], wb.astype(BF16), wc.astype(BF16), wdq.astype(BF16),
                                  jnp.pad(wdkv, ((0, 0), (0, D_KV_PAD - D_KV_LORA))).astype(BF16),
                                  wkr2.astype(BF16)], axis=1),
        "wg": wg.astype(BF16),
        "qg": row(q_norm_g[l]), "kvg": row(jnp.pad(kv_norm_g[l], (0, D_KV_PAD - D_KV_LORA))),
        "wq2": wq2.astype(BF16), "wkv2": wkv2.astype(BF16),
        "lng": row(sgu_ln_g[l]), "lnb": row(sgu_ln_b[l]),
        "ws": jnp.kron(jnp.eye(2, dtype=F32), sgu_w[l].reshape(C_HEADS * C_CHUNK, C_CHUNK)).astype(BF16),
        "bs": bias_sgu.astype(F32),
        "pa": p_a[l].astype(BF16), "pb": p_b[l].astype(BF16), "pc": p_c[l].astype(BF16),
        "pd": p_d[l].astype(BF16), "wo": w_o[l].astype(BF16),
        "w1": w1[l].astype(BF16), "w2": w2[l].astype(BF16),
    }


def _rope_tables(S):
    half = D_ROPE // 2
    inv = ROPE_THETA ** (-jnp.arange(half, dtype=F32) / half)
    ang = jnp.arange(S, dtype=F32)[:, None] * inv[None, :]
    cos = jnp.cos(ang)
    sin = jnp.sin(ang)
    pad = jnp.zeros((S, D_HEAD_PAD - D_NOPE - D_ROPE), F32)
    cos_t = jnp.concatenate([jnp.ones((S, D_NOPE), F32), cos, cos, pad], axis=1)
    sin_t = jnp.concatenate([jnp.zeros((S, D_NOPE), F32), sin, sin, pad], axis=1)
    return cos_t, sin_t


def _run_group(x, mods, layers, bias_tiles_fn, final_g):
    B, S, _ = x.shape
    cos_t, sin_t = _rope_tables(S)
    tables = _dft_tables(S)
    bias_tiles = bias_tiles_fn(S)
    ones_np = np.zeros((D_HEADS, D_HEAD_PAD), np.float32)
    ones_np[:, D_V] = 1.0
    ones_row = jnp.asarray(ones_np.reshape(1, D_PAD_W))
    gf = final_g.reshape(1, D_MODEL)
    for l, lw in enumerate(layers):
        mod = mods[l]
        y1, y2, b0, b1, b2, cu, cvn, mq, mk, mv = _inproj(x, mod, lw, cos_t, sin_t, ones_row)
        ya = _fourier_seq(tables, y1, y2)
        band = _dilated_branch(bias_tiles, b0, b1, b2)
        yd = _mla_attention(mq, mk, mv)
        x = _merge(x, mod, lw, ya, band, cu, cvn, yd)
        x = _mlp(x, mod, lw, gf, final=(l == len(layers) - 1))
    return x


def kernel(x_prompt, x_sample, c_prompt, c_sample, rel_bias, ada_w, ada_b, norm1_g, w_in, mla_q_norm_g, mla_kv_norm_g, mla_w_uq, mla_w_ukv, sgu_ln_g, sgu_ln_b, sgu_w, sgu_b, p_a, p_b, p_c, p_d, w_o, norm2_g, mlp_w1, mlp_w2, final_g):
    nb_p = c_prompt.shape[0]
    nb_s = c_sample.shape[0]
    rows = -(-(nb_p + nb_s) // 8) * 8
    c_all = jnp.concatenate([c_prompt, c_sample, jnp.zeros((rows - nb_p - nb_s, D_MODEL), F32)], axis=0)
    mod_all = _ada_mod(c_all, ada_w, ada_b).reshape(DEPTH, rows, N_MOD, D_MODEL)
    mods_p = mod_all[:, :nb_p]
    mods_s = mod_all[:, nb_p:nb_p + nb_s]

    wfft = _fold_channel_dft(w_in[:, :, :A_W])
    layers = [_prepare_layer(l, w_in, wfft, norm1_g, mla_q_norm_g, mla_kv_norm_g, mla_w_uq, mla_w_ukv,
                             sgu_ln_g, sgu_ln_b, sgu_w, sgu_b, p_a, p_b, p_c, p_d, w_o, norm2_g, mlp_w1, mlp_w2)
              for l in range(DEPTH)]

    tile_cache = {}

    def bias_tiles_fn(S):
        out = []
        for g, (_, dil) in enumerate(B_PAIRS):
            key = (g, min(2 * B_QBLK, S // dil))
            if key not in tile_cache:
                tile_cache[key] = _band_bias_tiles(rel_bias[:, 2 * g:2 * g + 2], dil, S // dil)
            out.append(tile_cache[key])
        return out

    y_prompt = _run_group(x_prompt, mods_p, layers, bias_tiles_fn, final_g)
    y_sample = _run_group(x_sample, mods_s, layers, bias_tiles_fn, final_g)
    return (y_prompt, y_sample)
```

```python
import functools
import math

import jax
import jax.numpy as jnp
import numpy as np
from jax import lax
from jax.experimental import pallas as pl
from jax.experimental.pallas import tpu as pltpu

F32 = jnp.float32
BF16 = jnp.bfloat16
HIGHEST = lax.Precision.HIGHEST

D_MODEL = 1024
DEPTH = 4
EPS = 1e-6
NEG_INF = -1e30
N_MOD = 6
A_GROUPS = 4
A_GROUP_W = 192
A_W = A_GROUPS * A_GROUP_W
B_PAIRS = ((128, 1), (512, 4), (2048, 16))
B_HEAD_DIM = 64
B_QKV_W = 384
B_GROUP_W = 128
B_HALF = 64
B_QBLK = 128
REL_BUCKETS = 32
REL_MAX_DIST = 1024
C_HEADS = 4
C_HEAD_W = 96
C_W = 384
C_CHUNK = 128
D_HEADS = 4
D_Q_LORA = 384
D_KV_LORA = 320
D_KV_PAD = 384
D_NOPE = 64
D_ROPE = 32
D_V = 64
D_HEAD_PAD = 128
D_PAD_W = D_HEADS * D_HEAD_PAD
D_OUT_W = D_HEADS * D_V
ROPE_THETA = 10000.0
LOG2_E = math.log2(math.e)

VMEM_LIMIT_BYTES = 56 * 1024 * 1024
LANE = 128
INPROJ_TILE = 512
INPROJ_SPLITS = (2 * A_W, 3 * B_QKV_W, 2 * C_W, D_Q_LORA, D_KV_PAD, 2 * D_HEAD_PAD)
MERGE_TILE = 1024
MLP_TILE = 1024
DFT_N2 = 128
DFT_PITCH_PAD = 8
DFT_COPY_UNROLL = 8
DFT_MATMUL_UNROLL = 4
DFT_SUB = 16
ATT_TQ = 2048
ATT_TK = 2048
BAND_ROWS = 4096
BAND_UNROLL = 32


def _cparams(*sem):
    return pltpu.CompilerParams(dimension_semantics=sem, vmem_limit_bytes=VMEM_LIMIT_BYTES)


def _resident(shape):
    nd = len(shape)
    return pl.BlockSpec(shape, lambda *_: (0,) * nd, pipeline_mode=pl.Buffered(1))


def _dot(a, b):
    return jnp.dot(a, b, preferred_element_type=F32)


def _dot_nt(a, b):
    return lax.dot_general(a, b, (((1,), (1,)), ((), ())), preferred_element_type=F32)


def _ada_kernel(c_ref, w_ref, b_ref, o_ref):
    c = c_ref[...]
    a = c * jax.nn.sigmoid(c)
    o_ref[0] = jnp.dot(a, w_ref[0], preferred_element_type=F32, precision=HIGHEST) + b_ref[0]


def _ada_mod(c_all, ada_w, ada_b):
    rows = c_all.shape[0]
    tn = 1536
    return pl.pallas_call(
        _ada_kernel,
        grid=(DEPTH, N_MOD * D_MODEL // tn),
        in_specs=[
            pl.BlockSpec((rows, D_MODEL), lambda l, j: (0, 0)),
            pl.BlockSpec((1, D_MODEL, tn), lambda l, j: (l, 0, j)),
            pl.BlockSpec((1, 1, tn), lambda l, j: (l, 0, j)),
        ],
        out_specs=pl.BlockSpec((1, rows, tn), lambda l, j: (l, 0, j)),
        out_shape=jax.ShapeDtypeStruct((DEPTH, rows, N_MOD * D_MODEL), F32),
        compiler_params=_cparams("parallel", "parallel"),
        name="ada_mod",
    )(c_all, ada_w, ada_b.reshape(DEPTH, 1, N_MOD * D_MODEL))


def _fold_kernel(w_ref, m_ref, o_ref):
    o_ref[0] = jnp.dot(w_ref[0], m_ref[...], preferred_element_type=F32, precision=HIGHEST).astype(BF16)


def _fold_channel_dft(w_a):
    n = np.arange(A_GROUP_W)
    idx = (n[:, None] * n[None, :]) % A_GROUP_W
    ang = 2.0 * np.pi * idx / A_GROUP_W
    eye = np.eye(A_GROUPS)
    m = np.concatenate([np.kron(eye, np.cos(ang)), np.kron(eye, np.sin(ang))], axis=1).astype(np.float32)
    return pl.pallas_call(
        _fold_kernel,
        grid=(DEPTH,),
        in_specs=[
            pl.BlockSpec((1, D_MODEL, A_W), lambda l: (l, 0, 0)),
            pl.BlockSpec((A_W, 2 * A_W), lambda l: (0, 0)),
        ],
        out_specs=pl.BlockSpec((1, D_MODEL, 2 * A_W), lambda l: (l, 0, 0)),
        out_shape=jax.ShapeDtypeStruct((DEPTH, D_MODEL, 2 * A_W), BF16),
        compiler_params=_cparams("parallel"),
        name="fold_channel_dft",
    )(w_a, jnp.asarray(m))


def _modulated_norm(x, g, shift, scale):
    ms = jnp.mean(x * x, axis=-1, keepdims=True)
    return (x * lax.rsqrt(ms + EPS) * g) * (1.0 + scale) + shift


def _inproj_kernel(x_ref, mod_ref, g1_ref, cos_ref, sin_ref, wmain_ref,
                   qg_ref, kvg_ref, wq2_ref, wkv2_ref, lng_ref, lnb_ref, ones_ref,
                   y1_ref, y2_ref, b0_ref, b1_ref, b2_ref, cu_ref, cvn_ref, mq_ref, mk_ref, mv_ref, res_sc, *, tm):
    x = x_ref[0]
    h = _modulated_norm(x, g1_ref[...], mod_ref[0, 0:1, :], mod_ref[0, 1:2, :])
    hb = h.astype(BF16)

    z = _dot(hb, wmain_ref[...])
    pieces, o = [], 0
    for w in INPROJ_SPLITS:
        pieces.append(z[:, o:o + w])
        o += w
    y, zb, zc, dcq, dckv, kr = pieces

    y1_ref[0] = y[:, :A_W].astype(BF16)
    y2_ref[0] = y[:, A_W:].astype(BF16)

    b0_ref[0] = zb[:, :B_QKV_W].astype(BF16)
    for c in range(2 * B_QKV_W // LANE):
        res_sc[c] = zb[:, B_QKV_W + c * LANE:B_QKV_W + (c + 1) * LANE]
    for g, out_ref in ((1, b1_ref), (2, b2_ref)):
        dil = B_PAIRS[g][1]
        for r in range(dil):
            for c in range(3):
                out_ref[0, r, :, c * LANE:(c + 1) * LANE] = (
                    res_sc.at[3 * (g - 1) + c][pl.ds(r, tm // dil, stride=dil), :].astype(BF16))

    cu_ref[0] = zc[:, :C_W].astype(BF16)
    cv = zc[:, C_W:]
    mu = jnp.mean(cv, axis=-1, keepdims=True)
    cc = cv - mu
    var = jnp.mean(cc * cc, axis=-1, keepdims=True)
    cvn_ref[0] = (cc * lax.rsqrt(var + EPS) * lng_ref[...] + lnb_ref[...]).astype(BF16)

    cos = cos_ref[...]
    sin = sin_ref[...]
    cos4 = jnp.concatenate([cos] * D_HEADS, axis=1)
    sin4 = jnp.concatenate([sin] * D_HEADS, axis=1)

    msq = jnp.mean(dcq * dcq, axis=-1, keepdims=True)
    qn = (dcq * lax.rsqrt(msq + EPS) * qg_ref[...]).astype(BF16)
    qq = _dot(qn, wq2_ref[...])
    q = (qq[:, :D_PAD_W] * cos4 + qq[:, D_PAD_W:] * sin4) * ((D_NOPE + D_ROPE) ** -0.5 * LOG2_E)
    mq_ref[0] = q.astype(BF16)

    mskv = jnp.sum(dckv * dckv, axis=-1, keepdims=True) * (1.0 / D_KV_LORA)
    kvn = (dckv * lax.rsqrt(mskv + EPS) * kvg_ref[...]).astype(BF16)
    kv = _dot(kvn, wkv2_ref[...])
    kpe = kr[:, :D_HEAD_PAD] * cos + kr[:, D_HEAD_PAD:] * sin
    mk_ref[0] = (kv[:, :D_PAD_W] + jnp.concatenate([kpe] * D_HEADS, axis=1)).astype(BF16)
    mv_ref[0] = (kv[:, D_PAD_W:] + ones_ref[...]).astype(BF16)


def _inproj(x, mod, lw, cos_t, sin_t, ones_row):
    B, S, _ = x.shape
    tm = INPROJ_TILE
    row = lambda w: pl.BlockSpec((1, tm, w), lambda b, i: (b, i, 0))
    tab = pl.BlockSpec((tm, D_HEAD_PAD), lambda b, i: (i, 0))
    weights = [lw["wmain"], lw["qg"], lw["kvg"], lw["wq2"], lw["wkv2"], lw["lng"], lw["lnb"], ones_row]

    def nat(w):
        return row(w), jax.ShapeDtypeStruct((B, S, w), BF16)

    def residue(dil):
        return (pl.BlockSpec((1, dil, tm // dil, B_QKV_W), lambda b, i: (b, 0, i, 0)),
                jax.ShapeDtypeStruct((B, dil, S // dil, B_QKV_W), BF16))

    outs = [nat(A_W), nat(A_W), nat(B_QKV_W), residue(B_PAIRS[1][1]), residue(B_PAIRS[2][1]),
            nat(C_W), nat(C_W), nat(D_PAD_W), nat(D_PAD_W), nat(D_PAD_W)]
    return pl.pallas_call(
        functools.partial(_inproj_kernel, tm=tm),
        grid=(B, S // tm),
        in_specs=[row(D_MODEL),
                  pl.BlockSpec((1, N_MOD, D_MODEL), lambda b, i: (b, 0, 0)),
                  _resident(lw["g1"].shape), tab, tab] + [_resident(w.shape) for w in weights],
        out_specs=[o[0] for o in outs],
        out_shape=[o[1] for o in outs],
        scratch_shapes=[pltpu.VMEM((2 * B_QKV_W // LANE, tm, LANE), F32)],
        compiler_params=_cparams("parallel", "parallel"),
        name="inproj",
    )(x, mod, lw["g1"], cos_t, sin_t, *weights)


def _fft_fused_kernel(m_ref, h_ref, y1_ref, y2_ref, o_ref, a_sc, sc, *, n1, n2):
    rows = n1 * DFT_SUB
    for s in range(n2 // DFT_SUB):
        js = slice(s * DFT_SUB, (s + 1) * DFT_SUB)
        rhs = jnp.concatenate([y1_ref[0, :, js, :].reshape(rows, A_W), y2_ref[0, :, js, :].reshape(rows, A_W)], axis=0)
        out = _dot(m_ref[...], rhs).astype(BF16)
        a_sc[0, :, js, :] = out[:rows].reshape(n1, DFT_SUB, A_W)
        a_sc[1, :, js, :] = out[rows:].reshape(n1, DFT_SUB, A_W)
    _fft_stage2(h_ref, lambda j: a_sc[0, j], lambda j: a_sc[1, j], o_ref, sc, n2)


def _fft_stage1_kernel(m_ref, y1_ref, y2_ref, ar_ref, ai_ref, sc, *, n1):
    nl = A_W // LANE
    pitch = n1 + DFT_PITCH_PAD

    def fill(i, carry):
        rows = pl.ds(i, DFT_SUB, stride=pitch)
        for c in range(nl):
            sl = slice(c * LANE, (c + 1) * LANE)
            sc.at[c][rows, :] = y1_ref[0, i, :, sl].astype(F32)
            sc.at[nl + c][rows, :] = y2_ref[0, i, :, sl].astype(F32)
        return carry

    lax.fori_loop(0, n1, fill, 0, unroll=DFT_COPY_UNROLL)

    def body(j, carry):
        rows = pl.ds(pl.multiple_of(j * pitch, 8), n1)
        top = jnp.concatenate([sc.at[c][rows, :] for c in range(nl)], axis=1)
        bot = jnp.concatenate([sc.at[nl + c][rows, :] for c in range(nl)], axis=1)
        out = _dot(m_ref[...], jnp.concatenate([top, bot], axis=0).astype(BF16))
        for c in range(nl):
            sc.at[c][rows, :] = out[:n1, c * LANE:(c + 1) * LANE]
            sc.at[nl + c][rows, :] = out[n1:, c * LANE:(c + 1) * LANE]
        return carry

    lax.fori_loop(0, DFT_SUB, body, 0, unroll=DFT_MATMUL_UNROLL)

    def drain(i, carry):
        rows = pl.ds(i, DFT_SUB, stride=pitch)
        for c in range(nl):
            sl = slice(c * LANE, (c + 1) * LANE)
            ar_ref[0, i, :, sl] = sc.at[c][rows, :].astype(BF16)
            ai_ref[0, i, :, sl] = sc.at[nl + c][rows, :].astype(BF16)
        return carry

    lax.fori_loop(0, n1, drain, 0, unroll=DFT_COPY_UNROLL)


def _fft_stage2_kernel(h_ref, ar_ref, ai_ref, o_ref, sc, *, n2):
    _fft_stage2(h_ref, lambda j: ar_ref[0, j], lambda j: ai_ref[0, j], o_ref, sc, n2)


def _fft_stage2(h_ref, get_ar, get_ai, o_ref, sc, n2):
    nl = A_W // LANE
    pitch = n2 + DFT_PITCH_PAD

    def body(j, carry):
        rhs = jnp.concatenate([get_ar(j), get_ai(j)], axis=0)
        out = _dot(h_ref[j], rhs)
        rows = pl.ds(pl.multiple_of(j * pitch, 8), n2)
        for c in range(nl):
            sc.at[c][rows, :] = out[:, c * LANE:(c + 1) * LANE]
        return carry

    lax.fori_loop(0, DFT_SUB, body, 0, unroll=DFT_MATMUL_UNROLL)

    def drain(k2, carry):
        rows = pl.ds(k2, DFT_SUB, stride=pitch)
        for c in range(nl):
            o_ref[0, k2, :, c * LANE:(c + 1) * LANE] = sc.at[c][rows, :].astype(BF16)
        return carry

    lax.fori_loop(0, n2, drain, 0, unroll=DFT_COPY_UNROLL)


def _fft_two_stage(m1, h2, y1, y2):
    B, S, _ = y1.shape
    n2 = DFT_N2
    n1 = S // n2
    nl = A_W // LANE
    stage2_scratch = pltpu.VMEM((nl, DFT_SUB * (n2 + DFT_PITCH_PAD), LANE), F32)
    if n1 == DFT_SUB:
        seq = pl.BlockSpec((1, n1, n2, A_W), lambda b: (b, 0, 0, 0))
        out = pl.pallas_call(
            functools.partial(_fft_fused_kernel, n1=n1, n2=n2),
            grid=(B,),
            in_specs=[_resident(m1.shape), _resident(h2.shape), seq, seq],
            out_specs=pl.BlockSpec((1, n2, n1, A_W), lambda b: (b, 0, 0, 0)),
            out_shape=jax.ShapeDtypeStruct((B, n2, n1, A_W), BF16),
            scratch_shapes=[pltpu.VMEM((2, n1, n2, A_W), BF16), stage2_scratch],
            compiler_params=_cparams("parallel"),
            name="fft_fused",
        )(m1, h2, y1.reshape(B, n1, n2, A_W), y2.reshape(B, n1, n2, A_W))
        return out.reshape(B, S, A_W)
    blk = pl.BlockSpec((1, n1, DFT_SUB, A_W), lambda b, j: (b, 0, j, 0))
    ar, ai = pl.pallas_call(
        functools.partial(_fft_stage1_kernel, n1=n1),
        grid=(B, n2 // DFT_SUB),
        in_specs=[_resident(m1.shape), blk, blk],
        out_specs=[blk, blk],
        out_shape=[jax.ShapeDtypeStruct((B, n1, n2, A_W), BF16)] * 2,
        scratch_shapes=[pltpu.VMEM((2 * nl, DFT_SUB * (n1 + DFT_PITCH_PAD), LANE), F32)],
        compiler_params=_cparams("parallel", "parallel"),
        name="fft_stage1",
    )(m1, y1.reshape(B, n1, n2, A_W), y2.reshape(B, n1, n2, A_W))
    slab = pl.BlockSpec((1, DFT_SUB, n2, A_W), lambda b, j: (b, j, 0, 0))
    out = pl.pallas_call(
        functools.partial(_fft_stage2_kernel, n2=n2),
        grid=(B, n1 // DFT_SUB),
        in_specs=[pl.BlockSpec((DFT_SUB, n2, 2 * n2), lambda b, j: (j, 0, 0)), slab, slab],
        out_specs=pl.BlockSpec((1, n2, DFT_SUB, A_W), lambda b, j: (b, 0, j, 0)),
        out_shape=jax.ShapeDtypeStruct((B, n2, n1, A_W), BF16),
        scratch_shapes=[stage2_scratch],
        compiler_params=_cparams("parallel", "parallel"),
        name="fft_stage2",
    )(h2, ar, ai)
    return out.reshape(B, S, A_W)


def _dft_tables(S):
    scale = 1.0 / math.sqrt(S * A_GROUP_W)

    def cs(idx, period):
        ang = idx.astype(F32) * (2.0 * math.pi / period)
        return jnp.cos(ang), jnp.sin(ang)

    n2 = DFT_N2
    n1 = S // n2
    a = jnp.arange(n1, dtype=jnp.int32)
    c1, s1 = cs((a[:, None] * a[None, :]) % n1, n1)
    if n1 == DFT_SUB:
        eye = jnp.eye(DFT_SUB, dtype=F32)
        c1, s1 = jnp.kron(c1, eye), jnp.kron(s1, eye)
    m1 = jnp.concatenate([jnp.concatenate([c1, -s1], axis=1), jnp.concatenate([s1, c1], axis=1)], axis=0)
    k1 = a[:, None, None]
    k2 = jnp.arange(n2, dtype=jnp.int32)[None, :, None]
    nn = jnp.arange(n2, dtype=jnp.int32)[None, None, :]
    hc, hs = cs((nn * (k1 + n1 * k2)) % S, S)
    h2 = jnp.concatenate([hc * scale, -hs * scale], axis=2)
    return {"m1": m1.astype(BF16), "h2": h2.astype(BF16)}


def _fourier_seq(tables, y1, y2):
    return _fft_two_stage(tables["m1"], tables["h2"], y1, y2)


def _band_kernel(bias_ref, q_ref, k_ref, v_ref, o_ref, lse_ref, *, seq, rows, nseq, kwin):
    nsub = rows // B_QBLK
    last_blk = seq // B_QBLK - 1
    j = pl.program_id(1)
    head0 = lax.broadcasted_iota(jnp.int32, (B_QBLK, B_GROUP_W), 1) < B_HEAD_DIM

    def body(t, carry):
        si = t // nsub
        b = t % nsub
        r0 = pl.multiple_of(b * B_QBLK, B_QBLK)
        gb = j * nsub + b
        ws = pl.multiple_of(jnp.clip(gb * B_QBLK - B_HALF, 0, seq - kwin), B_HALF)
        tix = jnp.where(gb == 0, 0, jnp.where(gb == last_blk, 2, 1))
        q = q_ref[si, pl.ds(r0, B_QBLK), :]
        kw = k_ref[si, pl.ds(ws, kwin), :]
        vw = v_ref[si, pl.ds(ws, kwin), :]
        zero = jnp.zeros_like(q)
        q2 = jnp.concatenate([jnp.where(head0, q, zero), jnp.where(head0, zero, q)], axis=0)
        s = _dot_nt(q2, kw) + bias_ref[tix]
        m = jnp.max(s, axis=1, keepdims=True)
        p = jnp.exp(s - m)
        l = jnp.sum(p, axis=1, keepdims=True)
        o2 = _dot(p.astype(BF16), vw) * (1.0 / l)
        lse2 = jnp.broadcast_to(m + jnp.log(l), (2 * B_QBLK, B_GROUP_W))
        o_ref[si, pl.ds(r0, B_QBLK), :] = jnp.where(head0, o2[:B_QBLK], o2[B_QBLK:]).astype(BF16)
        lse_ref[si, pl.ds(r0, B_QBLK), :] = jnp.where(head0, lse2[:B_QBLK], lse2[B_QBLK:])
        return carry

    lax.fori_loop(0, nseq * nsub, body, 0, unroll=BAND_UNROLL)


def _band_attention(bias, arr, cols):
    N, L, _ = arr.shape
    kwin = bias.shape[-1]
    if L >= BAND_ROWS:
        nseq, rows = 1, BAND_ROWS
    else:
        nseq, rows = min(BAND_ROWS // L, N), L
    assert N % nseq == 0 and (nseq * rows // B_QBLK) % BAND_UNROLL == 0
    cq, ck, cv = cols
    out_spec = pl.BlockSpec((nseq, rows, B_GROUP_W), lambda n, j: (n, j, 0))
    return pl.pallas_call(
        functools.partial(_band_kernel, seq=L, rows=rows, nseq=nseq, kwin=kwin),
        grid=(N // nseq, L // rows),
        in_specs=[
            _resident(bias.shape),
            pl.BlockSpec((nseq, rows, B_GROUP_W), lambda n, j: (n, j, cq)),
            pl.BlockSpec((nseq, L, B_GROUP_W), lambda n, j: (n, 0, ck)),
            pl.BlockSpec((nseq, L, B_GROUP_W), lambda n, j: (n, 0, cv)),
        ],
        out_specs=[out_spec, out_spec],
        out_shape=[jax.ShapeDtypeStruct((N, L, B_GROUP_W), BF16), jax.ShapeDtypeStruct((N, L, B_GROUP_W), F32)],
        compiler_params=_cparams("parallel", "arbitrary"),
        name="band_attention",
    )(bias, arr, arr, arr)


def _t5_bucket(rel):
    half = REL_BUCKETS // 2
    max_exact = half // 2
    ret = jnp.where(rel > 0, half, 0)
    n = jnp.abs(rel)
    large = max_exact + (jnp.log(jnp.maximum(n, max_exact).astype(F32) / max_exact)
                         / math.log(REL_MAX_DIST / max_exact) * (half - max_exact)).astype(jnp.int32)
    large = jnp.minimum(large, half - 1)
    return ret + jnp.where(n < max_exact, n, large)


def _band_bias_tiles(rel_bias_g, dil, L):
    kwin = min(2 * B_QBLK, L)
    i = np.arange(B_QBLK)[:, None]
    jj = np.arange(kwin)[None, :]
    tiles = []
    for off in (0, -B_HALF, B_QBLK - kwin):
        rel = jj + off - i
        band = np.abs(rel) <= B_HALF
        bucket = _t5_bucket(jnp.asarray(rel * dil, dtype=jnp.int32))
        onehot = (bucket[..., None] == jnp.arange(REL_BUCKETS, dtype=jnp.int32)).astype(F32)
        vals = jnp.einsum("ijb,bh->hij", onehot, rel_bias_g.astype(F32), precision=HIGHEST)
        tiles.append(jnp.where(jnp.asarray(band)[None], vals, NEG_INF))
    return jnp.stack(tiles).reshape(3, 2 * B_QBLK, kwin)


def _dilated_branch(bias_tiles, b0, b1, b2):
    outs = []
    for g, arr in enumerate((b0, b1, b2)):
        lead = arr.shape[:-2]
        L = arr.shape[-2]
        o, lse = _band_attention(bias_tiles[g], arr.reshape(-1, L, B_QKV_W), (0, 1, 2))
        outs.append((o.reshape(*lead, L, B_GROUP_W), lse.reshape(*lead, L, B_GROUP_W)))
    return outs


def _mla_kernel(q_ref, k_ref, v_ref, o_ref, m_sc, acc_sc, *, tk):
    kv = pl.program_id(2)

    @pl.when(kv == 0)
    def _():
        m_sc[...] = jnp.full(m_sc.shape, -jnp.inf, F32)
        acc_sc[...] = jnp.zeros(acc_sc.shape, F32)

    for h in range(D_HEADS):
        sl = slice(h * D_HEAD_PAD, (h + 1) * D_HEAD_PAD)
        s = _dot_nt(q_ref[0, :, sl], k_ref[0, :, sl])
        m_prev = m_sc[h]
        m_new = jnp.maximum(m_prev, jnp.max(s, axis=1, keepdims=True))
        alpha = jnp.exp2(m_prev - m_new)
        p = jnp.exp2(s - jnp.concatenate([m_new] * (tk // LANE), axis=1))
        acc_sc[h] = alpha * acc_sc[h] + _dot(p.astype(BF16), v_ref[0, :, sl])
        m_sc[h] = m_new

    @pl.when(kv == pl.num_programs(2) - 1)
    def _():
        _mla_store(o_ref, [acc_sc[h] for h in range(D_HEADS)])


def _mla_store(o_ref, accs):
    outs = [acc[:, :D_V] * (1.0 / acc[:, D_V:D_V + 1]) for acc in accs]
    o_ref[0] = jnp.concatenate(outs, axis=1).astype(BF16)


def _mla_attention(q, k, v):
    B, S, _ = q.shape
    tq = min(ATT_TQ, S)
    tk = min(ATT_TK, S)
    return pl.pallas_call(
        functools.partial(_mla_kernel, tk=tk),
        grid=(B, S // tq, S // tk),
        in_specs=[
            pl.BlockSpec((1, tq, D_PAD_W), lambda b, i, j: (b, i, 0)),
            pl.BlockSpec((1, tk, D_PAD_W), lambda b, i, j: (b, j, 0)),
            pl.BlockSpec((1, tk, D_PAD_W), lambda b, i, j: (b, j, 0)),
        ],
        out_specs=pl.BlockSpec((1, tq, D_OUT_W), lambda b, i, j: (b, i, 0)),
        out_shape=jax.ShapeDtypeStruct((B, S, D_OUT_W), BF16),
        scratch_shapes=[pltpu.VMEM((D_HEADS, tq, LANE), F32), pltpu.VMEM((D_HEADS, tq, D_HEAD_PAD), F32)],
        compiler_params=_cparams("parallel", "parallel", "arbitrary"),
        name="mla_attention",
    )(q, k, v)


def _merge_kernel(x_ref, mod_ref, g1_ref, ya_ref, ob0_ref, ob1_ref, ob2_ref, l0_ref, l1_ref, l2_ref,
                  cu_ref, cvn_ref, yd_ref, wg_ref, pa_ref, pb_ref, pc_ref, pd_ref, wo_ref,
                  ws_ref, bs_ref, o_ref, nat_sc, *, tm):
    x = x_ref[0]
    h = _modulated_norm(x, g1_ref[...], mod_ref[0, 0:1, :], mod_ref[0, 1:2, :])
    hb = h.astype(BF16)

    def gate(i):
        return jax.nn.sigmoid(_dot(hb, wg_ref[:, i * D_MODEL:(i + 1) * D_MODEL]))

    merged = gate(0) * _dot(ya_ref[0], pa_ref[...])

    for k, (o_ref_g, l_ref_g, dil) in enumerate(((ob1_ref, l1_ref, B_PAIRS[1][1]), (ob2_ref, l2_ref, B_PAIRS[2][1]))):
        for r in range(dil):
            rows = pl.ds(r, tm // dil, stride=dil)
            nat_sc.at[2 * k][rows, :] = o_ref_g[0, r].astype(F32)
            nat_sc.at[2 * k + 1][rows, :] = l_ref_g[0, r]
    l0, l1, l2 = l0_ref[0], nat_sc[1], nat_sc[3]
    mx = jnp.maximum(jnp.maximum(l0, l1), l2)
    e0, e1, e2 = jnp.exp(l0 - mx), jnp.exp(l1 - mx), jnp.exp(l2 - mx)
    yb = (e0 * ob0_ref[0].astype(F32) + e1 * nat_sc[0] + e2 * nat_sc[2]) / (e0 + e1 + e2)
    merged = merged + gate(1) * _dot(yb.astype(BF16), pb_ref[...])

    head = lax.broadcasted_iota(jnp.int32, (C_CHUNK, C_W), 1) // C_HEAD_W
    chunks = []
    for c in range(tm // (2 * C_CHUNK)):
        mm = _dot(ws_ref[...], cvn_ref[0, 2 * c * C_CHUNK:2 * (c + 1) * C_CHUNK, :])
        for half in range(2):
            rs = slice((2 * c + half) * C_CHUNK, (2 * c + half + 1) * C_CHUNK)
            mixed = bs_ref[...]
            for hh in range(C_HEADS):
                r0 = (half * C_HEADS + hh) * C_CHUNK
                mixed = mixed + jnp.where(head == hh, mm[r0:r0 + C_CHUNK], 0.0)
            chunks.append((cu_ref[0, rs, :].astype(F32) * mixed).astype(BF16))
    yc = jnp.concatenate(chunks, axis=0)
    merged = merged + gate(2) * _dot(yc, pc_ref[...])

    merged = merged + gate(3) * _dot(yd_ref[0], pd_ref[...])
    o_ref[0] = x + mod_ref[0, 2:3, :] * _dot(merged.astype(BF16), wo_ref[...])


def _merge(x, mod, lw, ya, band, cu, cvn, yd):
    B, S, _ = x.shape
    tm = MERGE_TILE
    row = lambda w: pl.BlockSpec((1, tm, w), lambda b, i: (b, i, 0))
    weights = [lw["wg"], lw["pa"], lw["pb"], lw["pc"], lw["pd"], lw["wo"], lw["ws"], lw["bs"]]
    (ob0, l0), (ob1, l1), (ob2, l2) = band
    res = lambda dil: pl.BlockSpec((1, dil, tm // dil, B_GROUP_W), lambda b, i: (b, 0, i, 0))
    r1, r2 = res(B_PAIRS[1][1]), res(B_PAIRS[2][1])
    return pl.pallas_call(
        functools.partial(_merge_kernel, tm=tm),
        grid=(B, S // tm),
        in_specs=[row(D_MODEL), pl.BlockSpec((1, N_MOD, D_MODEL), lambda b, i: (b, 0, 0)), _resident(lw["g1"].shape),
                  row(A_W), row(B_GROUP_W), r1, r2, row(B_GROUP_W), r1, r2, row(C_W), row(C_W), row(D_OUT_W)]
                 + [_resident(w.shape) for w in weights],
        out_specs=row(D_MODEL),
        out_shape=jax.ShapeDtypeStruct((B, S, D_MODEL), F32),
        scratch_shapes=[pltpu.VMEM((4, tm, B_GROUP_W), F32)],
        compiler_params=_cparams("parallel", "parallel"),
        name="merge",
    )(x, mod, lw["g1"], ya, ob0, ob1, ob2, l0, l1, l2, cu, cvn, yd, *weights)


def _mlp_kernel(x_ref, mod_ref, g2_ref, w1_ref, w2_ref, gf_ref, o_ref, *, final):
    x = x_ref[0]
    h = _modulated_norm(x, g2_ref[...], mod_ref[0, 3:4, :], mod_ref[0, 4:5, :])
    a = jnp.maximum(_dot(h.astype(BF16), w1_ref[...]), 0.0)
    y = x + mod_ref[0, 5:6, :] * _dot((a * a).astype(BF16), w2_ref[...])
    if final:
        ms = jnp.mean(y * y, axis=-1, keepdims=True)
        y = y * lax.rsqrt(ms + EPS) * gf_ref[...]
    o_ref[0] = y


def _mlp(x, mod, lw, final_g, final):
    B, S, _ = x.shape
    tm = MLP_TILE
    row = pl.BlockSpec((1, tm, D_MODEL), lambda b, i: (b, i, 0))
    return pl.pallas_call(
        functools.partial(_mlp_kernel, final=final),
        grid=(B, S // tm),
        in_specs=[row, pl.BlockSpec((1, N_MOD, D_MODEL), lambda b, i: (b, 0, 0)), _resident(lw["g2"].shape),
                  _resident(lw["w1"].shape), _resident(lw["w2"].shape), _resident(final_g.shape)],
        out_specs=row,
        out_shape=jax.ShapeDtypeStruct((B, S, D_MODEL), F32),
        compiler_params=_cparams("parallel", "parallel"),
        name="mlp",
    )(x, mod, lw["g2"], lw["w1"], lw["w2"], final_g)


def _head_slots(w, per_head, take):
    k = w.shape[0]
    wh = w.reshape(k, D_HEADS, per_head)[:, :, take]
    pad = D_HEAD_PAD - wh.shape[-1]
    return jnp.pad(wh, ((0, 0), (0, 0), (0, pad))).reshape(k, D_PAD_W)


def _rope_rot_cols(w):
    half = D_ROPE // 2
    return jnp.concatenate([-w[..., half:], w[..., :half]], axis=-1)


def _prepare_layer(l, w_in, wfft, norm1_g, q_norm_g, kv_norm_g, w_uq, w_ukv, sgu_ln_g, sgu_ln_b, sgu_w, sgu_b,
                   p_a, p_b, p_c, p_d, w_o, norm2_g, w1, w2):
    wi = w_in[l]
    o = A_W
    wb = wi[:, o:o + 3 * B_QKV_W]; o += 3 * B_QKV_W
    qkv_scale = jnp.asarray([B_HEAD_DIM ** -0.5, 1.0, 1.0], F32)[None, None, :, None]
    wb = (wb.reshape(D_MODEL, 3, len(B_PAIRS), B_GROUP_W).transpose(0, 2, 1, 3) * qkv_scale).reshape(D_MODEL, 3 * B_QKV_W)
    wc = wi[:, o:o + 2 * C_W]; o += 2 * C_W
    wdq = wi[:, o:o + D_Q_LORA]; o += D_Q_LORA
    wdkv = wi[:, o:o + D_KV_LORA]; o += D_KV_LORA
    wkr = wi[:, o:o + D_ROPE]; o += D_ROPE
    wg = wi[:, o:]

    def rope_slot(w):
        return jnp.pad(w, ((0, 0), (D_NOPE, D_HEAD_PAD - D_NOPE - D_ROPE)))

    wkr2 = jnp.concatenate([rope_slot(wkr), rope_slot(_rope_rot_cols(wkr))], axis=1)

    uq = w_uq[l].reshape(D_Q_LORA, D_HEADS, D_NOPE + D_ROPE)
    uq_rot = jnp.concatenate([jnp.zeros_like(uq[..., :D_NOPE]), _rope_rot_cols(uq[..., D_NOPE:])], axis=-1)
    padq = ((0, 0), (0, 0), (0, D_HEAD_PAD - D_NOPE - D_ROPE))
    wq2 = jnp.concatenate([jnp.pad(uq, padq).reshape(D_Q_LORA, D_PAD_W),
                           jnp.pad(uq_rot, padq).reshape(D_Q_LORA, D_PAD_W)], axis=1)

    ukv = jnp.pad(w_ukv[l], ((0, D_KV_PAD - D_KV_LORA), (0, 0)))
    wkv2 = jnp.concatenate([_head_slots(ukv, D_NOPE + D_V, slice(0, D_NOPE)),
                            _head_slots(ukv, D_NOPE + D_V, slice(D_NOPE, D_NOPE + D_V))], axis=1)

    bias_sgu = jnp.repeat(sgu_b[l].T, C_HEAD_W, axis=1)
    row = lambda v: v.reshape(1, -1).astype(F32)
    return {
        "g1": row(norm1_g[l]), "g2": row(norm2_g[l]),
        "wmain": jnp.concatenate([wfft[l], wb.astype(BF16), wc.astype(BF16), wdq.astype(BF16),
                                  jnp.pad(wdkv, ((0, 0), (0, D_KV_PAD - D_KV_LORA))).astype(BF16),
                                  wkr2.astype(BF16)], axis=1),
        "wg": wg.astype(BF16),
        "qg": row(q_norm_g[l]), "kvg": row(jnp.pad(kv_norm_g[l], (0, D_KV_PAD - D_KV_LORA))),
        "wq2": wq2.astype(BF16), "wkv2": wkv2.astype(BF16),
        "lng": row(sgu_ln_g[l]), "lnb": row(sgu_ln_b[l]),
        "ws": jnp.kron(jnp.eye(2, dtype=F32), sgu_w[l].reshape(C_HEADS * C_CHUNK, C_CHUNK)).astype(BF16),
        "bs": bias_sgu.astype(F32),
        "pa": p_a[l].astype(BF16), "pb": p_b[l].astype(BF16), "pc": p_c[l].astype(BF16),
        "pd": p_d[l].astype(BF16), "wo": w_o[l].astype(BF16),
        "w1": w1[l].astype(BF16), "w2": w2[l].astype(BF16),
    }


def _rope_tables(S):
    half = D_ROPE // 2
    inv = ROPE_THETA ** (-jnp.arange(half, dtype=F32) / half)
    ang = jnp.arange(S, dtype=F32)[:, None] * inv[None, :]
    cos = jnp.cos(ang)
    sin = jnp.sin(ang)
    pad = jnp.zeros((S, D_HEAD_PAD - D_NOPE - D_ROPE), F32)
    cos_t = jnp.concatenate([jnp.ones((S, D_NOPE), F32), cos, cos, pad], axis=1)
    sin_t = jnp.concatenate([jnp.zeros((S, D_NOPE), F32), sin, sin, pad], axis=1)
    return cos_t, sin_t


def _run_group(x, mods, layers, bias_tiles_fn, final_g):
    B, S, _ = x.shape
    cos_t, sin_t = _rope_tables(S)
    tables = _dft_tables(S)
    bias_tiles = bias_tiles_fn(S)
    ones_np = np.zeros((D_HEADS, D_HEAD_PAD), np.float32)
    ones_np[:, D_V] = 1.0
    ones_row = jnp.asarray(ones_np.reshape(1, D_PAD_W))
    gf = final_g.reshape(1, D_MODEL)
    for l, lw in enumerate(layers):
        mod = mods[l]
        y1, y2, b0, b1, b2, cu, cvn, mq, mk, mv = _inproj(x, mod, lw, cos_t, sin_t, ones_row)
        ya = _fourier_seq(tables, y1, y2)
        band = _dilated_branch(bias_tiles, b0, b1, b2)
        yd = _mla_attention(mq, mk, mv)
        x = _merge(x, mod, lw, ya, band, cu, cvn, yd)
        x = _mlp(x, mod, lw, gf, final=(l == len(layers) - 1))
    return x


def kernel(x_prompt, x_sample, c_prompt, c_sample, rel_bias, ada_w, ada_b, norm1_g, w_in, mla_q_norm_g, mla_kv_norm_g, mla_w_uq, mla_w_ukv, sgu_ln_g, sgu_ln_b, sgu_w, sgu_b, p_a, p_b, p_c, p_d, w_o, norm2_g, mlp_w1, mlp_w2, final_g):
    nb_p = c_prompt.shape[0]
    nb_s = c_sample.shape[0]
    rows = -(-(nb_p + nb_s) // 8) * 8
    c_all = jnp.concatenate([c_prompt, c_sample, jnp.zeros((rows - nb_p - nb_s, D_MODEL), F32)], axis=0)
    mod_all = _ada_mod(c_all, ada_w, ada_b).reshape(DEPTH, rows, N_MOD, D_MODEL)
    mods_p = mod_all[:, :nb_p]
    mods_s = mod_all[:, nb_p:nb_p + nb_s]

    wfft = _fold_channel_dft(w_in[:, :, :A_W])
    layers = [_prepare_layer(l, w_in, wfft, norm1_g, mla_q_norm_g, mla_kv_norm_g, mla_w_uq, mla_w_ukv,
                             sgu_ln_g, sgu_ln_b, sgu_w, sgu_b, p_a, p_b, p_c, p_d, w_o, norm2_g, mlp_w1, mlp_w2)
              for l in range(DEPTH)]

    tile_cache = {}

    def bias_tiles_fn(S):
        out = []
        for g, (_, dil) in enumerate(B_PAIRS):
            key = (g, min(2 * B_QBLK, S // dil))
            if key not in tile_cache:
                tile_cache[key] = _band_bias_tiles(rel_bias[:, 2 * g:2 * g + 2], dil, S // dil)
            out.append(tile_cache[key])
        return out

    y_prompt = _run_group(x_prompt, mods_p, layers, bias_tiles_fn, final_g)
    y_sample = _run_group(x_sample, mods_s, layers, bias_tiles_fn, final_g)
    return (y_prompt, y_sample)
```

```python
import functools
import math

import jax
import jax.numpy as jnp
import numpy as np
from jax import lax
from jax.experimental import pallas as pl
from jax.experimental.pallas import tpu as pltpu

F32 = jnp.float32
BF16 = jnp.bfloat16
HIGHEST = lax.Precision.HIGHEST

D_MODEL = 1024
DEPTH = 4
EPS = 1e-6
NEG_INF = -1e30
N_MOD = 6
A_GROUPS = 4
A_GROUP_W = 192
A_W = A_GROUPS * A_GROUP_W
B_PAIRS = ((128, 1), (512, 4), (2048, 16))
B_HEAD_DIM = 64
B_QKV_W = 384
B_GROUP_W = 128
B_HALF = 64
B_QBLK = 128
REL_BUCKETS = 32
REL_MAX_DIST = 1024
C_HEADS = 4
C_HEAD_W = 96
C_W = 384
C_CHUNK = 128
D_HEADS = 4
D_Q_LORA = 384
D_KV_LORA = 320
D_KV_PAD = 384
D_NOPE = 64
D_ROPE = 32
D_V = 64
D_HEAD_PAD = 128
D_PAD_W = D_HEADS * D_HEAD_PAD
D_OUT_W = D_HEADS * D_V
ROPE_THETA = 10000.0
LOG2_E = math.log2(math.e)

VMEM_LIMIT_BYTES = 56 * 1024 * 1024
LANE = 128
INPROJ_TILE = 512
INPROJ_SPLITS = (2 * A_W, 3 * B_QKV_W, 2 * C_W, D_Q_LORA, D_KV_PAD, 2 * D_HEAD_PAD)
MERGE_TILE = 1024
MLP_TILE = 1024
DFT_N2 = 128
DFT_PITCH_PAD = 8
DFT_COPY_UNROLL = 8
DFT_MATMUL_UNROLL = 4
DFT_SUB = 16
ATT_TQ = 2048
ATT_TK = 2048
BAND_ROWS = 4096
BAND_UNROLL = 32


def _cparams(*sem):
    return pltpu.CompilerParams(dimension_semantics=sem, vmem_limit_bytes=VMEM_LIMIT_BYTES)


def _resident(shape):
    nd = len(shape)
    return pl.BlockSpec(shape, lambda *_: (0,) * nd, pipeline_mode=pl.Buffered(1))


def _dot(a, b):
    return jnp.dot(a, b, preferred_element_type=F32)


def _dot_nt(a, b):
    return lax.dot_general(a, b, (((1,), (1,)), ((), ())), preferred_element_type=F32)


def _ada_kernel(c_ref, w_ref, b_ref, o_ref):
    c = c_ref[...]
    a = c * jax.nn.sigmoid(c)
    o_ref[0] = jnp.dot(a, w_ref[0], preferred_element_type=F32, precision=HIGHEST) + b_ref[0]


def _ada_mod(c_all, ada_w, ada_b):
    rows = c_all.shape[0]
    tn = 1536
    return pl.pallas_call(
        _ada_kernel,
        grid=(DEPTH, N_MOD * D_MODEL // tn),
        in_specs=[
            pl.BlockSpec((rows, D_MODEL), lambda l, j: (0, 0)),
            pl.BlockSpec((1, D_MODEL, tn), lambda l, j: (l, 0, j)),
            pl.BlockSpec((1, 1, tn), lambda l, j: (l, 0, j)),
        ],
        out_specs=pl.BlockSpec((1, rows, tn), lambda l, j: (l, 0, j)),
        out_shape=jax.ShapeDtypeStruct((DEPTH, rows, N_MOD * D_MODEL), F32),
        compiler_params=_cparams("parallel", "parallel"),
        name="ada_mod",
    )(c_all, ada_w, ada_b.reshape(DEPTH, 1, N_MOD * D_MODEL))


def _fold_kernel(w_ref, m_ref, o_ref):
    o_ref[0] = jnp.dot(w_ref[0], m_ref[...], preferred_element_type=F32, precision=HIGHEST).astype(BF16)


def _fold_channel_dft(w_a):
    n = np.arange(A_GROUP_W)
    idx = (n[:, None] * n[None, :]) % A_GROUP_W
    ang = 2.0 * np.pi * idx / A_GROUP_W
    eye = np.eye(A_GROUPS)
    m = np.concatenate([np.kron(eye, np.cos(ang)), np.kron(eye, np.sin(ang))], axis=1).astype(np.float32)
    return pl.pallas_call(
        _fold_kernel,
        grid=(DEPTH,),
        in_specs=[
            pl.BlockSpec((1, D_MODEL, A_W), lambda l: (l, 0, 0)),
            pl.BlockSpec((A_W, 2 * A_W), lambda l: (0, 0)),
        ],
        out_specs=pl.BlockSpec((1, D_MODEL, 2 * A_W), lambda l: (l, 0, 0)),
        out_shape=jax.ShapeDtypeStruct((DEPTH, D_MODEL, 2 * A_W), BF16),
        compiler_params=_cparams("parallel"),
        name="fold_channel_dft",
    )(w_a, jnp.asarray(m))


def _modulated_norm(x, g, shift, scale):
    ms = jnp.mean(x * x, axis=-1, keepdims=True)
    return (x * lax.rsqrt(ms + EPS) * g) * (1.0 + scale) + shift


def _inproj_kernel(x_ref, mod_ref, g1_ref, cos_ref, sin_ref, wmain_ref,
                   qg_ref, kvg_ref, wq2_ref, wkv2_ref, lng_ref, lnb_ref, ones_ref,
                   y1_ref, y2_ref, b0_ref, b1_ref, b2_ref, cu_ref, cvn_ref, mq_ref, mk_ref, mv_ref, res_sc, *, tm):
    x = x_ref[0]
    h = _modulated_norm(x, g1_ref[...], mod_ref[0, 0:1, :], mod_ref[0, 1:2, :])
    hb = h.astype(BF16)

    z = _dot(hb, wmain_ref[...])
    pieces, o = [], 0
    for w in INPROJ_SPLITS:
        pieces.append(z[:, o:o + w])
        o += w
    y, zb, zc, dcq, dckv, kr = pieces

    y1_ref[0] = y[:, :A_W].astype(BF16)
    y2_ref[0] = y[:, A_W:].astype(BF16)

    b0_ref[0] = zb[:, :B_QKV_W].astype(BF16)
    for c in range(2 * B_QKV_W // LANE):
        res_sc[c] = zb[:, B_QKV_W + c * LANE:B_QKV_W + (c + 1) * LANE]
    for g, out_ref in ((1, b1_ref), (2, b2_ref)):
        dil = B_PAIRS[g][1]
        for r in range(dil):
            for c in range(3):
                out_ref[0, r, :, c * LANE:(c + 1) * LANE] = (
                    res_sc.at[3 * (g - 1) + c][pl.ds(r, tm // dil, stride=dil), :].astype(BF16))

    cu_ref[0] = zc[:, :C_W].astype(BF16)
    cv = zc[:, C_W:]
    mu = jnp.mean(cv, axis=-1, keepdims=True)
    cc = cv - mu
    var = jnp.mean(cc * cc, axis=-1, keepdims=True)
    cvn_ref[0] = (cc * lax.rsqrt(var + EPS) * lng_ref[...] + lnb_ref[...]).astype(BF16)

    cos = cos_ref[...]
    sin = sin_ref[...]
    cos4 = jnp.concatenate([cos] * D_HEADS, axis=1)
    sin4 = jnp.concatenate([sin] * D_HEADS, axis=1)

    msq = jnp.mean(dcq * dcq, axis=-1, keepdims=True)
    qn = (dcq * lax.rsqrt(msq + EPS) * qg_ref[...]).astype(BF16)
    qq = _dot(qn, wq2_ref[...])
    q = (qq[:, :D_PAD_W] * cos4 + qq[:, D_PAD_W:] * sin4) * ((D_NOPE + D_ROPE) ** -0.5 * LOG2_E)
    mq_ref[0] = q.astype(BF16)

    mskv = jnp.sum(dckv * dckv, axis=-1, keepdims=True) * (1.0 / D_KV_LORA)
    kvn = (dckv * lax.rsqrt(mskv + EPS) * kvg_ref[...]).astype(BF16)
    kv = _dot(kvn, wkv2_ref[...])
    kpe = kr[:, :D_HEAD_PAD] * cos + kr[:, D_HEAD_PAD:] * sin
    mk_ref[0] = (kv[:, :D_PAD_W] + jnp.concatenate([kpe] * D_HEADS, axis=1)).astype(BF16)
    mv_ref[0] = (kv[:, D_PAD_W:] + ones_ref[...]).astype(BF16)


def _inproj(x, mod, lw, cos_t, sin_t, ones_row):
    B, S, _ = x.shape
    tm = INPROJ_TILE
    row = lambda w: pl.BlockSpec((1, tm, w), lambda b, i: (b, i, 0))
    tab = pl.BlockSpec((tm, D_HEAD_PAD), lambda b, i: (i, 0))
    weights = [lw["wmain"], lw["qg"], lw["kvg"], lw["wq2"], lw["wkv2"], lw["lng"], lw["lnb"], ones_row]

    def nat(w):
        return row(w), jax.ShapeDtypeStruct((B, S, w), BF16)

    def residue(dil):
        return (pl.BlockSpec((1, dil, tm // dil, B_QKV_W), lambda b, i: (b, 0, i, 0)),
                jax.ShapeDtypeStruct((B, dil, S // dil, B_QKV_W), BF16))

    outs = [nat(A_W), nat(A_W), nat(B_QKV_W), residue(B_PAIRS[1][1]), residue(B_PAIRS[2][1]),
            nat(C_W), nat(C_W), nat(D_PAD_W), nat(D_PAD_W), nat(D_PAD_W)]
    return pl.pallas_call(
        functools.partial(_inproj_kernel, tm=tm),
        grid=(B, S // tm),
        in_specs=[row(D_MODEL),
                  pl.BlockSpec((1, N_MOD, D_MODEL), lambda b, i: (b, 0, 0)),
                  _resident(lw["g1"].shape), tab, tab] + [_resident(w.shape) for w in weights],
        out_specs=[o[0] for o in outs],
        out_shape=[o[1] for o in outs],
        scratch_shapes=[pltpu.VMEM((2 * B_QKV_W // LANE, tm, LANE), F32)],
        compiler_params=_cparams("parallel", "parallel"),
        name="inproj",
    )(x, mod, lw["g1"], cos_t, sin_t, *weights)


def _fft_fused_kernel(m_ref, h_ref, y1_ref, y2_ref, o_ref, a_sc, sc, *, n1, n2):
    rows = n1 * DFT_SUB
    for s in range(n2 // DFT_SUB):
        js = slice(s * DFT_SUB, (s + 1) * DFT_SUB)
        rhs = jnp.concatenate([y1_ref[0, :, js, :].reshape(rows, A_W), y2_ref[0, :, js, :].reshape(rows, A_W)], axis=0)
        out = _dot(m_ref[...], rhs).astype(BF16)
        a_sc[0, :, js, :] = out[:rows].reshape(n1, DFT_SUB, A_W)
        a_sc[1, :, js, :] = out[rows:].reshape(n1, DFT_SUB, A_W)
    _fft_stage2(h_ref, lambda j: a_sc[0, j], lambda j: a_sc[1, j], o_ref, sc, n2)


def _fft_stage1_kernel(m_ref, y1_ref, y2_ref, ar_ref, ai_ref, sc, *, n1):
    nl = A_W // LANE
    pitch = n1 + DFT_PITCH_PAD

    def fill(i, carry):
        rows = pl.ds(i, DFT_SUB, stride=pitch)
        for c in range(nl):
            sl = slice(c * LANE, (c + 1) * LANE)
            sc.at[c][rows, :] = y1_ref[0, i, :, sl].astype(F32)
            sc.at[nl + c][rows, :] = y2_ref[0, i, :, sl].astype(F32)
        return carry

    lax.fori_loop(0, n1, fill, 0, unroll=DFT_COPY_UNROLL)

    def body(j, carry):
        rows = pl.ds(pl.multiple_of(j * pitch, 8), n1)
        top = jnp.concatenate([sc.at[c][rows, :] for c in range(nl)], axis=1)
        bot = jnp.concatenate([sc.at[nl + c][rows, :] for c in range(nl)], axis=1)
        out = _dot(m_ref[...], jnp.concatenate([top, bot], axis=0).astype(BF16))
        for c in range(nl):
            sc.at[c][rows, :] = out[:n1, c * LANE:(c + 1) * LANE]
            sc.at[nl + c][rows, :] = out[n1:, c * LANE:(c + 1) * LANE]
        return carry

    lax.fori_loop(0, DFT_SUB, body, 0, unroll=DFT_MATMUL_UNROLL)

    def drain(i, carry):
        rows = pl.ds(i, DFT_SUB, stride=pitch)
        for c in range(nl):
            sl = slice(c * LANE, (c + 1) * LANE)
            ar_ref[0, i, :, sl] = sc.at[c][rows, :].astype(BF16)
            ai_ref[0, i, :, sl] = sc.at[nl + c][rows, :].astype(BF16)
        return carry

    lax.fori_loop(0, n1, drain, 0, unroll=DFT_COPY_UNROLL)


def _fft_stage2_kernel(h_ref, ar_ref, ai_ref, o_ref, sc, *, n2):
    _fft_stage2(h_ref, lambda j: ar_ref[0, j], lambda j: ai_ref[0, j], o_ref, sc, n2)


def _fft_stage2(h_ref, get_ar, get_ai, o_ref, sc, n2):
    nl = A_W // LANE
    pitch = n2 + DFT_PITCH_PAD

    def body(j, carry):
        rhs = jnp.concatenate([get_ar(j), get_ai(j)], axis=0)
        out = _dot(h_ref[j], rhs)
        rows = pl.ds(pl.multiple_of(j * pitch, 8), n2)
        for c in range(nl):
            sc.at[c][rows, :] = out[:, c * LANE:(c + 1) * LANE]
        return carry

    lax.fori_loop(0, DFT_SUB, body, 0, unroll=DFT_MATMUL_UNROLL)

    def drain(k2, carry):
        rows = pl.ds(k2, DFT_SUB, stride=pitch)
        for c in range(nl):
            o_ref[0, k2, :, c * LANE:(c + 1) * LANE] = sc.at[c][rows, :].astype(BF16)
        return carry

    lax.fori_loop(0, n2, drain, 0, unroll=DFT_COPY_UNROLL)


def _fft_two_stage(m1, h2, y1, y2):
    B, S, _ = y1.shape
    n2 = DFT_N2
    n1 = S // n2
    nl = A_W // LANE
    stage2_scratch = pltpu.VMEM((nl, DFT_SUB * (n2 + DFT_PITCH_PAD), LANE), F32)
    if n1 == DFT_SUB:
        seq = pl.BlockSpec((1, n1, n2, A_W), lambda b: (b, 0, 0, 0))
        out = pl.pallas_call(
            functools.partial(_fft_fused_kernel, n1=n1, n2=n2),
            grid=(B,),
            in_specs=[_resident(m1.shape), _resident(h2.shape), seq, seq],
            out_specs=pl.BlockSpec((1, n2, n1, A_W), lambda b: (b, 0, 0, 0)),
            out_shape=jax.ShapeDtypeStruct((B, n2, n1, A_W), BF16),
            scratch_shapes=[pltpu.VMEM((2, n1, n2, A_W), BF16), stage2_scratch],
            compiler_params=_cparams("parallel"),
            name="fft_fused",
        )(m1, h2, y1.reshape(B, n1, n2, A_W), y2.reshape(B, n1, n2, A_W))
        return out.reshape(B, S, A_W)
    blk = pl.BlockSpec((1, n1, DFT_SUB, A_W), lambda b, j: (b, 0, j, 0))
    ar, ai = pl.pallas_call(
        functools.partial(_fft_stage1_kernel, n1=n1),
        grid=(B, n2 // DFT_SUB),
        in_specs=[_resident(m1.shape), blk, blk],
        out_specs=[blk, blk],
        out_shape=[jax.ShapeDtypeStruct((B, n1, n2, A_W), BF16)] * 2,
        scratch_shapes=[pltpu.VMEM((2 * nl, DFT_SUB * (n1 + DFT_PITCH_PAD), LANE), F32)],
        compiler_params=_cparams("parallel", "parallel"),
        name="fft_stage1",
    )(m1, y1.reshape(B, n1, n2, A_W), y2.reshape(B, n1, n2, A_W))
    slab = pl.BlockSpec((1, DFT_SUB, n2, A_W), lambda b, j: (b, j, 0, 0))
    out = pl.pallas_call(
        functools.partial(_fft_stage2_kernel, n2=n2),
        grid=(B, n1 // DFT_SUB),
        in_specs=[pl.BlockSpec((DFT_SUB, n2, 2 * n2), lambda b, j: (j, 0, 0)), slab, slab],
        out_specs=pl.BlockSpec((1, n2, DFT_SUB, A_W), lambda b, j: (b, 0, j, 0)),
        out_shape=jax.ShapeDtypeStruct((B, n2, n1, A_W), BF16),
        scratch_shapes=[stage2_scratch],
        compiler_params=_cparams("parallel", "parallel"),
        name="fft_stage2",
    )(h2, ar, ai)
    return out.reshape(B, S, A_W)


def _dft_tables(S):
    scale = 1.0 / math.sqrt(S * A_GROUP_W)

    def cs(idx, period):
        ang = idx.astype(F32) * (2.0 * math.pi / period)
        return jnp.cos(ang), jnp.sin(ang)

    n2 = DFT_N2
    n1 = S // n2
    a = jnp.arange(n1, dtype=jnp.int32)
    c1, s1 = cs((a[:, None] * a[None, :]) % n1, n1)
    if n1 == DFT_SUB:
        eye = jnp.eye(DFT_SUB, dtype=F32)
        c1, s1 = jnp.kron(c1, eye), jnp.kron(s1, eye)
    m1 = jnp.concatenate([jnp.concatenate([c1, -s1], axis=1), jnp.concatenate([s1, c1], axis=1)], axis=0)
    k1 = a[:, None, None]
    k2 = jnp.arange(n2, dtype=jnp.int32)[None, :, None]
    nn = jnp.arange(n2, dtype=jnp.int32)[None, None, :]
    hc, hs = cs((nn * (k1 + n1 * k2)) % S, S)
    h2 = jnp.concatenate([hc * scale, -hs * scale], axis=2)
    return {"m1": m1.astype(BF16), "h2": h2.astype(BF16)}


def _fourier_seq(tables, y1, y2):
    return _fft_two_stage(tables["m1"], tables["h2"], y1, y2)


def _band_kernel(bias_ref, q_ref, k_ref, v_ref, o_ref, lse_ref, *, seq, rows, nseq, kwin):
    nsub = rows // B_QBLK
    last_blk = seq // B_QBLK - 1
    j = pl.program_id(1)
    head0 = lax.broadcasted_iota(jnp.int32, (B_QBLK, B_GROUP_W), 1) < B_HEAD_DIM

    def body(t, carry):
        si = t // nsub
        b = t % nsub
        r0 = pl.multiple_of(b * B_QBLK, B_QBLK)
        gb = j * nsub + b
        ws = pl.multiple_of(jnp.clip(gb * B_QBLK - B_HALF, 0, seq - kwin), B_HALF)
        tix = jnp.where(gb == 0, 0, jnp.where(gb == last_blk, 2, 1))
        q = q_ref[si, pl.ds(r0, B_QBLK), :]
        kw = k_ref[si, pl.ds(ws, kwin), :]
        vw = v_ref[si, pl.ds(ws, kwin), :]
        zero = jnp.zeros_like(q)
        q2 = jnp.concatenate([jnp.where(head0, q, zero), jnp.where(head0, zero, q)], axis=0)
        s = _dot_nt(q2, kw) + bias_ref[tix]
        m = jnp.max(s, axis=1, keepdims=True)
        p = jnp.exp(s - m)
        pv = _dot(p.astype(BF16), jnp.concatenate([vw, jnp.ones_like(vw)], axis=1))
        l = pv[:, B_GROUP_W:]
        o2 = pv[:, :B_GROUP_W] * (1.0 / l)
        lse2 = m + jnp.log(l)
        o_ref[si, pl.ds(r0, B_QBLK), :] = jnp.where(head0, o2[:B_QBLK], o2[B_QBLK:]).astype(BF16)
        lse_ref[si, pl.ds(r0, B_QBLK), :] = jnp.where(head0, lse2[:B_QBLK], lse2[B_QBLK:])
        return carry

    lax.fori_loop(0, nseq * nsub, body, 0, unroll=BAND_UNROLL)


def _band_attention(bias, arr, cols):
    N, L, _ = arr.shape
    kwin = bias.shape[-1]
    if L >= BAND_ROWS:
        nseq, rows = 1, BAND_ROWS
    else:
        nseq, rows = min(BAND_ROWS // L, N), L
    assert N % nseq == 0 and (nseq * rows // B_QBLK) % BAND_UNROLL == 0
    cq, ck, cv = cols
    out_spec = pl.BlockSpec((nseq, rows, B_GROUP_W), lambda n, j: (n, j, 0))
    return pl.pallas_call(
        functools.partial(_band_kernel, seq=L, rows=rows, nseq=nseq, kwin=kwin),
        grid=(N // nseq, L // rows),
        in_specs=[
            _resident(bias.shape),
            pl.BlockSpec((nseq, rows, B_GROUP_W), lambda n, j: (n, j, cq)),
            pl.BlockSpec((nseq, L, B_GROUP_W), lambda n, j: (n, 0, ck)),
            pl.BlockSpec((nseq, L, B_GROUP_W), lambda n, j: (n, 0, cv)),
        ],
        out_specs=[out_spec, out_spec],
        out_shape=[jax.ShapeDtypeStruct((N, L, B_GROUP_W), BF16), jax.ShapeDtypeStruct((N, L, B_GROUP_W), F32)],
        compiler_params=_cparams("parallel", "arbitrary"),
        name="band_attention",
    )(bias, arr, arr, arr)


def _t5_bucket(rel):
    half = REL_BUCKETS // 2
    max_exact = half // 2
    ret = jnp.where(rel > 0, half, 0)
    n = jnp.abs(rel)
    large = max_exact + (jnp.log(jnp.maximum(n, max_exact).astype(F32) / max_exact)
                         / math.log(REL_MAX_DIST / max_exact) * (half - max_exact)).astype(jnp.int32)
    large = jnp.minimum(large, half - 1)
    return ret + jnp.where(n < max_exact, n, large)


def _band_bias_tiles(rel_bias_g, dil, L):
    kwin = min(2 * B_QBLK, L)
    i = np.arange(B_QBLK)[:, None]
    jj = np.arange(kwin)[None, :]
    tiles = []
    for off in (0, -B_HALF, B_QBLK - kwin):
        rel = jj + off - i
        band = np.abs(rel) <= B_HALF
        bucket = _t5_bucket(jnp.asarray(rel * dil, dtype=jnp.int32))
        onehot = (bucket[..., None] == jnp.arange(REL_BUCKETS, dtype=jnp.int32)).astype(F32)
        vals = jnp.einsum("ijb,bh->hij", onehot, rel_bias_g.astype(F32), precision=HIGHEST)
        tiles.append(jnp.where(jnp.asarray(band)[None], vals, NEG_INF))
    return jnp.stack(tiles).reshape(3, 2 * B_QBLK, kwin)


def _dilated_branch(bias_tiles, b0, b1, b2):
    outs = []
    for g, arr in enumerate((b0, b1, b2)):
        lead = arr.shape[:-2]
        L = arr.shape[-2]
        o, lse = _band_attention(bias_tiles[g], arr.reshape(-1, L, B_QKV_W), (0, 1, 2))
        outs.append((o.reshape(*lead, L, B_GROUP_W), lse.reshape(*lead, L, B_GROUP_W)))
    return outs


def _mla_kernel(q_ref, k_ref, v_ref, o_ref, m_sc, acc_sc, *, tk):
    kv = pl.program_id(2)

    @pl.when(kv == 0)
    def _():
        m_sc[...] = jnp.full(m_sc.shape, -jnp.inf, F32)
        acc_sc[...] = jnp.zeros(acc_sc.shape, F32)

    for h in range(D_HEADS):
        sl = slice(h * D_HEAD_PAD, (h + 1) * D_HEAD_PAD)
        s = _dot_nt(q_ref[0, :, sl], k_ref[0, :, sl])
        m_prev = m_sc[h]
        m_new = jnp.maximum(m_prev, jnp.max(s, axis=1, keepdims=True))
        alpha = jnp.exp2(m_prev - m_new)
        p = jnp.exp2(s - jnp.concatenate([m_new] * (tk // LANE), axis=1))
        acc_sc[h] = alpha * acc_sc[h] + _dot(p.astype(BF16), v_ref[0, :, sl])
        m_sc[h] = m_new

    @pl.when(kv == pl.num_programs(2) - 1)
    def _():
        _mla_store(o_ref, [acc_sc[h] for h in range(D_HEADS)])


def _mla_store(o_ref, accs):
    outs = [acc[:, :D_V] * (1.0 / acc[:, D_V:D_V + 1]) for acc in accs]
    o_ref[0] = jnp.concatenate(outs, axis=1).astype(BF16)


def _mla_attention(q, k, v):
    B, S, _ = q.shape
    tq = min(ATT_TQ, S)
    tk = min(ATT_TK, S)
    return pl.pallas_call(
        functools.partial(_mla_kernel, tk=tk),
        grid=(B, S // tq, S // tk),
        in_specs=[
            pl.BlockSpec((1, tq, D_PAD_W), lambda b, i, j: (b, i, 0)),
            pl.BlockSpec((1, tk, D_PAD_W), lambda b, i, j: (b, j, 0)),
            pl.BlockSpec((1, tk, D_PAD_W), lambda b, i, j: (b, j, 0)),
        ],
        out_specs=pl.BlockSpec((1, tq, D_OUT_W), lambda b, i, j: (b, i, 0)),
        out_shape=jax.ShapeDtypeStruct((B, S, D_OUT_W), BF16),
        scratch_shapes=[pltpu.VMEM((D_HEADS, tq, LANE), F32), pltpu.VMEM((D_HEADS, tq, D_HEAD_PAD), F32)],
        compiler_params=_cparams("parallel", "parallel", "arbitrary"),
        name="mla_attention",
    )(q, k, v)


def _merge_kernel(x_ref, mod_ref, g1_ref, ya_ref, ob0_ref, ob1_ref, ob2_ref, l0_ref, l1_ref, l2_ref,
                  cu_ref, cvn_ref, yd_ref, wg_ref, pa_ref, pb_ref, pc_ref, pd_ref, wo_ref,
                  ws_ref, bs_ref, o_ref, nat_sc, *, tm):
    x = x_ref[0]
    h = _modulated_norm(x, g1_ref[...], mod_ref[0, 0:1, :], mod_ref[0, 1:2, :])
    hb = h.astype(BF16)

    def gate(i):
        return jax.nn.sigmoid(_dot(hb, wg_ref[:, i * D_MODEL:(i + 1) * D_MODEL]))

    merged = gate(0) * _dot(ya_ref[0], pa_ref[...])

    for k, (o_ref_g, l_ref_g, dil) in enumerate(((ob1_ref, l1_ref, B_PAIRS[1][1]), (ob2_ref, l2_ref, B_PAIRS[2][1]))):
        for r in range(dil):
            rows = pl.ds(r, tm // dil, stride=dil)
            nat_sc.at[2 * k][rows, :] = o_ref_g[0, r].astype(F32)
            nat_sc.at[2 * k + 1][rows, :] = l_ref_g[0, r]
    l0, l1, l2 = l0_ref[0], nat_sc[1], nat_sc[3]
    mx = jnp.maximum(jnp.maximum(l0, l1), l2)
    e0, e1, e2 = jnp.exp(l0 - mx), jnp.exp(l1 - mx), jnp.exp(l2 - mx)
    yb = (e0 * ob0_ref[0].astype(F32) + e1 * nat_sc[0] + e2 * nat_sc[2]) / (e0 + e1 + e2)
    merged = merged + gate(1) * _dot(yb.astype(BF16), pb_ref[...])

    head = lax.broadcasted_iota(jnp.int32, (C_CHUNK, C_W), 1) // C_HEAD_W
    chunks = []
    for c in range(tm // (2 * C_CHUNK)):
        mm = _dot(ws_ref[...], cvn_ref[0, 2 * c * C_CHUNK:2 * (c + 1) * C_CHUNK, :])
        for half in range(2):
            rs = slice((2 * c + half) * C_CHUNK, (2 * c + half + 1) * C_CHUNK)
            mixed = bs_ref[...]
            for hh in range(C_HEADS):
                r0 = (half * C_HEADS + hh) * C_CHUNK
                mixed = mixed + jnp.where(head == hh, mm[r0:r0 + C_CHUNK], 0.0)
            chunks.append((cu_ref[0, rs, :].astype(F32) * mixed).astype(BF16))
    yc = jnp.concatenate(chunks, axis=0)
    merged = merged + gate(2) * _dot(yc, pc_ref[...])

    merged = merged + gate(3) * _dot(yd_ref[0], pd_ref[...])
    o_ref[0] = x + mod_ref[0, 2:3, :] * _dot(merged.astype(BF16), wo_ref[...])


def _merge(x, mod, lw, ya, band, cu, cvn, yd):
    B, S, _ = x.shape
    tm = MERGE_TILE
    row = lambda w: pl.BlockSpec((1, tm, w), lambda b, i: (b, i, 0))
    weights = [lw["wg"], lw["pa"], lw["pb"], lw["pc"], lw["pd"], lw["wo"], lw["ws"], lw["bs"]]
    (ob0, l0), (ob1, l1), (ob2, l2) = band
    res = lambda dil: pl.BlockSpec((1, dil, tm // dil, B_GROUP_W), lambda b, i: (b, 0, i, 0))
    r1, r2 = res(B_PAIRS[1][1]), res(B_PAIRS[2][1])
    return pl.pallas_call(
        functools.partial(_merge_kernel, tm=tm),
        grid=(B, S // tm),
        in_specs=[row(D_MODEL), pl.BlockSpec((1, N_MOD, D_MODEL), lambda b, i: (b, 0, 0)), _resident(lw["g1"].shape),
                  row(A_W), row(B_GROUP_W), r1, r2, row(B_GROUP_W), r1, r2, row(C_W), row(C_W), row(D_OUT_W)]
                 + [_resident(w.shape) for w in weights],
        out_specs=row(D_MODEL),
        out_shape=jax.ShapeDtypeStruct((B, S, D_MODEL), F32),
        scratch_shapes=[pltpu.VMEM((4, tm, B_GROUP_W), F32)],
        compiler_params=_cparams("parallel", "parallel"),
        name="merge",
    )(x, mod, lw["g1"], ya, ob0, ob1, ob2, l0, l1, l2, cu, cvn, yd, *weights)


def _mlp_kernel(x_ref, mod_ref, g2_ref, w1_ref, w2_ref, gf_ref, o_ref, *, final):
    x = x_ref[0]
    h = _modulated_norm(x, g2_ref[...], mod_ref[0, 3:4, :], mod_ref[0, 4:5, :])
    a = jnp.maximum(_dot(h.astype(BF16), w1_ref[...]), 0.0)
    y = x + mod_ref[0, 5:6, :] * _dot((a * a).astype(BF16), w2_ref[...])
    if final:
        ms = jnp.mean(y * y, axis=-1, keepdims=True)
        y = y * lax.rsqrt(ms + EPS) * gf_ref[...]
    o_ref[0] = y


def _mlp(x, mod, lw, final_g, final):
    B, S, _ = x.shape
    tm = MLP_TILE
    row = pl.BlockSpec((1, tm, D_MODEL), lambda b, i: (b, i, 0))
    return pl.pallas_call(
        functools.partial(_mlp_kernel, final=final),
        grid=(B, S // tm),
        in_specs=[row, pl.BlockSpec((1, N_MOD, D_MODEL), lambda b, i: (b, 0, 0)), _resident(lw["g2"].shape),
                  _resident(lw["w1"].shape), _resident(lw["w2"].shape), _resident(final_g.shape)],
        out_specs=row,
        out_shape=jax.ShapeDtypeStruct((B, S, D_MODEL), F32),
        compiler_params=_cparams("parallel", "parallel"),
        name="mlp",
    )(x, mod, lw["g2"], lw["w1"], lw["w2"], final_g)


def _head_slots(w, per_head, take):
    k = w.shape[0]
    wh = w.reshape(k, D_HEADS, per_head)[:, :, take]
    pad = D_HEAD_PAD - wh.shape[-1]
    return jnp.pad(wh, ((0, 0), (0, 0), (0, pad))).reshape(k, D_PAD_W)


def _rope_rot_cols(w):
    half = D_ROPE // 2
    return jnp.concatenate([-w[..., half:], w[..., :half]], axis=-1)


def _prepare_layer(l, w_in, wfft, norm1_g, q_norm_g, kv_norm_g, w_uq, w_ukv, sgu_ln_g, sgu_ln_b, sgu_w, sgu_b,
                   p_a, p_b, p_c, p_d, w_o, norm2_g, w1, w2):
    wi = w_in[l]
    o = A_W
    wb = wi[:, o:o + 3 * B_QKV_W]; o += 3 * B_QKV_W
    qkv_scale = jnp.asarray([B_HEAD_DIM ** -0.5, 1.0, 1.0], F32)[None, None, :, None]
    wb = (wb.reshape(D_MODEL, 3, len(B_PAIRS), B_GROUP_W).transpose(0, 2, 1, 3) * qkv_scale).reshape(D_MODEL, 3 * B_QKV_W)
    wc = wi[:, o:o + 2 * C_W]; o += 2 * C_W
    wdq = wi[:, o:o + D_Q_LORA]; o += D_Q_LORA
    wdkv = wi[:, o:o + D_KV_LORA]; o += D_KV_LORA
    wkr = wi[:, o:o + D_ROPE]; o += D_ROPE
    wg = wi[:, o:]

    def rope_slot(w):
        return jnp.pad(w, ((0, 0), (D_NOPE, D_HEAD_PAD - D_NOPE - D_ROPE)))

    wkr2 = jnp.concatenate([rope_slot(wkr), rope_slot(_rope_rot_cols(wkr))], axis=1)

    uq = w_uq[l].reshape(D_Q_LORA, D_HEADS, D_NOPE + D_ROPE)
    uq_rot = jnp.concatenate([jnp.zeros_like(uq[..., :D_NOPE]), _rope_rot_cols(uq[..., D_NOPE:])], axis=-1)
    padq = ((0, 0), (0, 0), (0, D_HEAD_PAD - D_NOPE - D_ROPE))
    wq2 = jnp.concatenate([jnp.pad(uq, padq).reshape(D_Q_LORA, D_PAD_W),
                           jnp.pad(uq_rot, padq).reshape(D_Q_LORA, D_PAD_W)], axis=1)

    ukv = jnp.pad(w_ukv[l], ((0, D_KV_PAD - D_KV_LORA), (0, 0)))
    wkv2 = jnp.concatenate([_head_slots(ukv, D_NOPE + D_V, slice(0, D_NOPE)),
                            _head_slots(ukv, D_NOPE + D_V, slice(D_NOPE, D_NOPE + D_V))], axis=1)

    bias_sgu = jnp.repeat(sgu_b[l].T, C_HEAD_W, axis=1)
    row = lambda v: v.reshape(1, -1).astype(F32)
    return {
        "g1": row(norm1_g[l]), "g2": row(norm2_g[l]),
        "wmain": jnp.concatenate([wfft[l], wb.astype(BF16), wc.astype(BF16), wdq.astype(BF16),
                                  jnp.pad(wdkv, ((0, 0), (0, D_KV_PAD - D_KV_LORA))).astype(BF16),
                                  wkr2.astype(BF16)], axis=1),
        "wg": wg.astype(BF16),
        "qg": row(q_norm_g[l]), "kvg": row(jnp.pad(kv_norm_g[l], (0, D_KV_PAD - D_KV_LORA))),
        "wq2": wq2.astype(BF16), "wkv2": wkv2.astype(BF16),
        "lng": row(sgu_ln_g[l]), "lnb": row(sgu_ln_b[l]),
        "ws": jnp.kron(jnp.eye(2, dtype=F32), sgu_w[l].reshape(C_HEADS * C_CHUNK, C_CHUNK)).astype(BF16),
        "bs": bias_sgu.astype(F32),
        "pa": p_a[l].astype(BF16), "pb": p_b[l].astype(BF16), "pc": p_c[l].astype(BF16),
        "pd": p_d[l].astype(BF16), "wo": w_o[l].astype(BF16),
        "w1": w1[l].astype(BF16), "w2": w2[l].astype(BF16),
    }


def _rope_tables(S):
    half = D_ROPE // 2
    inv = ROPE_THETA ** (-jnp.arange(half, dtype=F32) / half)
    ang = jnp.arange(S, dtype=F32)[:, None] * inv[None, :]
    cos = jnp.cos(ang)
    sin = jnp.sin(ang)
    pad = jnp.zeros((S, D_HEAD_PAD - D_NOPE - D_ROPE), F32)
    cos_t = jnp.concatenate([jnp.ones((S, D_NOPE), F32), cos, cos, pad], axis=1)
    sin_t = jnp.concatenate([jnp.zeros((S, D_NOPE), F32), sin, sin, pad], axis=1)
    return cos_t, sin_t


def _run_group(x, mods, layers, bias_tiles_fn, final_g):
    B, S, _ = x.shape
    cos_t, sin_t = _rope_tables(S)
    tables = _dft_tables(S)
    bias_tiles = bias_tiles_fn(S)
    ones_np = np.zeros((D_HEADS, D_HEAD_PAD), np.float32)
    ones_np[:, D_V] = 1.0
    ones_row = jnp.asarray(ones_np.reshape(1, D_PAD_W))
    gf = final_g.reshape(1, D_MODEL)
    for l, lw in enumerate(layers):
        mod = mods[l]
        y1, y2, b0, b1, b2, cu, cvn, mq, mk, mv = _inproj(x, mod, lw, cos_t, sin_t, ones_row)
        ya = _fourier_seq(tables, y1, y2)
        band = _dilated_branch(bias_tiles, b0, b1, b2)
        yd = _mla_attention(mq, mk, mv)
        x = _merge(x, mod, lw, ya, band, cu, cvn, yd)
        x = _mlp(x, mod, lw, gf, final=(l == len(layers) - 1))
    return x


def kernel(x_prompt, x_sample, c_prompt, c_sample, rel_bias, ada_w, ada_b, norm1_g, w_in, mla_q_norm_g, mla_kv_norm_g, mla_w_uq, mla_w_ukv, sgu_ln_g, sgu_ln_b, sgu_w, sgu_b, p_a, p_b, p_c, p_d, w_o, norm2_g, mlp_w1, mlp_w2, final_g):
    nb_p = c_prompt.shape[0]
    nb_s = c_sample.shape[0]
    rows = -(-(nb_p + nb_s) // 8) * 8
    c_all = jnp.concatenate([c_prompt, c_sample, jnp.zeros((rows - nb_p - nb_s, D_MODEL), F32)], axis=0)
    mod_all = _ada_mod(c_all, ada_w, ada_b).reshape(DEPTH, rows, N_MOD, D_MODEL)
    mods_p = mod_all[:, :nb_p]
    mods_s = mod_all[:, nb_p:nb_p + nb_s]

    wfft = _fold_channel_dft(w_in[:, :, :A_W])
    layers = [_prepare_layer(l, w_in, wfft, norm1_g, mla_q_norm_g, mla_kv_norm_g, mla_w_uq, mla_w_ukv,
                             sgu_ln_g, sgu_ln_b, sgu_w, sgu_b, p_a, p_b, p_c, p_d, w_o, norm2_g, mlp_w1, mlp_w2)
              for l in range(DEPTH)]

    tile_cache = {}

    def bias_tiles_fn(S):
        out = []
        for g, (_, dil) in enumerate(B_PAIRS):
            key = (g, min(2 * B_QBLK, S // dil))
            if key not in tile_cache:
                tile_cache[key] = _band_bias_tiles(rel_bias[:, 2 * g:2 * g + 2], dil, S // dil)
            out.append(tile_cache[key])
        return out

    y_prompt = _run_group(x_prompt, mods_p, layers, bias_tiles_fn, final_g)
    y_sample = _run_group(x_sample, mods_s, layers, bias_tiles_fn, final_g)
    return (y_prompt, y_sample)
```
